```python
import jax
import jax.numpy as jnp
from jax import lax
import numpy as np

D_MODEL = 2048
BATCH = 2
SEQ = 4096
DEPTH = 1
DEC_BATCH = 32
DEC_SEQ = 8
PAST_LEN = 16384
PAGE_SIZE = 128

D_ATTN = D_MODEL // 2
D_RNN = D_MODEL - D_ATTN
N_HEADS = 8
HEAD_DIM = D_ATTN // N_HEADS
N_KV = 2
GQA = N_HEADS // N_KV
N_BRANCH = 3
CMP_BLOCK = 32
SEL_BLOCK = 64
SEL_PER_CMP = SEL_BLOCK // CMP_BLOCK
TOP_N = 16
WINDOW = 512
Q_BLOCK = 128
FORCE_SCORE = 1e4
RNN_BLOCKS = 16
RNN_BLOCK_DIM = D_RNN // RNN_BLOCKS
CONV_W = 4
LRU_C = 8.0
N_GROUPS = 4
EXP_PER_GROUP = 4
N_EXPERTS = N_GROUPS * EXP_PER_GROUP
TOP_K_IN_GROUP = 2
D_EXPERT = D_MODEL // 4
RMS_EPS = 1e-6
KV_W = N_KV * 2 * HEAD_DIM
GATE_W = N_HEADS * N_BRANCH
D_IN = D_ATTN + 3 * KV_W + GATE_W + 2 * D_RNN
SPLIT_OFFSETS = [D_ATTN, D_ATTN + KV_W, D_ATTN + 2 * KV_W, D_ATTN + 3 * KV_W,
                 D_ATTN + 3 * KV_W + GATE_W, D_ATTN + 3 * KV_W + GATE_W + D_RNN]

kernel_name = 'hymba_nsa_rglru_hmoe_step'


def rms_norm(x, g):
    xf = x.astype(jnp.float32)
    y = xf * lax.rsqrt(jnp.mean(xf * xf, axis=-1, keepdims=True) + RMS_EPS)
    return (y * g.astype(jnp.float32)).astype(x.dtype)


def alibi_slopes():
    h = jnp.arange(1, N_HEADS + 1, dtype=jnp.float32)
    return jnp.exp2(-8.0 * h / N_HEADS).reshape(N_KV, GQA)


def masked_softmax(s, mask):
    s = jnp.where(mask, s.astype(jnp.float32), -jnp.inf)
    m = jnp.max(s, axis=-1, keepdims=True)
    m = jnp.where(jnp.isfinite(m), m, 0.0)
    p = jnp.where(mask, jnp.exp(s - m), 0.0)
    return p / jnp.maximum(jnp.sum(p, axis=-1, keepdims=True), 1e-30)


def compress_blocks(kv, w_cmp):
    b, n = kv.shape[:2]
    blk = kv.reshape(b, n // CMP_BLOCK, CMP_BLOCK, N_KV, 2, HEAD_DIM)
    return jnp.einsum('bnjgcd,jc->bngcd', blk, w_cmp.astype(kv.dtype))


def nsa_block(q, gates, t_pos, cmp_kv, sel_fetch, n_sel, win_kv, win_pos, slopes):
    b, tq = q.shape[:2]
    f32 = jnp.float32
    qg = q.reshape(b, tq, N_KV, GQA, HEAD_DIM) * (HEAD_DIM ** -0.5)
    sl = slopes[None, None, :, :, None]
    nc = cmp_kv.shape[1]
    c_end = (jnp.arange(nc) + 1) * CMP_BLOCK - 1
    dist_c = t_pos[:, None] - c_end[None, :]
    s_c = jnp.einsum('btgrd,bngd->btgrn', qg, cmp_kv[..., 0, :]).astype(f32)
    s_c = s_c - sl * dist_c.astype(f32)[None, :, None, None, :]
    p_c = masked_softmax(s_c, (dist_c >= 0)[None, :, None, None, :])
    o_c = jnp.einsum('btgrn,bngd->btgrd', p_c.astype(cmp_kv.dtype), cmp_kv[..., 1, :])
    imp = jnp.pad(p_c.sum(axis=3), ((0, 0), (0, 0), (0, 0), (0, n_sel * SEL_PER_CMP - nc)))
    imp = imp.reshape(b, tq, N_KV, n_sel, SEL_PER_CMP).sum(-1)
    blk = jnp.arange(n_sel)[None, :]
    cur = (t_pos // SEL_BLOCK)[:, None]
    forced = (blk == 0) | (blk == cur) | (blk == cur - 1)
    imp = jnp.where(forced[None, :, None, :], FORCE_SCORE, imp)
    imp = jnp.where((blk > cur)[None, :, None, :], -jnp.inf, imp)
    k_top = min(TOP_N, n_sel)
    _, idx = lax.top_k(imp, k_top)
    kv_s = sel_fetch(idx).reshape(b, tq, N_KV, k_top * SEL_BLOCK, 2, HEAD_DIM)
    pos_s = (idx[..., None] * SEL_BLOCK + jnp.arange(SEL_BLOCK)).reshape(b, tq, N_KV, k_top * SEL_BLOCK)
    dist_s = t_pos[None, :, None, None] - pos_s
    s_s = jnp.einsum('btgrd,btgkd->btgrk', qg, kv_s[..., 0, :]).astype(f32)
    s_s = s_s - sl * dist_s.astype(f32)[:, :, :, None, :]
    p_s = masked_softmax(s_s, (dist_s >= 0)[:, :, :, None, :])
    o_s = jnp.einsum('btgrk,btgkd->btgrd', p_s.astype(kv_s.dtype), kv_s[..., 1, :])
    dist_w = t_pos[:, None] - win_pos[None, :]
    mask_w = (dist_w >= 0) & (dist_w <= WINDOW) & (win_pos >= 0)[None, :]
    s_w = jnp.einsum('btgrd,bkgd->btgrk', qg, win_kv[..., 0, :]).astype(f32)
    s_w = s_w - sl * dist_w.astype(f32)[None, :, None, None, :]
    p_w = masked_softmax(s_w, mask_w[None, :, None, None, :])
    o_w = jnp.einsum('btgrk,bkgd->btgrd', p_w.astype(win_kv.dtype), win_kv[..., 1, :])
    g = jax.nn.sigmoid(gates.astype(f32)).reshape(b, tq, N_KV, GQA, N_BRANCH)
    o = g[..., 0:1] * o_c + g[..., 1:2] * o_s + g[..., 2:3] * o_w
    return o.reshape(b, tq, D_ATTN).astype(q.dtype)


def nsa_prompt(q, gates, kv_c, kv_s, kv_w, w_cmp, slopes):
    b, s = q.shape[:2]
    cmp_kv = compress_blocks(kv_c, w_cmp)
    n_sel = s // SEL_BLOCK
    sel_blocks = kv_s.reshape(b, n_sel, SEL_BLOCK, N_KV, 2, HEAD_DIM)
    b_idx = jnp.arange(b)[:, None, None, None]
    g_idx = jnp.arange(N_KV)[None, None, :, None]

    def fetch(idx):
        return sel_blocks[b_idx, idx, :, g_idx]

    win_pad = jnp.pad(kv_w, ((0, 0), (WINDOW, 0), (0, 0), (0, 0), (0, 0)))
    nq = s // Q_BLOCK
    q_b = q.reshape(b, nq, Q_BLOCK, N_HEADS, HEAD_DIM).swapaxes(0, 1)
    g_b = gates.reshape(b, nq, Q_BLOCK, N_HEADS, N_BRANCH).swapaxes(0, 1)

    def one_block(args):
        i, q_i, g_i = args
        start = i * Q_BLOCK
        t_pos = start + jnp.arange(Q_BLOCK)
        win = lax.dynamic_slice_in_dim(win_pad, start, WINDOW + Q_BLOCK, axis=1)
        win_pos = start - WINDOW + jnp.arange(WINDOW + Q_BLOCK)
        return nsa_block(q_i, g_i, t_pos, cmp_kv, fetch, n_sel, win, win_pos, slopes)

    out = lax.map(one_block, (jnp.arange(nq), q_b, g_b))
    return out.swapaxes(0, 1).reshape(b, s, D_ATTN)


def nsa_sample(q, gates, kv_c, kv_s, kv_w, cache_c, cache_s, cache_w, page_table, w_cmp, slopes):
    b, t = q.shape[:2]
    past = page_table.shape[1] * PAGE_SIZE
    cmp_kv = compress_blocks(cache_c[page_table].reshape(b, past, N_KV, 2, HEAD_DIM), w_cmp)
    n_new_c = t // CMP_BLOCK
    if n_new_c > 0:
        cmp_kv = jnp.concatenate([cmp_kv, compress_blocks(kv_c[:, :n_new_c * CMP_BLOCK], w_cmp)], axis=1)
    n_sel = -(-(past + t) // SEL_BLOCK)
    n_past_sel = past // SEL_BLOCK
    n_new_sel = n_sel - n_past_sel
    sel_per_page = PAGE_SIZE // SEL_BLOCK
    pool_blocks = cache_s.reshape(-1, SEL_BLOCK, N_KV, 2, HEAD_DIM)
    new_blocks = jnp.pad(kv_s, ((0, 0), (0, n_new_sel * SEL_BLOCK - t), (0, 0), (0, 0), (0, 0)))
    new_blocks = new_blocks.reshape(b, n_new_sel, SEL_BLOCK, N_KV, 2, HEAD_DIM)
    b_idx = jnp.arange(b)[:, None, None, None]
    g_idx = jnp.arange(N_KV)[None, None, :, None]

    def fetch(idx):
        lp = jnp.minimum(idx, n_past_sel - 1)
        phys = page_table[b_idx, lp // sel_per_page] * sel_per_page + lp % sel_per_page
        from_pool = pool_blocks[phys, :, g_idx]
        from_new = new_blocks[b_idx, jnp.clip(idx - n_past_sel, 0, n_new_sel - 1), :, g_idx]
        return jnp.where((idx < n_past_sel)[..., None, None, None], from_pool, from_new)

    n_win = cache_w.shape[1]
    win = jnp.concatenate([cache_w, kv_w], axis=1)
    win_pos = past - n_win + jnp.arange(n_win + t)
    t_pos = past + jnp.arange(t)
    out = nsa_block(q, gates, t_pos, cmp_kv, fetch, n_sel, win, win_pos, slopes)
    return out, win[:, t:]


def causal_conv(x, buf, w, bias):
    t = x.shape[1]
    xp = jnp.concatenate([buf, x], axis=1)
    y = bias + xp[:, 0:t] * w[0]
    for k in range(1, CONV_W):
        y = y + xp[:, k:k + t] * w[k]
    return y, xp[:, t:]


def _lin_comb(e1, e2):
    a1, b1 = e1
    a2, b2 = e2
    return a1 * a2, a2 * b1 + b2


def rg_lru(x, h0, w_a, b_a, w_x, b_x, lam):
    b, t, c = x.shape
    f32 = jnp.float32
    xb = x.reshape(b, t, RNN_BLOCKS, RNN_BLOCK_DIM)
    r = jax.nn.sigmoid((jnp.einsum('btnd,nde->btne', xb, w_a).reshape(b, t, c) + b_a).astype(f32))
    i = jax.nn.sigmoid((jnp.einsum('btnd,nde->btne', xb, w_x).reshape(b, t, c) + b_x).astype(f32))
    log_a = -LRU_C * r * jax.nn.softplus(-lam.astype(f32))
    a = jnp.exp(log_a)
    u = jnp.sqrt(-jnp.expm1(2.0 * log_a)) * (i * x.astype(f32))
    u = u.at[:, 0].add(a[:, 0] * h0.astype(f32))
    _, h = lax.associative_scan(_lin_comb, (a, u), axis=1)
    return h.astype(x.dtype), h[:, -1].astype(x.dtype)


def recurrent_group(xr, xg, conv_buf, h0, conv_w, conv_b, w_a, b_a, w_x, b_x, lam):
    xc, new_buf = causal_conv(xr, conv_buf, conv_w, conv_b)
    h, h_last = rg_lru(xc, h0, w_a, b_a, w_x, b_x, lam)
    return h * jax.nn.gelu(xg), new_buf, h_last


def hier_moe(x, w_rg, b_rg, w_re, b_re, w_gate, w_up, w_down):
    n = x.shape[0]
    f32 = jnp.float32
    rows = jnp.arange(n)
    g_logit = (x @ w_rg).astype(f32) + b_rg.astype(f32)
    g_prob = jax.nn.softmax(g_logit, axis=-1)
    g_star = jnp.argmax(g_logit, axis=-1)
    e_logit = ((x @ w_re).astype(f32) + b_re.astype(f32)).reshape(n, N_GROUPS, EXP_PER_GROUP)
    e_prob = jax.nn.softmax(e_logit[rows, g_star], axis=-1)
    top_p, top_i = lax.top_k(e_prob, TOP_K_IN_GROUP)
    w = top_p / jnp.sum(top_p, axis=-1, keepdims=True) * g_prob[rows, g_star][:, None]
    e_idx = g_star[:, None] * EXP_PER_GROUP + top_i
    comb = jnp.einsum('nk,nke->ne', w, jax.nn.one_hot(e_idx, N_EXPERTS, dtype=f32))
    hg = jnp.einsum('nd,edf->nef', x, w_gate)
    hu = jnp.einsum('nd,edf->nef', x, w_up)
    h = jax.nn.silu(hg) * hu * comb[:, :, None].astype(x.dtype)
    return jnp.einsum('nef,efd->nd', h, w_down)


def project(x, norm_g, w_in):
    b, t, _ = x.shape
    z = rms_norm(x, norm_g) @ w_in
    q, kc, ks, kw, gt, xr, xg = jnp.split(z, SPLIT_OFFSETS, axis=-1)
    kv = lambda a: a.reshape(b, t, N_KV, 2, HEAD_DIM)
    return (q.reshape(b, t, N_HEADS, HEAD_DIM), kv(kc), kv(ks), kv(kw),
            gt.reshape(b, t, N_HEADS, N_BRANCH), xr, xg)


def finish(x, attn_o, rnn_o, w_out, norm_ffn, w_rg, b_rg, w_re, b_re, w_gate, w_up, w_down):
    h = x + jnp.concatenate([attn_o, rnn_o], axis=-1) @ w_out
    b, t, d = h.shape
    ffn = hier_moe(rms_norm(h, norm_ffn).reshape(b * t, d), w_rg, b_rg, w_re, b_re, w_gate, w_up, w_down)
    return h + ffn.reshape(b, t, d).astype(h.dtype)


def setup_inputs(seed: int = 0) -> dict:
    key = jax.random.key(seed)
    ks = jax.random.split(key, 32)
    nrm = jax.random.normal
    n_pages = PAST_LEN // PAGE_SIZE
    n_used = DEC_BATCH * n_pages
    n_pool = n_used + max(1, n_used // 4)
    n_win = min(WINDOW, PAST_LEN)
    page_table = jax.random.permutation(ks[7], n_pool)[:n_used].reshape(DEC_BATCH, n_pages).astype(jnp.int32)
    u = jax.random.uniform(ks[17], (DEPTH, D_RNN), minval=0.9, maxval=0.999)
    s = u ** (1.0 / LRU_C)
    return {
        'x_prompt': nrm(ks[0], (BATCH, SEQ, D_MODEL), jnp.float32),
        'x_sample': nrm(ks[1], (DEC_BATCH, DEC_SEQ, D_MODEL), jnp.float32),
        'cache_cmp_kv': nrm(ks[2], (DEPTH, n_pool, PAGE_SIZE, N_KV, 2, HEAD_DIM), jnp.float32),
        'cache_sel_kv': nrm(ks[3], (DEPTH, n_pool, PAGE_SIZE, N_KV, 2, HEAD_DIM), jnp.float32),
        'cache_win_kv': nrm(ks[4], (DEPTH, DEC_BATCH, n_win, N_KV, 2, HEAD_DIM), jnp.float32),
        'state_conv': nrm(ks[5], (DEPTH, DEC_BATCH, CONV_W - 1, D_RNN), jnp.float32),
        'state_h': 0.5 * nrm(ks[6], (DEPTH, DEC_BATCH, D_RNN), jnp.float32),
        'page_table': page_table,
        'norm_mix': 1.0 + 0.02 * nrm(ks[8], (DEPTH, D_MODEL), jnp.float32),
        'w_in': nrm(ks[9], (DEPTH, D_MODEL, D_IN), jnp.float32) * D_MODEL ** -0.5,
        'cmp_pool_w': (1.0 + 0.1 * nrm(ks[10], (DEPTH, CMP_BLOCK, 2), jnp.float32)) / CMP_BLOCK,
        'conv_w': nrm(ks[11], (DEPTH, CONV_W, D_RNN), jnp.float32) * CONV_W ** -0.5,
        'conv_b': 0.01 * nrm(ks[12], (DEPTH, D_RNN), jnp.float32),
        'lru_wa': nrm(ks[13], (DEPTH, RNN_BLOCKS, RNN_BLOCK_DIM, RNN_BLOCK_DIM), jnp.float32) * RNN_BLOCK_DIM ** -0.5,
        'lru_ba': 0.01 * nrm(ks[14], (DEPTH, D_RNN), jnp.float32),
        'lru_wx': nrm(ks[15], (DEPTH, RNN_BLOCKS, RNN_BLOCK_DIM, RNN_BLOCK_DIM), jnp.float32) * RNN_BLOCK_DIM ** -0.5,
        'lru_bx': 0.01 * nrm(ks[16], (DEPTH, D_RNN), jnp.float32),
        'lru_lambda': jnp.log(s) - jnp.log1p(-s),
        'w_out': nrm(ks[18], (DEPTH, D_MODEL, D_MODEL), jnp.float32) * D_MODEL ** -0.5,
        'norm_ffn': 1.0 + 0.02 * nrm(ks[19], (DEPTH, D_MODEL), jnp.float32),
        'w_router_group': nrm(ks[20], (DEPTH, D_MODEL, N_GROUPS), jnp.float32) * D_MODEL ** -0.5,
        'b_router_group': 0.01 * nrm(ks[21], (DEPTH, N_GROUPS), jnp.float32),
        'w_router_expert': nrm(ks[22], (DEPTH, D_MODEL, N_EXPERTS), jnp.float32) * D_MODEL ** -0.5,
        'b_router_expert': 0.01 * nrm(ks[23], (DEPTH, N_EXPERTS), jnp.float32),
        'w_exp_gate': nrm(ks[24], (DEPTH, N_EXPERTS, D_MODEL, D_EXPERT), jnp.float32) * D_MODEL ** -0.5,
        'w_exp_up': nrm(ks[25], (DEPTH, N_EXPERTS, D_MODEL, D_EXPERT), jnp.float32) * D_MODEL ** -0.5,
        'w_exp_down': nrm(ks[26], (DEPTH, N_EXPERTS, D_EXPERT, D_MODEL), jnp.float32) * D_EXPERT ** -0.5,
        'norm_final': 1.0 + 0.02 * nrm(ks[27], (D_MODEL,), jnp.float32),
    }


def reference(x_prompt, x_sample, cache_cmp_kv, cache_sel_kv, cache_win_kv, state_conv, state_h, page_table,
              norm_mix, w_in, cmp_pool_w, conv_w, conv_b, lru_wa, lru_ba, lru_wx, lru_bx, lru_lambda, w_out,
              norm_ffn, w_router_group, b_router_group, w_router_expert, b_router_expert,
              w_exp_gate, w_exp_up, w_exp_down, norm_final):
    slopes = alibi_slopes()
    hp, hs = x_prompt, x_sample
    bp = x_prompt.shape[0]
    l_cmp_p, l_cmp_s, l_sel_p, l_sel_s, l_win_p, l_win_s = [], [], [], [], [], []
    l_conv_p, l_conv_s, l_h_p, l_h_s = [], [], [], []
    for l in range(DEPTH):
        moe_w = (w_out[l], norm_ffn[l], w_router_group[l], b_router_group[l], w_router_expert[l],
                 b_router_expert[l], w_exp_gate[l], w_exp_up[l], w_exp_down[l])
        lru_w = (conv_w[l], conv_b[l], lru_wa[l], lru_ba[l], lru_wx[l], lru_bx[l], lru_lambda[l])
        q, kc, ks, kw, gt, xr, xg = project(hp, norm_mix[l], w_in[l])
        attn_p = nsa_prompt(q, gt, kc, ks, kw, cmp_pool_w[l], slopes)
        zbuf = jnp.zeros((bp, CONV_W - 1, D_RNN), xr.dtype)
        zh = jnp.zeros((bp, D_RNN), xr.dtype)
        rnn_p, conv_p, h_p = recurrent_group(xr, xg, zbuf, zh, *lru_w)
        hp = finish(hp, attn_p, rnn_p, *moe_w)
        l_cmp_p.append(kc)
        l_sel_p.append(ks)
        l_win_p.append(kw[:, -min(WINDOW, kw.shape[1]):])
        l_conv_p.append(conv_p)
        l_h_p.append(h_p)
        q, kc, ks, kw, gt, xr, xg = project(hs, norm_mix[l], w_in[l])
        attn_s, win_s = nsa_sample(q, gt, kc, ks, kw, cache_cmp_kv[l], cache_sel_kv[l], cache_win_kv[l],
                                   page_table, cmp_pool_w[l], slopes)
        rnn_s, conv_s, h_s = recurrent_group(xr, xg, state_conv[l], state_h[l], *lru_w)
        hs = finish(hs, attn_s, rnn_s, *moe_w)
        l_cmp_s.append(kc)
        l_sel_s.append(ks)
        l_win_s.append(win_s)
        l_conv_s.append(conv_s)
        l_h_s.append(h_s)
    y_prompt = rms_norm(hp, norm_final)
    y_sample = rms_norm(hs, norm_final)
    new_cmp_prompt = jnp.stack(l_cmp_p)
    new_cmp_sample = jnp.stack(l_cmp_s)
    new_sel_prompt = jnp.stack(l_sel_p)
    new_sel_sample = jnp.stack(l_sel_s)
    new_win_prompt = jnp.stack(l_win_p)
    new_win_sample = jnp.stack(l_win_s)
    new_conv_prompt = jnp.stack(l_conv_p)
    new_conv_sample = jnp.stack(l_conv_s)
    new_h_prompt = jnp.stack(l_h_p)
    new_h_sample = jnp.stack(l_h_s)
    return (y_prompt, y_sample, new_cmp_prompt, new_cmp_sample, new_sel_prompt, new_sel_sample,
            new_win_prompt, new_win_sample, new_conv_prompt, new_conv_sample, new_h_prompt, new_h_sample)
```

```python
import functools

import jax
import jax.numpy as jnp
from jax import lax
from jax.experimental import pallas as pl
from jax.experimental.pallas import tpu as pltpu

F32 = jnp.float32
BF16 = jnp.bfloat16

D_MODEL = 2048
D_ATTN = 1024
D_RNN = 1024
N_HEADS = 8
HEAD_DIM = 128
N_KV = 2
GQA = 4
KV_W = 512
CMP_BLOCK = 32
SEL_BLOCK = 64
TOP_N = 16
WINDOW = 512
FORCE_SCORE = 1e4
RNN_BLOCKS = 16
RNN_BLOCK_DIM = 64
CONV_W = 4
LRU_C = 8.0
N_GROUPS = 4
EXP_PER_GROUP = 4
N_EXPERTS = 16
D_EXPERT = 512
RMS_EPS = 1e-6
PAGE_SIZE = 128

LANES = 128
NEG = -1e30
VMEM_LIMIT = 56 * 1024 * 1024

_SEG_Q = (0, 1024)
_SEG_KC = (1024, 1536)
_SEG_KS = (1536, 2048)
_SEG_KW = (2048, 2560)
_SEG_XR = (2560, 3584)
_SEG_XG = (3584, 4608)
_SEG_GT = (4608, 4864)
_PROJ_W = 4864


def _dot(a, b):
    return jnp.dot(a, b, preferred_element_type=F32)


def _dot_nt(a, b):
    return lax.dot_general(a, b, (((1,), (1,)), ((), ())), preferred_element_type=F32)


def _rms(x, g):
    return x * lax.rsqrt(jnp.mean(x * x, axis=-1, keepdims=True) + RMS_EPS) * g


def _split3(x):
    h1 = x.astype(BF16)
    r1 = x - h1.astype(F32)
    h2 = r1.astype(BF16)
    h3 = (r1 - h2.astype(F32)).astype(BF16)
    return h1, h2, h3


def _dot_f32(a, b):
    a1, a2, a3 = _split3(a)
    b1, b2, b3 = _split3(b)
    return (_dot(a3, b1) + _dot(a2, b2) + _dot(a1, b3)) + (_dot(a2, b1) + _dot(a1, b2)) + _dot(a1, b1)


def _dot_sel(a, m01):
    a1, a2, a3 = _split3(a)
    return (_dot(a3, m01) + _dot(a2, m01)) + _dot(a1, m01)


def _proj_kernel(x_ref, g_ref, w_ref, q_ref, kc_ref, ks_ref, kw_ref, xr_ref, xg_ref, gt_ref):
    xn = _rms(x_ref[...], g_ref[...]).astype(BF16)
    outs = ((q_ref, _SEG_Q), (kc_ref, _SEG_KC), (ks_ref, _SEG_KS), (kw_ref, _SEG_KW),
            (xr_ref, _SEG_XR), (xg_ref, _SEG_XG), (gt_ref, _SEG_GT))
    for ref, (lo, hi) in outs:
        ref[...] = _dot(xn, w_ref[:, lo:hi])


def _project(x, g, w, tm):
    n = x.shape[0]
    widths = [hi - lo for lo, hi in (_SEG_Q, _SEG_KC, _SEG_KS, _SEG_KW, _SEG_XR, _SEG_XG, _SEG_GT)]
    return pl.pallas_call(
        _proj_kernel,
        grid=(n // tm,),
        in_specs=[pl.BlockSpec((tm, D_MODEL), lambda i: (i, 0)),
                  pl.BlockSpec((1, D_MODEL), lambda i: (0, 0)),
                  pl.BlockSpec((D_MODEL, _PROJ_W), lambda i: (0, 0), pipeline_mode=pl.Buffered(1))],
        out_specs=[pl.BlockSpec((tm, w_), lambda i: (i, 0)) for w_ in widths],
        out_shape=[jax.ShapeDtypeStruct((n, w_), F32) for w_ in widths],
        compiler_params=pltpu.CompilerParams(dimension_semantics=("arbitrary",), vmem_limit_bytes=VMEM_LIMIT),
        name="proj",
    )(x, g, w)


def _compress_kernel(n_pages, pt_ref, *refs):
    del pt_ref
    pages, w_ref, o_ref = refs[:n_pages], refs[n_pages], refs[n_pages + 1]
    if n_pages == 1:
        x = pages[0][0]
    else:
        x = jnp.concatenate([p[0] for p in pages], axis=0)
    rows = x.shape[0]
    xb = x.reshape(rows // CMP_BLOCK, CMP_BLOCK, KV_W) * w_ref[...][None]
    o_ref[0] = jnp.sum(xb, axis=1)


def _compress(table, src, wfull, n_batch, steps, n_pages, page_rows):
    out_rows = n_pages * page_rows // CMP_BLOCK
    per_b = steps * n_pages

    def page_spec(k):
        return pl.BlockSpec((1, page_rows, KV_W), lambda b, j, pt: (pt[b * per_b + j * n_pages + k], 0, 0))

    return pl.pallas_call(
        functools.partial(_compress_kernel, n_pages),
        grid_spec=pltpu.PrefetchScalarGridSpec(
            num_scalar_prefetch=1,
            grid=(n_batch, steps),
            in_specs=[page_spec(k) for k in range(n_pages)] + [pl.BlockSpec((CMP_BLOCK, KV_W), lambda b, j, pt: (0, 0))],
            out_specs=pl.BlockSpec((1, out_rows, KV_W), lambda b, j, pt: (b, j, 0)),
        ),
        out_shape=jax.ShapeDtypeStruct((n_batch, steps * out_rows, KV_W), F32),
        compiler_params=pltpu.CompilerParams(dimension_semantics=("arbitrary", "arbitrary"), vmem_limit_bytes=VMEM_LIMIT),
        name="compress",
    )(table, *([src] * n_pages), wfull)


def _softplus(x):
    return jnp.maximum(x, 0.0) + jnp.log1p(jnp.exp(-jnp.abs(x)))


def _rglru_kernel(tc, xr_ref, xg_ref, cs_ref, h0_ref, cw_ref, cb_ref, wa_ref, ba_ref, wx_ref, bx_ref, lam_ref,
                  o_ref, hl_ref, tail_scr, h_scr):
    @pl.when(pl.program_id(2) == 0)
    def _():
        tail_scr[...] = cs_ref[0]
        h_scr[...] = h0_ref[0]

    x = xr_ref[0]
    xp = jnp.concatenate([tail_scr[...], x], axis=0)
    w = cw_ref[...]
    xc = cb_ref[...] + pltpu.roll(xp, 3, axis=0)[8:] * w[0:1]
    xc = xc + pltpu.roll(xp, 2, axis=0)[8:] * w[1:2]
    xc = xc + pltpu.roll(xp, 1, axis=0)[8:] * w[2:3]
    xc = xc + x * w[3:4]
    tail_scr[...] = xp[tc:]

    xcb = xc.astype(BF16)
    r = jax.nn.sigmoid(_dot(xcb, wa_ref[0]) + ba_ref[...])
    gi = jax.nn.sigmoid(_dot(xcb, wx_ref[0]) + bx_ref[...])
    log_a = -LRU_C * r * _softplus(-lam_ref[...])
    a = jnp.exp(log_a)
    u = jnp.sqrt(-jnp.tanh(log_a) * (a * a + 1.0)) * (gi * xc)

    row = lax.broadcasted_iota(jnp.int32, a.shape, 0)
    s = 1
    while s < tc:
        keep = row >= s
        a_sh = jnp.where(keep, pltpu.roll(a, s, axis=0), 1.0)
        u_sh = jnp.where(keep, pltpu.roll(u, s, axis=0), 0.0)
        u = a * u_sh + u
        a = a * a_sh
        s *= 2
    h = a * h_scr[...] + u
    h_last = h[tc - 1:tc]
    h_scr[...] = h_last
    hl_ref[0] = h_last
    o_ref[0] = h * jax.nn.gelu(xg_ref[0])


def _rglru(xr, xg, conv_state8, h0, conv_w, conv_b, wa_t, ba, wx_t, bx, lam, tc):
    b, t, _ = xr.shape
    ct = 256
    n_ct = D_RNN // ct
    seq = lambda bi, c, j: (bi, j, c)
    per_b = lambda bi, c, j: (bi, 0, c)
    par = lambda bi, c, j: (0, c)
    return pl.pallas_call(
        functools.partial(_rglru_kernel, tc),
        grid=(b, n_ct, t // tc),
        in_specs=[pl.BlockSpec((1, tc, ct), seq), pl.BlockSpec((1, tc, ct), seq),
                  pl.BlockSpec((1, 8, ct), per_b), pl.BlockSpec((1, 1, ct), per_b),
                  pl.BlockSpec((CONV_W, ct), par), pl.BlockSpec((1, ct), par),
                  pl.BlockSpec((1, ct, ct), lambda bi, c, j: (c, 0, 0)), pl.BlockSpec((1, ct), par),
                  pl.BlockSpec((1, ct, ct), lambda bi, c, j: (c, 0, 0)), pl.BlockSpec((1, ct), par),
                  pl.BlockSpec((1, ct), par)],
        out_specs=[pl.BlockSpec((1, tc, ct), seq), pl.BlockSpec((1, 1, ct), per_b)],
        out_shape=[jax.ShapeDtypeStruct((b, t, D_RNN), F32), jax.ShapeDtypeStruct((b, 1, D_RNN), F32)],
        scratch_shapes=[pltpu.VMEM((8, ct), F32), pltpu.VMEM((1, ct), F32)],
        compiler_params=pltpu.CompilerParams(dimension_semantics=("arbitrary", "arbitrary", "arbitrary"),
                                             vmem_limit_bytes=VMEM_LIMIT),
        name="rglru",
    )(xr, xg, conv_state8, h0, conv_w, conv_b, wa_t, ba, wx_t, bx, lam)


TQ = 128


def _nsa_prompt_kernel(slopes_ref, q_ref, gt_ref, kc_ref, vc_ref, ks_ref, vs_ref, kw_ref, vw_ref, o_ref,
                       q_scr, impt_scr, sel_scr, m_scr, l_scr, acc_scr):
    g = pl.program_id(1)
    i = pl.program_id(2)
    start = i * TQ
    n_cmp = kc_ref.shape[1]
    n_sel = n_cmp // 2
    t_col = start + lax.broadcasted_iota(jnp.int32, (TQ, 1), 0)
    slopes = [slopes_ref[g * GQA + r] for r in range(GQA)]
    gates = jax.nn.sigmoid(gt_ref[0])

    scale = HEAD_DIM ** -0.5
    for r in range(GQA):
        q_scr[r] = (q_ref[0, :, r * HEAD_DIM:(r + 1) * HEAD_DIM] * scale).astype(BF16)

    kc = kc_ref[0].astype(BF16)
    vc = vc_ref[0].astype(BF16)
    c_end = (lax.broadcasted_iota(jnp.int32, (1, n_cmp), 1) + 1) * CMP_BLOCK - 1
    dist_c = t_col - c_end
    mask_c = dist_c >= 0
    dist_cf = dist_c.astype(F32)
    p_sum = jnp.zeros((TQ, n_cmp), F32)
    for r in range(GQA):
        s = _dot_nt(q_scr[r], kc) - slopes[r] * dist_cf
        s = jnp.where(mask_c, s, NEG)
        m = jnp.max(s, axis=-1, keepdims=True)
        p = jnp.where(mask_c, jnp.exp(s - m), 0.0)
        p = p / jnp.maximum(jnp.sum(p, axis=-1, keepdims=True), 1e-30)
        p_sum = p_sum + p
        o_ref[0, :, r * HEAD_DIM:(r + 1) * HEAD_DIM] = gates[:, 3 * r:3 * r + 1] * _dot(p.astype(BF16), vc)

    impt_scr[...] = p_sum.T
    imp = impt_scr[pl.ds(0, n_sel, stride=2), :] + impt_scr[pl.ds(1, n_sel, stride=2), :]
    blk = lax.broadcasted_iota(jnp.int32, (n_sel, TQ), 0)
    cur = (start + lax.broadcasted_iota(jnp.int32, (n_sel, TQ), 1)) // SEL_BLOCK
    forced = (blk == 0) | (blk == cur) | (blk == cur - 1)
    imp = jnp.where(forced, FORCE_SCORE, imp)
    imp = jnp.where(blk > cur, NEG, imp)
    rank = jnp.zeros((n_sel, TQ), jnp.int32)
    for j in range(n_sel):
        row = imp[j:j + 1, :]
        tie = jnp.where(blk > j, jnp.where(row == imp, 1, 0), 0)
        rank = rank + jnp.where(row > imp, 1, tie)
    sel_t = jnp.where(rank < TOP_N, 1.0, 0.0)
    sel = jnp.concatenate([sel_t, jnp.zeros((LANES - n_sel, TQ), F32)], axis=0).T
    sel_scr[...] = sel.astype(BF16)

    def flash(k_ref, v_ref, c_lo, c_hi, mask_fn):
        for r in range(GQA):
            m_scr[r] = jnp.full((TQ, 1), NEG, F32)
            l_scr[r] = jnp.zeros((TQ, 1), F32)
            acc_scr[r] = jnp.zeros((TQ, HEAD_DIM), F32)

        def body(c, carry):
            off = pl.multiple_of(c * LANES, LANES)
            kch = k_ref[0, pl.ds(off, LANES), :].astype(BF16)
            vch = v_ref[0, pl.ds(off, LANES), :].astype(BF16)
            dist = t_col - (off + lax.broadcasted_iota(jnp.int32, (1, LANES), 1))
            mask = mask_fn(c, dist)
            dist_f = dist.astype(F32)
            for r in range(GQA):
                s = _dot_nt(q_scr[r], kch) - slopes[r] * dist_f
                s = jnp.where(mask, s, NEG)
                m_old = m_scr[r]
                m_new = jnp.maximum(m_old, jnp.max(s, axis=-1, keepdims=True))
                alpha = jnp.exp(m_old - m_new)
                p = jnp.where(mask, jnp.exp(s - m_new), 0.0)
                l_scr[r] = alpha * l_scr[r] + jnp.sum(p, axis=-1, keepdims=True)
                acc_scr[r] = alpha * acc_scr[r] + _dot(p.astype(BF16), vch)
                m_scr[r] = m_new
            return carry

        lax.fori_loop(c_lo, c_hi, body, 0)
        return [acc_scr[r] / jnp.maximum(l_scr[r], 1e-30) for r in range(GQA)]

    def sel_mask(c, dist):
        brow = lax.broadcasted_iota(jnp.int32, (LANES, LANES), 0)
        kcol = lax.broadcasted_iota(jnp.int32, (LANES, LANES), 1)
        expand = jnp.where(brow == 2 * c + kcol // SEL_BLOCK, 1.0, 0.0).astype(BF16)
        chosen = _dot(sel_scr[...], expand)
        return (chosen > 0.5) & (dist >= 0)

    def win_mask(c, dist):
        return (dist >= 0) & (dist <= WINDOW)

    o_s = flash(ks_ref, vs_ref, 0, i + 1, sel_mask)
    for r in range(GQA):
        cols = slice(r * HEAD_DIM, (r + 1) * HEAD_DIM)
        o_ref[0, :, cols] = o_ref[0, :, cols] + gates[:, 3 * r + 1:3 * r + 2] * o_s[r]
    o_w = flash(kw_ref, vw_ref, jnp.maximum(i - WINDOW // LANES, 0), i + 1, win_mask)
    for r in range(GQA):
        cols = slice(r * HEAD_DIM, (r + 1) * HEAD_DIM)
        o_ref[0, :, cols] = o_ref[0, :, cols] + gates[:, 3 * r + 2:3 * r + 3] * o_w[r]


def _nsa_prompt(slopes, q, gt, cmp_kv, ks, kw):
    b, s, _ = q.shape
    n_cmp = cmp_kv.shape[1]
    k_of = lambda bi, g, i: (bi, 0, 2 * g)
    v_of = lambda bi, g, i: (bi, 0, 2 * g + 1)
    return pl.pallas_call(
        _nsa_prompt_kernel,
        grid=(b, N_KV, s // TQ),
        in_specs=[pl.BlockSpec(memory_space=pltpu.SMEM),
                  pl.BlockSpec((1, TQ, GQA * HEAD_DIM), lambda bi, g, i: (bi, i, g)),
                  pl.BlockSpec((1, TQ, LANES), lambda bi, g, i: (bi, i, g)),
                  pl.BlockSpec((1, n_cmp, HEAD_DIM), k_of), pl.BlockSpec((1, n_cmp, HEAD_DIM), v_of),
                  pl.BlockSpec((1, s, HEAD_DIM), k_of), pl.BlockSpec((1, s, HEAD_DIM), v_of),
                  pl.BlockSpec((1, s, HEAD_DIM), k_of), pl.BlockSpec((1, s, HEAD_DIM), v_of)],
        out_specs=pl.BlockSpec((1, TQ, GQA * HEAD_DIM), lambda bi, g, i: (bi, i, g)),
        out_shape=jax.ShapeDtypeStruct((b, s, D_ATTN), F32),
        scratch_shapes=[pltpu.VMEM((GQA, TQ, HEAD_DIM), BF16),
                        pltpu.VMEM((n_cmp, TQ), F32),
                        pltpu.VMEM((TQ, LANES), BF16),
                        pltpu.VMEM((GQA, TQ, 1), F32),
                        pltpu.VMEM((GQA, TQ, 1), F32),
                        pltpu.VMEM((GQA, TQ, HEAD_DIM), F32)],
        compiler_params=pltpu.CompilerParams(dimension_semantics=("arbitrary", "arbitrary", "arbitrary"),
                                             vmem_limit_bytes=VMEM_LIMIT),
        name="nsa_prompt",
    )(slopes, q, gt, cmp_kv, cmp_kv, ks, ks, kw, kw)


def _nsa_sample_a_kernel(past, q_ref, gt_ref, cmp_ref, cw_ref, kwn_ref, idx_ref, o_ref):
    t_new = q_ref.shape[1]
    n_cmp = cmp_ref.shape[1]
    n_past_sel = n_cmp // 2
    n_win = cw_ref.shape[1]
    rows = GQA * t_new
    gates = jax.nn.sigmoid(gt_ref[0])
    t_row = past + lax.broadcasted_iota(jnp.int32, (rows, 1), 0) % t_new
    scale = HEAD_DIM ** -0.5
    pair = jnp.where(lax.broadcasted_iota(jnp.int32, (n_cmp, n_past_sel), 0) // 2
                     == lax.broadcasted_iota(jnp.int32, (n_cmp, n_past_sel), 1), 1.0, 0.0).astype(BF16)
    imps = []
    for g in range(N_KV):
        qg = jnp.concatenate([q_ref[0, :, (g * GQA + r) * HEAD_DIM:(g * GQA + r + 1) * HEAD_DIM]
                              for r in range(GQA)], axis=0)
        qg = (qg * scale).astype(BF16)
        slope = jnp.concatenate([jnp.full((t_new, 1), 2.0 ** -(g * GQA + r + 1), F32) for r in range(GQA)], axis=0)
        kcol = g * 2 * HEAD_DIM
        kc = cmp_ref[0, :, kcol:kcol + HEAD_DIM].astype(BF16)
        vc = cmp_ref[0, :, kcol + HEAD_DIM:kcol + 2 * HEAD_DIM].astype(BF16)
        c_end = (lax.broadcasted_iota(jnp.int32, (1, n_cmp), 1) + 1) * CMP_BLOCK - 1
        dist_c = t_row - c_end
        mask_c = dist_c >= 0
        s = _dot_nt(qg, kc) - slope * dist_c.astype(F32)
        s = jnp.where(mask_c, s, NEG)
        m = jnp.max(s, axis=-1, keepdims=True)
        p = jnp.where(mask_c, jnp.exp(s - m), 0.0)
        p = p / jnp.maximum(jnp.sum(p, axis=-1, keepdims=True), 1e-30)
        o_c = _dot(p.astype(BF16), vc)
        p_heads = p[0:t_new]
        for r in range(1, GQA):
            p_heads = p_heads + p[r * t_new:(r + 1) * t_new]
        imps.append(_dot_sel(p_heads, pair))
        n_pad = LANES - t_new
        kw = jnp.concatenate([cw_ref[0, :, kcol:kcol + HEAD_DIM], kwn_ref[0, :, kcol:kcol + HEAD_DIM],
                              jnp.zeros((n_pad, HEAD_DIM), F32)], axis=0).astype(BF16)
        vw = jnp.concatenate([cw_ref[0, :, kcol + HEAD_DIM:kcol + 2 * HEAD_DIM],
                              kwn_ref[0, :, kcol + HEAD_DIM:kcol + 2 * HEAD_DIM],
                              jnp.zeros((n_pad, HEAD_DIM), F32)], axis=0).astype(BF16)
        win_pos = past - n_win + lax.broadcasted_iota(jnp.int32, (1, n_win + LANES), 1)
        dist_w = t_row - win_pos
        mask_w = (dist_w >= 0) & (dist_w <= WINDOW)
        s = _dot_nt(qg, kw) - slope * dist_w.astype(F32)
        s = jnp.where(mask_w, s, NEG)
        m = jnp.max(s, axis=-1, keepdims=True)
        p = jnp.where(mask_w, jnp.exp(s - m), 0.0)
        p = p / jnp.maximum(jnp.sum(p, axis=-1, keepdims=True), 1e-30)
        o_w = _dot(p.astype(BF16), vw)
        for r in range(GQA):
            h = g * GQA + r
            gl = g * LANES + 3 * r
            o_ref[0, :, h * HEAD_DIM:(h + 1) * HEAD_DIM] = (
                gates[:, gl:gl + 1] * o_c[r * t_new:(r + 1) * t_new]
                + gates[:, gl + 2:gl + 3] * o_w[r * t_new:(r + 1) * t_new])

    imp = jnp.concatenate(imps, axis=0)
    n_rows = N_KV * t_new
    lane = lax.broadcasted_iota(jnp.int32, (n_rows, n_past_sel), 1)
    lane_f = lane.astype(F32)
    cur = (past + lax.broadcasted_iota(jnp.int32, (n_rows, n_past_sel), 0) % t_new) // SEL_BLOCK
    forced = (lane == 0) | (lane == cur) | (lane == cur - 1)
    imp = jnp.where(forced, FORCE_SCORE, imp)
    imp = jnp.where(lane > cur, NEG, imp)
    out_lane = lax.broadcasted_iota(jnp.int32, (n_rows, LANES), 1)
    idx = jnp.zeros((n_rows, LANES), F32)
    for k in range(TOP_N - 1):
        m = jnp.max(imp, axis=-1, keepdims=True)
        j = jnp.min(jnp.where(imp == m, lane_f, float(n_past_sel)), axis=-1, keepdims=True)
        idx = jnp.where(out_lane == k, j, idx)
        imp = jnp.where(lane_f == j, -3e38, imp)
    idx_ref[0] = idx.astype(jnp.int32)


def _nsa_sample_a(past, q, gt, cmp_kv, cache_w, kw_new):
    b, t_new, _ = q.shape
    per_b3 = lambda bi: (bi, 0, 0)
    return pl.pallas_call(
        functools.partial(_nsa_sample_a_kernel, past),
        grid=(b,),
        in_specs=[pl.BlockSpec((1, t_new, D_ATTN), per_b3), pl.BlockSpec((1, t_new, 2 * LANES), per_b3),
                  pl.BlockSpec((1,) + cmp_kv.shape[1:], per_b3), pl.BlockSpec((1,) + cache_w.shape[1:], per_b3),
                  pl.BlockSpec((1, t_new, KV_W), per_b3)],
        out_specs=[pl.BlockSpec((1, N_KV * t_new, LANES), per_b3), pl.BlockSpec((1, t_new, D_ATTN), per_b3)],
        out_shape=[jax.ShapeDtypeStruct((b, N_KV * t_new, LANES), jnp.int32),
                   jax.ShapeDtypeStruct((b, t_new, D_ATTN), F32)],
        compiler_params=pltpu.CompilerParams(dimension_semantics=("arbitrary",), vmem_limit_bytes=VMEM_LIMIT),
        name="nsa_sample_a",
    )(q, gt, cmp_kv, cache_w, kw_new)


N_GATHER = TOP_N - 1
GATHER_ROWS = TOP_N * SEL_BLOCK


def _nsa_sample_b_kernel(past, t_new, idx_ref, pt_ref, q_ref, gs_ref, slope_ref, ksn_ref, ocw_ref, pool_ref,
                         o_ref, kv_buf, sems):
    b = pl.program_id(0)
    t = pl.program_id(1)
    pages_per_b = past // PAGE_SIZE
    sel_per_page = PAGE_SIZE // SEL_BLOCK

    def block_copy(g, k):
        lp = idx_ref[((b * N_KV + g) * t_new + t) * N_GATHER + k]
        phys = pt_ref[b * pages_per_b + lp // sel_per_page] * sel_per_page + lp % sel_per_page
        return pltpu.make_async_copy(
            pool_ref.at[phys, :, pl.ds(g * 2 * HEAD_DIM, 2 * HEAD_DIM)],
            kv_buf.at[g, pl.ds(k * SEL_BLOCK, SEL_BLOCK), :],
            sems.at[g, k])

    for g in range(N_KV):
        for k in range(N_GATHER):
            block_copy(g, k).start()

    t_pos = past + t
    q8 = (q_ref[0] * HEAD_DIM ** -0.5).astype(BF16)
    slope = slope_ref[:, 0:1]
    lane = lax.broadcasted_iota(jnp.int32, (1, GATHER_ROWS), 1)
    new_lo = N_GATHER * SEL_BLOCK
    outs = []
    for g in range(N_KV):
        kv_buf[g, pl.ds(new_lo, t_new), :] = ksn_ref[0, :, g * 2 * HEAD_DIM:(g + 1) * 2 * HEAD_DIM]
        kv_buf[g, pl.ds(new_lo + t_new, SEL_BLOCK - t_new), :] = jnp.zeros((SEL_BLOCK - t_new, 2 * HEAD_DIM), F32)
        pos = past + (lane - new_lo)
        for k in range(N_GATHER):
            lp = idx_ref[((b * N_KV + g) * t_new + t) * N_GATHER + k]
            pos = jnp.where(lane // SEL_BLOCK == k, lp * SEL_BLOCK + lane % SEL_BLOCK, pos)
        for k in range(N_GATHER):
            block_copy(g, k).wait()
        kk = kv_buf[g, :, 0:HEAD_DIM].astype(BF16)
        vv = kv_buf[g, :, HEAD_DIM:2 * HEAD_DIM].astype(BF16)
        dist = t_pos - pos
        mask = dist >= 0
        s = _dot_nt(q8, kk) - slope * dist.astype(F32)
        s = jnp.where(mask, s, NEG)
        m = jnp.max(s, axis=-1, keepdims=True)
        p = jnp.where(mask, jnp.exp(s - m), 0.0)
        p = p / jnp.maximum(jnp.sum(p, axis=-1, keepdims=True), 1e-30)
        outs.append(_dot(p.astype(BF16), vv))
    head = lax.broadcasted_iota(jnp.int32, (N_HEADS, HEAD_DIM), 0)
    o_s = jnp.where(head < GQA, outs[0], outs[1])
    o_ref[0] = ocw_ref[0] + jax.nn.sigmoid(gs_ref[0]) * o_s


def _nsa_sample_b(past, idx, page_table, q, gate_sel, slopes8, ks_new, o_cw, pool):
    n_tok = q.shape[0]
    b, t_new, _ = ks_new.shape
    tok = lambda bi, ti, *_: (bi * t_new + ti, 0, 0)
    return pl.pallas_call(
        functools.partial(_nsa_sample_b_kernel, past, t_new),
        grid_spec=pltpu.PrefetchScalarGridSpec(
            num_scalar_prefetch=2,
            grid=(b, t_new),
            in_specs=[pl.BlockSpec((1, N_HEADS, HEAD_DIM), tok), pl.BlockSpec((1, N_HEADS, HEAD_DIM), tok),
                      pl.BlockSpec((N_HEADS, LANES), lambda bi, ti, *_: (0, 0)),
                      pl.BlockSpec((1, t_new, KV_W), lambda bi, ti, *_: (bi, 0, 0)),
                      pl.BlockSpec((1, N_HEADS, HEAD_DIM), tok),
                      pl.BlockSpec(memory_space=pl.ANY)],
            out_specs=pl.BlockSpec((1, N_HEADS, HEAD_DIM), tok),
            scratch_shapes=[pltpu.VMEM((N_KV, GATHER_ROWS, 2 * HEAD_DIM), F32),
                            pltpu.SemaphoreType.DMA((N_KV, N_GATHER))],
        ),
        out_shape=jax.ShapeDtypeStruct((n_tok, N_HEADS, HEAD_DIM), F32),
        compiler_params=pltpu.CompilerParams(dimension_semantics=("arbitrary", "arbitrary"),
                                             vmem_limit_bytes=VMEM_LIMIT),
        name="nsa_sample_b",
    )(idx, page_table, q, gate_sel, slopes8, ks_new, o_cw, pool)


def _finish_kernel(x_ref, a_ref, r_ref, wo_ref, gn_ref, wr_ref, br_ref, h_ref, xn_ref, comb_ref):
    h = (x_ref[...] + _dot(a_ref[...].astype(BF16), wo_ref[0:D_ATTN, :])
         + _dot(r_ref[...].astype(BF16), wo_ref[D_ATTN:D_MODEL, :]))
    h_ref[...] = h
    xn = _rms(h, gn_ref[...])
    xn_ref[...] = xn.astype(BF16)
    logit = _dot_f32(xn, wr_ref[...]) + br_ref[...]
    lane = lax.broadcasted_iota(jnp.int32, logit.shape, 1)
    lane_f = lane.astype(F32)
    is_g = (lane >= N_EXPERTS) & (lane < N_EXPERTS + N_GROUPS)
    gl = jnp.where(is_g, logit, NEG)
    g_max = jnp.max(gl, axis=-1, keepdims=True)
    g_star = jnp.min(jnp.where(gl == g_max, lane_f, 1e9), axis=-1, keepdims=True) - N_EXPERTS
    g_prob = 1.0 / jnp.sum(jnp.where(is_g, jnp.exp(gl - g_max), 0.0), axis=-1, keepdims=True)
    in_grp = (lane < N_EXPERTS) & ((lane // EXP_PER_GROUP).astype(F32) == g_star)
    el = jnp.where(in_grp, logit, NEG)
    e_max = jnp.max(el, axis=-1, keepdims=True)
    ee = jnp.where(in_grp, jnp.exp(el - e_max), 0.0)
    ep = jnp.where(in_grp, ee / jnp.sum(ee, axis=-1, keepdims=True), -1.0)
    p1 = jnp.max(ep, axis=-1, keepdims=True)
    i1 = jnp.min(jnp.where(ep == p1, lane_f, 1e9), axis=-1, keepdims=True)
    ep2 = jnp.where(lane_f == i1, -1.0, ep)
    p2 = jnp.max(ep2, axis=-1, keepdims=True)
    i2 = jnp.min(jnp.where(ep2 == p2, lane_f, 1e9), axis=-1, keepdims=True)
    tot = p1 + p2
    comb_ref[...] = (jnp.where(lane_f == i1, p1 / tot * g_prob, 0.0)
                     + jnp.where(lane_f == i2, p2 / tot * g_prob, 0.0))


def _finish(x, attn_o, rnn_o, w_out, g_ffn, w_router, b_router, tm):
    n = x.shape[0]
    row = lambda i: (i, 0)
    fixed = lambda i: (0, 0)
    return pl.pallas_call(
        _finish_kernel,
        grid=(n // tm,),
        in_specs=[pl.BlockSpec((tm, D_MODEL), row), pl.BlockSpec((tm, D_ATTN), row), pl.BlockSpec((tm, D_RNN), row),
                  pl.BlockSpec((D_MODEL, D_MODEL), fixed, pipeline_mode=pl.Buffered(1)),
                  pl.BlockSpec((1, D_MODEL), fixed),
                  pl.BlockSpec((D_MODEL, LANES), fixed), pl.BlockSpec((1, LANES), fixed)],
        out_specs=[pl.BlockSpec((tm, D_MODEL), row), pl.BlockSpec((tm, D_MODEL), row), pl.BlockSpec((tm, LANES), row)],
        out_shape=[jax.ShapeDtypeStruct((n, D_MODEL), F32), jax.ShapeDtypeStruct((n, D_MODEL), BF16),
                   jax.ShapeDtypeStruct((n, LANES), F32)],
        compiler_params=pltpu.CompilerParams(dimension_semantics=("arbitrary",), vmem_limit_bytes=VMEM_LIMIT),
        name="finish",
    )(x, attn_o, rnn_o, w_out, g_ffn, w_router, b_router)


def _moe_kernel(xn_ref, comb_ref, h_ref, wg_ref, wu_ref, wd_ref, gf_ref, y_ref, acc_scr):
    e = pl.program_id(1)

    @pl.when(e == 0)
    def _():
        acc_scr[...] = h_ref[...]

    xn = xn_ref[...]
    hg = _dot(xn, wg_ref[0])
    hu = _dot(xn, wu_ref[0])
    lane = lax.broadcasted_iota(jnp.int32, comb_ref.shape, 1)
    cw = jnp.sum(jnp.where(lane == e, comb_ref[...], 0.0), axis=-1, keepdims=True)
    hh = jax.nn.silu(hg) * hu * cw
    acc_scr[...] += _dot(hh.astype(BF16), wd_ref[0])

    @pl.when(e == pl.num_programs(1) - 1)
    def _():
        y_ref[...] = _rms(acc_scr[...], gf_ref[...])


def _moe(xn, comb, h, w_gate, w_up, w_down, g_final, tm):
    n = xn.shape[0]
    row = lambda i, e: (i, 0)
    return pl.pallas_call(
        _moe_kernel,
        grid=(n // tm, N_EXPERTS),
        in_specs=[pl.BlockSpec((tm, D_MODEL), row), pl.BlockSpec((tm, LANES), row), pl.BlockSpec((tm, D_MODEL), row),
                  pl.BlockSpec((1, D_MODEL, D_EXPERT), lambda i, e: (e, 0, 0)),
                  pl.BlockSpec((1, D_MODEL, D_EXPERT), lambda i, e: (e, 0, 0)),
                  pl.BlockSpec((1, D_EXPERT, D_MODEL), lambda i, e: (e, 0, 0)),
                  pl.BlockSpec((1, D_MODEL), lambda i, e: (0, 0))],
        out_specs=pl.BlockSpec((tm, D_MODEL), row),
        out_shape=jax.ShapeDtypeStruct((n, D_MODEL), F32),
        scratch_shapes=[pltpu.VMEM((tm, D_MODEL), F32)],
        compiler_params=pltpu.CompilerParams(dimension_semantics=("arbitrary", "arbitrary"),
                                             vmem_limit_bytes=VMEM_LIMIT),
        name="moe",
    )(xn, comb, h, w_gate, w_up, w_down, g_final)


def _block_diag_tiles(w):
    per = 256 // RNN_BLOCK_DIM
    w4 = w.reshape(RNN_BLOCKS // per, per, RNN_BLOCK_DIM, RNN_BLOCK_DIM)
    eye = jnp.eye(per, dtype=w.dtype)
    tiles = jnp.einsum('tpde,pq->tpdqe', w4, eye)
    return tiles.reshape(RNN_BLOCKS // per, 256, 256).astype(BF16)


def _layer(l, xp, xs, cache_cmp_kv, cache_sel_kv, cache_win_kv, state_conv, state_h, page_table,
           norm_mix, w_in, cmp_pool_w, conv_w, conv_b, lru_wa, lru_ba, lru_wx, lru_bx, lru_lambda, w_out,
           norm_ffn, w_router_group, b_router_group, w_router_expert, b_router_expert,
           w_exp_gate, w_exp_up, w_exp_down, final_gain):
    bp, sp, _ = xp.shape
    bs, ts, _ = xs.shape
    n_pages = page_table.shape[1]
    past = n_pages * PAGE_SIZE

    wi = w_in[l]
    gt_cols = wi[:, 2560:2584].reshape(D_MODEL, N_KV, GQA * 3)
    gt_cols = jnp.pad(gt_cols, ((0, 0), (0, 0), (0, LANES - GQA * 3))).reshape(D_MODEL, N_KV * LANES)
    w_proj = jnp.concatenate([wi[:, :2560], wi[:, 2584:], gt_cols], axis=1).astype(BF16)
    g_mix = norm_mix[l].reshape(1, D_MODEL)
    wfull = jnp.tile(jnp.repeat(cmp_pool_w[l], HEAD_DIM, axis=1), (1, N_KV))
    wa_t = _block_diag_tiles(lru_wa[l])
    wx_t = _block_diag_tiles(lru_wx[l])
    row = lambda v: v.reshape(1, -1)
    slopes = jnp.exp2(-8.0 * jnp.arange(1, N_HEADS + 1, dtype=F32) / N_HEADS)
    w_o = w_out[l].astype(BF16)
    w_router = jnp.pad(jnp.concatenate([w_router_expert[l], w_router_group[l]], axis=1),
                       ((0, 0), (0, LANES - N_EXPERTS - N_GROUPS)))
    b_router = jnp.pad(jnp.concatenate([b_router_expert[l], b_router_group[l]]),
                       (0, LANES - N_EXPERTS - N_GROUPS)).reshape(1, LANES)
    wg, wu, wd = w_exp_gate[l].astype(BF16), w_exp_up[l].astype(BF16), w_exp_down[l].astype(BF16)
    lru = (conv_w[l], row(conv_b[l]), wa_t, row(lru_ba[l]), wx_t, row(lru_bx[l]), row(lru_lambda[l]))

    def tail(x, attn_o, rnn_o, tm_f, tm_m):
        h, xn, comb = _finish(x, attn_o, rnn_o, w_o, row(norm_ffn[l]), w_router, b_router, tm_f)
        return _moe(xn, comb, h, wg, wu, wd, final_gain, tm_m)

    np_ = bp * sp
    q, kc, ks, kw, xr, xg, gt = _project(xp.reshape(np_, D_MODEL), g_mix, w_proj, 256)
    shp = lambda a: a.reshape(bp, sp, a.shape[-1])
    ident = jnp.arange(np_ // 1024, dtype=jnp.int32)
    cmp_p = _compress(ident, kc.reshape(np_ // 1024, 1024, KV_W), wfull, bp, sp // 1024, 1, 1024)
    attn_p = _nsa_prompt(slopes, shp(q), shp(gt), cmp_p, shp(ks), shp(kw))
    rnn_p, h_p = _rglru(shp(xr), shp(xg), jnp.zeros((bp, 8, D_RNN), F32), jnp.zeros((bp, 1, D_RNN), F32), *lru, tc=512)
    y_p = tail(xp.reshape(np_, D_MODEL), attn_p.reshape(np_, D_ATTN), rnn_p.reshape(np_, D_RNN), 256, 512)
    kv6 = lambda a, b_, t_: a.reshape(b_, t_, N_KV, 2, HEAD_DIM)
    outs_p = (y_p.reshape(bp, sp, D_MODEL), kv6(kc, bp, sp), kv6(ks, bp, sp),
              kv6(kw, bp, sp)[:, -min(WINDOW, sp):], shp(xr)[:, sp - (CONV_W - 1):], h_p.reshape(bp, D_RNN))

    ns_ = bs * ts
    q, kc, ks, kw, xr, xg, gt = _project(xs.reshape(ns_, D_MODEL), g_mix, w_proj, ns_)
    shs = lambda a: a.reshape(bs, ts, a.shape[-1])
    pool_c = cache_cmp_kv[l].reshape(-1, PAGE_SIZE, KV_W)
    cmp_s = _compress(page_table.reshape(-1), pool_c, wfull, bs, n_pages // 8, 8, PAGE_SIZE)
    cache_w = cache_win_kv[l].reshape(bs, -1, KV_W)
    idx, o_cw = _nsa_sample_a(past, shs(q), shs(gt), cmp_s, cache_w, shs(kw))
    idx = idx.reshape(bs, N_KV, ts, LANES)[..., :N_GATHER].reshape(-1)
    gate_sel = gt.reshape(ns_, N_KV, LANES)[:, :, :GQA * 3].reshape(ns_, N_HEADS, 3)[:, :, 1:2]
    gate_sel = jnp.broadcast_to(gate_sel, (ns_, N_HEADS, HEAD_DIM))
    slopes8 = jnp.broadcast_to(slopes.reshape(N_HEADS, 1), (N_HEADS, LANES))
    pool_s = cache_sel_kv[l].reshape(-1, SEL_BLOCK, KV_W)
    attn_s = _nsa_sample_b(past, idx, page_table.reshape(-1), q.reshape(ns_, N_HEADS, HEAD_DIM), gate_sel, slopes8,
                           shs(ks), o_cw.reshape(ns_, N_HEADS, HEAD_DIM), pool_s)
    conv8 = jnp.pad(state_conv[l], ((0, 0), (8 - (CONV_W - 1), 0), (0, 0)))
    rnn_s, h_s = _rglru(shs(xr), shs(xg), conv8, state_h[l].reshape(bs, 1, D_RNN), *lru, tc=ts)
    y_s = tail(xs.reshape(ns_, D_MODEL), attn_s.reshape(ns_, D_ATTN), rnn_s.reshape(ns_, D_RNN), ns_, ns_)
    win_s = jnp.concatenate([cache_w, shs(kw)], axis=1)[:, ts:]
    conv_s = jnp.concatenate([state_conv[l], shs(xr)], axis=1)[:, ts:]
    outs_s = (y_s.reshape(bs, ts, D_MODEL), kv6(kc, bs, ts), kv6(ks, bs, ts),
              win_s.reshape(bs, -1, N_KV, 2, HEAD_DIM), conv_s, h_s.reshape(bs, D_RNN))
    return outs_p, outs_s


def kernel(x_prompt, x_sample, cache_cmp_kv, cache_sel_kv, cache_win_kv, state_conv, state_h, page_table, norm_mix, w_in, cmp_pool_w, conv_w, conv_b, lru_wa, lru_ba, lru_wx, lru_bx, lru_lambda, w_out, norm_ffn, w_router_group, b_router_group, w_router_expert, b_router_expert, w_exp_gate, w_exp_up, w_exp_down, norm_final):
    depth = w_in.shape[0]
    assert depth == 1, "the final norm is fused into the single layer's expert kernel"
    p, s = _layer(0, x_prompt, x_sample, cache_cmp_kv, cache_sel_kv, cache_win_kv, state_conv, state_h, page_table,
                  norm_mix, w_in, cmp_pool_w, conv_w, conv_b, lru_wa, lru_ba, lru_wx, lru_bx, lru_lambda, w_out,
                  norm_ffn, w_router_group, b_router_group, w_router_expert, b_router_expert,
                  w_exp_gate, w_exp_up, w_exp_down, norm_final.reshape(1, D_MODEL))
    st = lambda a: a[None]
    return (p[0], s[0], st(p[1]), st(s[1]), st(p[2]), st(s[2]), st(p[3]), st(s[3]),
            st(p[4]), st(s[4]), st(p[5]), st(s[5]))
```

```python
import functools

import jax
import jax.numpy as jnp
from jax import lax
from jax.experimental import pallas as pl
from jax.experimental.pallas import tpu as pltpu

F32 = jnp.float32
BF16 = jnp.bfloat16

D_MODEL = 2048
D_ATTN = 1024
D_RNN = 1024
N_HEADS = 8
HEAD_DIM = 128
N_KV = 2
GQA = 4
KV_W = 512
CMP_BLOCK = 32
SEL_BLOCK = 64
TOP_N = 16
WINDOW = 512
FORCE_SCORE = 1e4
RNN_BLOCKS = 16
RNN_BLOCK_DIM = 64
CONV_W = 4
LRU_C = 8.0
N_GROUPS = 4
EXP_PER_GROUP = 4
N_EXPERTS = 16
D_EXPERT = 512
RMS_EPS = 1e-6
PAGE_SIZE = 128

LANES = 128
NEG = -1e30
VMEM_LIMIT = 56 * 1024 * 1024

_SEG_Q = (0, 1024)
_SEG_KC = (1024, 1536)
_SEG_KS = (1536, 2048)
_SEG_KW = (2048, 2560)
_SEG_XR = (2560, 3584)
_SEG_XG = (3584, 4608)
_SEG_GT = (4608, 4864)
_PROJ_W = 4864


def _dot(a, b):
    return jnp.dot(a, b, preferred_element_type=F32)


def _dot_nt(a, b):
    return lax.dot_general(a, b, (((1,), (1,)), ((), ())), preferred_element_type=F32)


def _rms(x, g):
    return x * lax.rsqrt(jnp.mean(x * x, axis=-1, keepdims=True) + RMS_EPS) * g


def _split3(x):
    h1 = x.astype(BF16)
    r1 = x - h1.astype(F32)
    h2 = r1.astype(BF16)
    h3 = (r1 - h2.astype(F32)).astype(BF16)
    return h1, h2, h3


def _dot_f32(a, b):
    a1, a2, a3 = _split3(a)
    b1, b2, b3 = _split3(b)
    return (_dot(a3, b1) + _dot(a2, b2) + _dot(a1, b3)) + (_dot(a2, b1) + _dot(a1, b2)) + _dot(a1, b1)


def _dot_sel(a, m01):
    a1, a2, a3 = _split3(a)
    return (_dot(a3, m01) + _dot(a2, m01)) + _dot(a1, m01)


def _proj_kernel(x_ref, g_ref, w_ref, q_ref, kc_ref, ks_ref, kw_ref, xr_ref, xg_ref, gt_ref):
    xn = _rms(x_ref[...], g_ref[...]).astype(BF16)
    outs = ((q_ref, _SEG_Q), (kc_ref, _SEG_KC), (ks_ref, _SEG_KS), (kw_ref, _SEG_KW),
            (xr_ref, _SEG_XR), (xg_ref, _SEG_XG), (gt_ref, _SEG_GT))
    for ref, (lo, hi) in outs:
        ref[...] = _dot(xn, w_ref[:, lo:hi])


def _project(x, g, w, tm):
    n = x.shape[0]
    widths = [hi - lo for lo, hi in (_SEG_Q, _SEG_KC, _SEG_KS, _SEG_KW, _SEG_XR, _SEG_XG, _SEG_GT)]
    return pl.pallas_call(
        _proj_kernel,
        grid=(n // tm,),
        in_specs=[pl.BlockSpec((tm, D_MODEL), lambda i: (i, 0)),
                  pl.BlockSpec((1, D_MODEL), lambda i: (0, 0)),
                  pl.BlockSpec((D_MODEL, _PROJ_W), lambda i: (0, 0), pipeline_mode=pl.Buffered(1))],
        out_specs=[pl.BlockSpec((tm, w_), lambda i: (i, 0)) for w_ in widths],
        out_shape=[jax.ShapeDtypeStruct((n, w_), F32) for w_ in widths],
        compiler_params=pltpu.CompilerParams(dimension_semantics=("arbitrary",), vmem_limit_bytes=VMEM_LIMIT),
        name="proj",
    )(x, g, w)


def _compress_kernel(n_pages, pt_ref, *refs):
    del pt_ref
    pages, w_ref, o_ref = refs[:n_pages], refs[n_pages], refs[n_pages + 1]
    if n_pages == 1:
        x = pages[0][0]
    else:
        x = jnp.concatenate([p[0] for p in pages], axis=0)
    rows = x.shape[0]
    xb = x.reshape(rows // CMP_BLOCK, CMP_BLOCK, KV_W) * w_ref[...][None]
    o_ref[0] = jnp.sum(xb, axis=1)


def _compress(table, src, wfull, n_batch, steps, n_pages, page_rows):
    out_rows = n_pages * page_rows // CMP_BLOCK
    per_b = steps * n_pages

    def page_spec(k):
        return pl.BlockSpec((1, page_rows, KV_W), lambda b, j, pt: (pt[b * per_b + j * n_pages + k], 0, 0))

    return pl.pallas_call(
        functools.partial(_compress_kernel, n_pages),
        grid_spec=pltpu.PrefetchScalarGridSpec(
            num_scalar_prefetch=1,
            grid=(n_batch, steps),
            in_specs=[page_spec(k) for k in range(n_pages)] + [pl.BlockSpec((CMP_BLOCK, KV_W), lambda b, j, pt: (0, 0))],
            out_specs=pl.BlockSpec((1, out_rows, KV_W), lambda b, j, pt: (b, j, 0)),
        ),
        out_shape=jax.ShapeDtypeStruct((n_batch, steps * out_rows, KV_W), F32),
        compiler_params=pltpu.CompilerParams(dimension_semantics=("arbitrary", "arbitrary"), vmem_limit_bytes=VMEM_LIMIT),
        name="compress",
    )(table, *([src] * n_pages), wfull)


def _softplus(x):
    return jnp.maximum(x, 0.0) + jnp.log1p(jnp.exp(-jnp.abs(x)))


def _rglru_kernel(tc, xr_ref, xg_ref, cs_ref, h0_ref, cw_ref, cb_ref, wa_ref, ba_ref, wx_ref, bx_ref, lam_ref,
                  o_ref, hl_ref, tail_scr, h_scr):
    @pl.when(pl.program_id(2) == 0)
    def _():
        tail_scr[...] = cs_ref[0]
        h_scr[...] = h0_ref[0]

    x = xr_ref[0]
    xp = jnp.concatenate([tail_scr[...], x], axis=0)
    w = cw_ref[...]
    xc = cb_ref[...] + pltpu.roll(xp, 3, axis=0)[8:] * w[0:1]
    xc = xc + pltpu.roll(xp, 2, axis=0)[8:] * w[1:2]
    xc = xc + pltpu.roll(xp, 1, axis=0)[8:] * w[2:3]
    xc = xc + x * w[3:4]
    tail_scr[...] = xp[tc:]

    xcb = xc.astype(BF16)
    r = jax.nn.sigmoid(_dot(xcb, wa_ref[0]) + ba_ref[...])
    gi = jax.nn.sigmoid(_dot(xcb, wx_ref[0]) + bx_ref[...])
    log_a = -LRU_C * r * _softplus(-lam_ref[...])
    a = jnp.exp(log_a)
    u = jnp.sqrt(-jnp.tanh(log_a) * (a * a + 1.0)) * (gi * xc)

    row = lax.broadcasted_iota(jnp.int32, a.shape, 0)
    s = 1
    while s < tc:
        keep = row >= s
        a_sh = jnp.where(keep, pltpu.roll(a, s, axis=0), 1.0)
        u_sh = jnp.where(keep, pltpu.roll(u, s, axis=0), 0.0)
        u = a * u_sh + u
        a = a * a_sh
        s *= 2
    h = a * h_scr[...] + u
    h_last = h[tc - 1:tc]
    h_scr[...] = h_last
    hl_ref[0] = h_last
    o_ref[0] = h * jax.nn.gelu(xg_ref[0])


def _rglru(xr, xg, conv_state8, h0, conv_w, conv_b, wa_t, ba, wx_t, bx, lam, tc):
    b, t, _ = xr.shape
    ct = 256
    n_ct = D_RNN // ct
    seq = lambda bi, c, j: (bi, j, c)
    per_b = lambda bi, c, j: (bi, 0, c)
    par = lambda bi, c, j: (0, c)
    return pl.pallas_call(
        functools.partial(_rglru_kernel, tc),
        grid=(b, n_ct, t // tc),
        in_specs=[pl.BlockSpec((1, tc, ct), seq), pl.BlockSpec((1, tc, ct), seq),
                  pl.BlockSpec((1, 8, ct), per_b), pl.BlockSpec((1, 1, ct), per_b),
                  pl.BlockSpec((CONV_W, ct), par), pl.BlockSpec((1, ct), par),
                  pl.BlockSpec((1, ct, ct), lambda bi, c, j: (c, 0, 0)), pl.BlockSpec((1, ct), par),
                  pl.BlockSpec((1, ct, ct), lambda bi, c, j: (c, 0, 0)), pl.BlockSpec((1, ct), par),
                  pl.BlockSpec((1, ct), par)],
        out_specs=[pl.BlockSpec((1, tc, ct), seq), pl.BlockSpec((1, 1, ct), per_b)],
        out_shape=[jax.ShapeDtypeStruct((b, t, D_RNN), F32), jax.ShapeDtypeStruct((b, 1, D_RNN), F32)],
        scratch_shapes=[pltpu.VMEM((8, ct), F32), pltpu.VMEM((1, ct), F32)],
        compiler_params=pltpu.CompilerParams(dimension_semantics=("arbitrary", "arbitrary", "arbitrary"),
                                             vmem_limit_bytes=VMEM_LIMIT),
        name="rglru",
    )(xr, xg, conv_state8, h0, conv_w, conv_b, wa_t, ba, wx_t, bx, lam)


TQ = 128
W4 = GQA * TQ


def _nsa_prompt_kernel(slopes_ref, q_ref, gt_ref, kc_ref, vc_ref, ks_ref, vs_ref, kw_ref, vw_ref, o_ref,
                       q_scr, kcb, vct, ksb, vst, kwb, vwt, bias0, caus, wlow,
                       impt_scr, selb_scr, m_scr, l_scr, acc_scr, out_scr):
    g = pl.program_id(1)
    i = pl.program_id(2)
    s_len = ks_ref.shape[1]
    n_cmp = kc_ref.shape[1]
    n_sel = n_cmp // 2
    lane = lax.broadcasted_iota(jnp.int32, (1, W4), 1)
    tl_row = (lane % TQ).astype(F32)
    slope_row = jnp.full((1, W4), slopes_ref[g * GQA + GQA - 1], F32)
    for r in reversed(range(GQA - 1)):
        slope_row = jnp.where(lane < (r + 1) * TQ, slopes_ref[g * GQA + r], slope_row)

    @pl.when(i == 0)
    def _prepare():
        kcb[...] = kc_ref[0].astype(BF16)
        vct[...] = vc_ref[0].T.astype(BF16)

        def cast(j, carry):
            off = pl.multiple_of(j * LANES, LANES)
            ksb[pl.ds(off, LANES), :] = ks_ref[0, pl.ds(off, LANES), :].astype(BF16)
            kwb[pl.ds(off, LANES), :] = kw_ref[0, pl.ds(off, LANES), :].astype(BF16)
            vst[:, pl.ds(off, LANES)] = vs_ref[0, pl.ds(off, LANES), :].T.astype(BF16)
            vwt[:, pl.ds(off, LANES)] = vw_ref[0, pl.ds(off, LANES), :].T.astype(BF16)
            return carry

        lax.fori_loop(0, s_len // LANES, cast, 0)
        rel = tl_row - lax.broadcasted_iota(jnp.int32, (LANES, W4), 0).astype(F32)
        bias0[...] = slope_row * rel
        caus[...] = jnp.where(rel >= 0, 0.0, NEG)
        wlow[...] = jnp.where(rel <= 0, 0.0, NEG)

    start_f = (i * TQ).astype(F32)
    scale = HEAD_DIM ** -0.5
    for r in range(GQA):
        q_scr[r * TQ:(r + 1) * TQ, :] = (q_ref[0, :, r * HEAD_DIM:(r + 1) * HEAD_DIM] * scale).astype(BF16)
    gate_t = jax.nn.sigmoid(gt_ref[0]).T

    def gate_row(branch):
        return jnp.concatenate([gate_t[3 * r + branch:3 * r + branch + 1, :] for r in range(GQA)], axis=1)

    t_row = start_f + tl_row
    c_end = ((lax.broadcasted_iota(jnp.int32, (n_cmp, W4), 0) + 1) * CMP_BLOCK - 1).astype(F32)
    dist_c = t_row - c_end
    ok_c = dist_c >= 0
    x = jnp.where(ok_c, _dot_nt(kcb[...], q_scr[...]) - slope_row * dist_c, NEG)
    e = jnp.where(ok_c, jnp.exp(x - jnp.max(x, axis=0, keepdims=True)), 0.0)
    p = e * (1.0 / jnp.maximum(jnp.sum(e, axis=0, keepdims=True), 1e-30))
    out_scr[...] = gate_row(0) * _dot(vct[...], p.astype(BF16))

    p_heads = p[:, 0:TQ]
    for r in range(1, GQA):
        p_heads = p_heads + p[:, r * TQ:(r + 1) * TQ]
    impt_scr[...] = p_heads
    imp = impt_scr[pl.ds(0, n_sel, stride=2), :] + impt_scr[pl.ds(1, n_sel, stride=2), :]
    blk = lax.broadcasted_iota(jnp.int32, (n_sel, TQ), 0)
    cur = (i * TQ + lax.broadcasted_iota(jnp.int32, (n_sel, TQ), 1)) // SEL_BLOCK
    forced = (blk == 0) | (blk == cur) | (blk == cur - 1)
    imp = jnp.where(forced, FORCE_SCORE, imp)
    imp = jnp.where(blk > cur, NEG, imp)
    rank = jnp.zeros((n_sel, TQ), jnp.int32)
    for j in range(n_sel):
        row = imp[j:j + 1, :]
        tie = jnp.where(blk > j, jnp.where(row == imp, 1, 0), 0)
        rank = rank + jnp.where(row > imp, 1, tie)
    selb = jnp.where(rank < TOP_N, 0.0, NEG)
    for j in range(n_sel):
        selb_scr[j] = selb[j:j + 1, :]

    half = lax.broadcasted_iota(jnp.int32, (LANES, TQ), 0) < SEL_BLOCK

    def chunk(k_b, v_t, c, bias_tile, selected):
        off = pl.multiple_of(c * LANES, LANES)
        x = _dot_nt(k_b[pl.ds(off, LANES), :], q_scr[...]) - bias0[...]
        if bias_tile is not None:
            x = x + bias_tile[...]
        if selected:
            sb = jnp.where(half, selb_scr[2 * c], selb_scr[2 * c + 1])
            x = x + jnp.concatenate([sb] * GQA, axis=1)
        r_c = slope_row * ((c * LANES).astype(F32) - start_f)
        m_old = m_scr[...]
        m_new = jnp.maximum(m_old, jnp.max(x, axis=0, keepdims=True) + r_c)
        pr = jnp.exp(x + (r_c - m_new))
        alpha = jnp.exp(m_old - m_new)
        l_scr[...] = alpha * l_scr[...] + jnp.sum(pr, axis=0, keepdims=True)
        acc_scr[...] = alpha * acc_scr[...] + _dot(v_t[:, pl.ds(off, LANES)], pr.astype(BF16))
        m_scr[...] = m_new

    def reset():
        m_scr[...] = jnp.full((1, W4), NEG, F32)
        l_scr[...] = jnp.zeros((1, W4), F32)
        acc_scr[...] = jnp.zeros((HEAD_DIM, W4), F32)

    def result():
        return acc_scr[...] * (1.0 / jnp.maximum(l_scr[...], 1e-30))

    def loop(k_b, v_t, lo, hi, selected):
        def body(c, carry):
            chunk(k_b, v_t, c, None, selected)
            return carry
        lax.fori_loop(lo, hi, body, 0)

    reset()
    loop(ksb, vst, 0, i, True)
    chunk(ksb, vst, i, caus, True)
    out_scr[...] += gate_row(1) * result()

    reset()
    n_back = WINDOW // LANES

    @pl.when(i >= n_back)
    def _():
        chunk(kwb, vwt, i - n_back, wlow, False)

    loop(kwb, vwt, jnp.maximum(i - n_back + 1, 0), i, False)
    chunk(kwb, vwt, i, caus, False)
    o_t = out_scr[...] + gate_row(2) * result()
    for r in range(GQA):
        o_ref[0, :, r * HEAD_DIM:(r + 1) * HEAD_DIM] = o_t[:, r * TQ:(r + 1) * TQ].T


def _nsa_prompt(slopes, q, gt, cmp_kv, ks, kw):
    b, s, _ = q.shape
    n_cmp = cmp_kv.shape[1]
    k_of = lambda bi, g, i: (bi, 0, 2 * g)
    v_of = lambda bi, g, i: (bi, 0, 2 * g + 1)
    tile = pltpu.VMEM((LANES, W4), F32)
    return pl.pallas_call(
        _nsa_prompt_kernel,
        grid=(b, N_KV, s // TQ),
        in_specs=[pl.BlockSpec(memory_space=pltpu.SMEM),
                  pl.BlockSpec((1, TQ, GQA * HEAD_DIM), lambda bi, g, i: (bi, i, g)),
                  pl.BlockSpec((1, TQ, LANES), lambda bi, g, i: (bi, i, g)),
                  pl.BlockSpec((1, n_cmp, HEAD_DIM), k_of), pl.BlockSpec((1, n_cmp, HEAD_DIM), v_of),
                  pl.BlockSpec((1, s, HEAD_DIM), k_of), pl.BlockSpec((1, s, HEAD_DIM), v_of),
                  pl.BlockSpec((1, s, HEAD_DIM), k_of), pl.BlockSpec((1, s, HEAD_DIM), v_of)],
        out_specs=pl.BlockSpec((1, TQ, GQA * HEAD_DIM), lambda bi, g, i: (bi, i, g)),
        out_shape=jax.ShapeDtypeStruct((b, s, D_ATTN), F32),
        scratch_shapes=[pltpu.VMEM((W4, HEAD_DIM), BF16),
                        pltpu.VMEM((n_cmp, HEAD_DIM), BF16), pltpu.VMEM((HEAD_DIM, n_cmp), BF16),
                        pltpu.VMEM((s, HEAD_DIM), BF16), pltpu.VMEM((HEAD_DIM, s), BF16),
                        pltpu.VMEM((s, HEAD_DIM), BF16), pltpu.VMEM((HEAD_DIM, s), BF16),
                        tile, tile, tile,
                        pltpu.VMEM((n_cmp, TQ), F32),
                        pltpu.VMEM((n_cmp // 2, 1, TQ), F32),
                        pltpu.VMEM((1, W4), F32), pltpu.VMEM((1, W4), F32),
                        pltpu.VMEM((HEAD_DIM, W4), F32), pltpu.VMEM((HEAD_DIM, W4), F32)],
        compiler_params=pltpu.CompilerParams(dimension_semantics=("arbitrary", "arbitrary", "arbitrary"),
                                             vmem_limit_bytes=VMEM_LIMIT),
        name="nsa_prompt",
    )(slopes, q, gt, cmp_kv, cmp_kv, ks, ks, kw, kw)


KV_ROWS = 2 * N_KV


def _compress_paged_kernel(n_pages, pt_ref, *refs):
    del pt_ref
    pages, w_ref, o_ref = refs[:n_pages], refs[n_pages], refs[n_pages + 1]
    for gc in range(KV_ROWS):
        x = jnp.concatenate([p[pl.ds(gc, PAGE_SIZE, stride=KV_ROWS), :] for p in pages], axis=0)
        xb = x.reshape(n_pages * PAGE_SIZE // CMP_BLOCK, CMP_BLOCK, HEAD_DIM) * w_ref[gc % 2][None]
        o_ref[0, gc] = jnp.sum(xb, axis=1)


def _compress_paged(table, pool2d, w2, n_batch, pages_per_b, n_pages):
    steps = pages_per_b // n_pages
    out_rows = n_pages * PAGE_SIZE // CMP_BLOCK
    rows = PAGE_SIZE * KV_ROWS

    def page_spec(k):
        return pl.BlockSpec((rows, HEAD_DIM), lambda b, j, pt: (pt[b * pages_per_b + j * n_pages + k], 0))

    return pl.pallas_call(
        functools.partial(_compress_paged_kernel, n_pages),
        grid_spec=pltpu.PrefetchScalarGridSpec(
            num_scalar_prefetch=1,
            grid=(n_batch, steps),
            in_specs=[page_spec(k) for k in range(n_pages)]
            + [pl.BlockSpec((2, CMP_BLOCK, HEAD_DIM), lambda b, j, pt: (0, 0, 0))],
            out_specs=pl.BlockSpec((1, KV_ROWS, out_rows, HEAD_DIM), lambda b, j, pt: (b, 0, j, 0)),
        ),
        out_shape=jax.ShapeDtypeStruct((n_batch, KV_ROWS, steps * out_rows, HEAD_DIM), F32),
        compiler_params=pltpu.CompilerParams(dimension_semantics=("arbitrary", "arbitrary"), vmem_limit_bytes=VMEM_LIMIT),
        name="compress_paged",
    )(table, *([pool2d] * n_pages), w2)


def _nsa_sample_a_kernel(past, q_ref, gt_ref, cmp_ref, cw_ref, kwn_ref, idx_ref, o_ref):
    t_new = q_ref.shape[1]
    n_cmp = cmp_ref.shape[2]
    n_past_sel = n_cmp // 2
    n_win = cw_ref.shape[0] // KV_ROWS
    rows = GQA * t_new
    gates = jax.nn.sigmoid(gt_ref[0])
    t_row = past + lax.broadcasted_iota(jnp.int32, (rows, 1), 0) % t_new
    scale = HEAD_DIM ** -0.5
    pair = jnp.where(lax.broadcasted_iota(jnp.int32, (n_cmp, n_past_sel), 0) // 2
                     == lax.broadcasted_iota(jnp.int32, (n_cmp, n_past_sel), 1), 1.0, 0.0).astype(BF16)
    imps = []
    for g in range(N_KV):
        qg = jnp.concatenate([q_ref[0, :, (g * GQA + r) * HEAD_DIM:(g * GQA + r + 1) * HEAD_DIM]
                              for r in range(GQA)], axis=0)
        qg = (qg * scale).astype(BF16)
        slope = jnp.concatenate([jnp.full((t_new, 1), 2.0 ** -(g * GQA + r + 1), F32) for r in range(GQA)], axis=0)
        kcol = g * 2 * HEAD_DIM
        kc = cmp_ref[0, 2 * g].astype(BF16)
        vc = cmp_ref[0, 2 * g + 1].astype(BF16)
        c_end = (lax.broadcasted_iota(jnp.int32, (1, n_cmp), 1) + 1) * CMP_BLOCK - 1
        dist_c = t_row - c_end
        mask_c = dist_c >= 0
        s = _dot_nt(qg, kc) - slope * dist_c.astype(F32)
        s = jnp.where(mask_c, s, NEG)
        m = jnp.max(s, axis=-1, keepdims=True)
        p = jnp.where(mask_c, jnp.exp(s - m), 0.0)
        p = p / jnp.maximum(jnp.sum(p, axis=-1, keepdims=True), 1e-30)
        o_c = _dot(p.astype(BF16), vc)
        p_heads = p[0:t_new]
        for r in range(1, GQA):
            p_heads = p_heads + p[r * t_new:(r + 1) * t_new]
        imps.append(_dot_sel(p_heads, pair))
        n_pad = LANES - t_new
        kw = jnp.concatenate([cw_ref[pl.ds(2 * g, n_win, stride=KV_ROWS), :], kwn_ref[0, :, kcol:kcol + HEAD_DIM],
                              jnp.zeros((n_pad, HEAD_DIM), F32)], axis=0).astype(BF16)
        vw = jnp.concatenate([cw_ref[pl.ds(2 * g + 1, n_win, stride=KV_ROWS), :],
                              kwn_ref[0, :, kcol + HEAD_DIM:kcol + 2 * HEAD_DIM],
                              jnp.zeros((n_pad, HEAD_DIM), F32)], axis=0).astype(BF16)
        win_pos = past - n_win + lax.broadcasted_iota(jnp.int32, (1, n_win + LANES), 1)
        dist_w = t_row - win_pos
        mask_w = (dist_w >= 0) & (dist_w <= WINDOW)
        s = _dot_nt(qg, kw) - slope * dist_w.astype(F32)
        s = jnp.where(mask_w, s, NEG)
        m = jnp.max(s, axis=-1, keepdims=True)
        p = jnp.where(mask_w, jnp.exp(s - m), 0.0)
        p = p / jnp.maximum(jnp.sum(p, axis=-1, keepdims=True), 1e-30)
        o_w = _dot(p.astype(BF16), vw)
        for r in range(GQA):
            h = g * GQA + r
            gl = g * LANES + 3 * r
            o_ref[0, :, h * HEAD_DIM:(h + 1) * HEAD_DIM] = (
                gates[:, gl:gl + 1] * o_c[r * t_new:(r + 1) * t_new]
                + gates[:, gl + 2:gl + 3] * o_w[r * t_new:(r + 1) * t_new])

    imp = jnp.concatenate(imps, axis=0)
    n_rows = N_KV * t_new
    lane = lax.broadcasted_iota(jnp.int32, (n_rows, n_past_sel), 1)
    lane_f = lane.astype(F32)
    cur = (past + lax.broadcasted_iota(jnp.int32, (n_rows, n_past_sel), 0) % t_new) // SEL_BLOCK
    forced = (lane == 0) | (lane == cur) | (lane == cur - 1)
    imp = jnp.where(forced, FORCE_SCORE, imp)
    imp = jnp.where(lane > cur, NEG, imp)
    out_lane = lax.broadcasted_iota(jnp.int32, (n_rows, LANES), 1)
    idx = jnp.zeros((n_rows, LANES), F32)
    for k in range(TOP_N - 1):
        m = jnp.max(imp, axis=-1, keepdims=True)
        j = jnp.min(jnp.where(imp == m, lane_f, float(n_past_sel)), axis=-1, keepdims=True)
        idx = jnp.where(out_lane == k, j, idx)
        imp = jnp.where(lane_f == j, -3e38, imp)
    idx_ref[0] = idx.astype(jnp.int32)


def _nsa_sample_a(past, q, gt, cmp_kv, cache_w2d, kw_new):
    b, t_new, _ = q.shape
    win_rows = cache_w2d.shape[0] // b
    per_b3 = lambda bi: (bi, 0, 0)
    return pl.pallas_call(
        functools.partial(_nsa_sample_a_kernel, past),
        grid=(b,),
        in_specs=[pl.BlockSpec((1, t_new, D_ATTN), per_b3), pl.BlockSpec((1, t_new, 2 * LANES), per_b3),
                  pl.BlockSpec((1,) + cmp_kv.shape[1:], lambda bi: (bi, 0, 0, 0)),
                  pl.BlockSpec((win_rows, HEAD_DIM), lambda bi: (bi, 0)),
                  pl.BlockSpec((1, t_new, KV_W), per_b3)],
        out_specs=[pl.BlockSpec((1, N_KV * t_new, LANES), per_b3), pl.BlockSpec((1, t_new, D_ATTN), per_b3)],
        out_shape=[jax.ShapeDtypeStruct((b, N_KV * t_new, LANES), jnp.int32),
                   jax.ShapeDtypeStruct((b, t_new, D_ATTN), F32)],
        compiler_params=pltpu.CompilerParams(dimension_semantics=("arbitrary",), vmem_limit_bytes=VMEM_LIMIT),
        name="nsa_sample_a",
    )(q, gt, cmp_kv, cache_w2d, kw_new)


N_GATHER = TOP_N - 1
BLOCK_ROWS = SEL_BLOCK * KV_ROWS
GATHER_KEYS = TOP_N * SEL_BLOCK


def _nsa_sample_b_kernel(past, t_new, idx_ref, pt_ref, q_ref, gs_ref, slope_ref, ocw_ref, new_ref, pool_ref,
                         o_ref, kv_buf, sems):
    n_steps = pl.num_programs(0) * t_new
    step = pl.program_id(0) * t_new + pl.program_id(1)
    slot = step % 2
    pages_per_b = past // PAGE_SIZE
    sel_per_page = PAGE_SIZE // SEL_BLOCK

    def block_index(st, g, k):
        return idx_ref[((st // t_new * N_KV + g) * t_new + st % t_new) * N_GATHER + k]

    def copies(st, sl):
        out = []
        for g in range(N_KV):
            for k in range(N_GATHER):
                lp = block_index(st, g, k)
                phys = pt_ref[st // t_new * pages_per_b + lp // sel_per_page] * sel_per_page + lp % sel_per_page
                out.append(pltpu.make_async_copy(
                    pool_ref.at[pl.ds(pl.multiple_of(phys * BLOCK_ROWS, BLOCK_ROWS), BLOCK_ROWS), :],
                    kv_buf.at[sl, g, pl.ds(k * BLOCK_ROWS, BLOCK_ROWS), :], sems.at[sl, g, k]))
            out.append(pltpu.make_async_copy(
                new_ref.at[st // t_new], kv_buf.at[sl, g, pl.ds(N_GATHER * BLOCK_ROWS, BLOCK_ROWS), :],
                sems.at[sl, g, N_GATHER]))
        return out

    @pl.when(step == 0)
    def _():
        for cp in copies(step, slot):
            cp.start()

    @pl.when(step + 1 < n_steps)
    def _():
        for cp in copies(step + 1, 1 - slot):
            cp.start()

    t_pos = past + pl.program_id(1)
    q8 = (q_ref[0] * HEAD_DIM ** -0.5).astype(BF16)
    slope = slope_ref[:, 0:1]
    lane = lax.broadcasted_iota(jnp.int32, (1, GATHER_KEYS), 1)
    for cp in copies(step, slot):
        cp.wait()
    outs = []
    for g in range(N_KV):
        pos = past + (lane - N_GATHER * SEL_BLOCK)
        for k in range(N_GATHER):
            pos = jnp.where(lane // SEL_BLOCK == k, block_index(step, g, k) * SEL_BLOCK + lane % SEL_BLOCK, pos)
        kk = kv_buf[slot, g, pl.ds(2 * g, GATHER_KEYS, stride=KV_ROWS), :].astype(BF16)
        vv = kv_buf[slot, g, pl.ds(2 * g + 1, GATHER_KEYS, stride=KV_ROWS), :].astype(BF16)
        dist = t_pos - pos
        mask = dist >= 0
        s = _dot_nt(q8, kk) - slope * dist.astype(F32)
        s = jnp.where(mask, s, NEG)
        m = jnp.max(s, axis=-1, keepdims=True)
        p = jnp.where(mask, jnp.exp(s - m), 0.0)
        p = p / jnp.maximum(jnp.sum(p, axis=-1, keepdims=True), 1e-30)
        outs.append(_dot(p.astype(BF16), vv))
    head = lax.broadcasted_iota(jnp.int32, (N_HEADS, HEAD_DIM), 0)
    o_s = jnp.where(head < GQA, outs[0], outs[1])
    o_ref[0] = ocw_ref[0] + jax.nn.sigmoid(gs_ref[0]) * o_s


def _nsa_sample_b(past, idx, page_table, q, gate_sel, slopes8, o_cw, new_blocks, pool2d):
    n_tok = q.shape[0]
    b = new_blocks.shape[0]
    t_new = n_tok // b
    tok = lambda bi, ti, *_: (bi * t_new + ti, 0, 0)
    return pl.pallas_call(
        functools.partial(_nsa_sample_b_kernel, past, t_new),
        grid_spec=pltpu.PrefetchScalarGridSpec(
            num_scalar_prefetch=2,
            grid=(b, t_new),
            in_specs=[pl.BlockSpec((1, N_HEADS, HEAD_DIM), tok), pl.BlockSpec((1, N_HEADS, HEAD_DIM), tok),
                      pl.BlockSpec((N_HEADS, LANES), lambda bi, ti, *_: (0, 0)),
                      pl.BlockSpec((1, N_HEADS, HEAD_DIM), tok),
                      pl.BlockSpec(memory_space=pl.ANY), pl.BlockSpec(memory_space=pl.ANY)],
            out_specs=pl.BlockSpec((1, N_HEADS, HEAD_DIM), tok),
            scratch_shapes=[pltpu.VMEM((2, N_KV, TOP_N * BLOCK_ROWS, HEAD_DIM), F32),
                            pltpu.SemaphoreType.DMA((2, N_KV, TOP_N))],
        ),
        out_shape=jax.ShapeDtypeStruct((n_tok, N_HEADS, HEAD_DIM), F32),
        compiler_params=pltpu.CompilerParams(dimension_semantics=("arbitrary", "arbitrary"),
                                             vmem_limit_bytes=VMEM_LIMIT),
        name="nsa_sample_b",
    )(idx, page_table, q, gate_sel, slopes8, o_cw, new_blocks, pool2d)


def _finish_kernel(x_ref, a_ref, r_ref, wo_ref, gn_ref, wr_ref, br_ref, h_ref, xn_ref, comb_ref):
    h = (x_ref[...] + _dot(a_ref[...].astype(BF16), wo_ref[0:D_ATTN, :])
         + _dot(r_ref[...].astype(BF16), wo_ref[D_ATTN:D_MODEL, :]))
    h_ref[...] = h
    xn = _rms(h, gn_ref[...])
    xn_ref[...] = xn.astype(BF16)
    logit = _dot_f32(xn, wr_ref[...]) + br_ref[...]
    lane = lax.broadcasted_iota(jnp.int32, logit.shape, 1)
    lane_f = lane.astype(F32)
    is_g = (lane >= N_EXPERTS) & (lane < N_EXPERTS + N_GROUPS)
    gl = jnp.where(is_g, logit, NEG)
    g_max = jnp.max(gl, axis=-1, keepdims=True)
    g_star = jnp.min(jnp.where(gl == g_max, lane_f, 1e9), axis=-1, keepdims=True) - N_EXPERTS
    g_prob = 1.0 / jnp.sum(jnp.where(is_g, jnp.exp(gl - g_max), 0.0), axis=-1, keepdims=True)
    in_grp = (lane < N_EXPERTS) & ((lane // EXP_PER_GROUP).astype(F32) == g_star)
    el = jnp.where(in_grp, logit, NEG)
    e_max = jnp.max(el, axis=-1, keepdims=True)
    ee = jnp.where(in_grp, jnp.exp(el - e_max), 0.0)
    ep = jnp.where(in_grp, ee / jnp.sum(ee, axis=-1, keepdims=True), -1.0)
    p1 = jnp.max(ep, axis=-1, keepdims=True)
    i1 = jnp.min(jnp.where(ep == p1, lane_f, 1e9), axis=-1, keepdims=True)
    ep2 = jnp.where(lane_f == i1, -1.0, ep)
    p2 = jnp.max(ep2, axis=-1, keepdims=True)
    i2 = jnp.min(jnp.where(ep2 == p2, lane_f, 1e9), axis=-1, keepdims=True)
    tot = p1 + p2
    comb_ref[...] = (jnp.where(lane_f == i1, p1 / tot * g_prob, 0.0)
                     + jnp.where(lane_f == i2, p2 / tot * g_prob, 0.0))


def _finish(x, attn_o, rnn_o, w_out, g_ffn, w_router, b_router, tm):
    n = x.shape[0]
    row = lambda i: (i, 0)
    fixed = lambda i: (0, 0)
    return pl.pallas_call(
        _finish_kernel,
        grid=(n // tm,),
        in_specs=[pl.BlockSpec((tm, D_MODEL), row), pl.BlockSpec((tm, D_ATTN), row), pl.BlockSpec((tm, D_RNN), row),
                  pl.BlockSpec((D_MODEL, D_MODEL), fixed, pipeline_mode=pl.Buffered(1)),
                  pl.BlockSpec((1, D_MODEL), fixed),
                  pl.BlockSpec((D_MODEL, LANES), fixed), pl.BlockSpec((1, LANES), fixed)],
        out_specs=[pl.BlockSpec((tm, D_MODEL), row), pl.BlockSpec((tm, D_MODEL), row), pl.BlockSpec((tm, LANES), row)],
        out_shape=[jax.ShapeDtypeStruct((n, D_MODEL), F32), jax.ShapeDtypeStruct((n, D_MODEL), BF16),
                   jax.ShapeDtypeStruct((n, LANES), F32)],
        compiler_params=pltpu.CompilerParams(dimension_semantics=("arbitrary",), vmem_limit_bytes=VMEM_LIMIT),
        name="finish",
    )(x, attn_o, rnn_o, w_out, g_ffn, w_router, b_router)


def _moe_kernel(xn_ref, comb_ref, h_ref, wg_ref, wu_ref, wd_ref, gf_ref, y_ref, acc_scr):
    e = pl.program_id(1)

    @pl.when(e == 0)
    def _():
        acc_scr[...] = h_ref[...]

    xn = xn_ref[...]
    hg = _dot(xn, wg_ref[0])
    hu = _dot(xn, wu_ref[0])
    lane = lax.broadcasted_iota(jnp.int32, comb_ref.shape, 1)
    cw = jnp.sum(jnp.where(lane == e, comb_ref[...], 0.0), axis=-1, keepdims=True)
    hh = jax.nn.silu(hg) * hu * cw
    acc_scr[...] += _dot(hh.astype(BF16), wd_ref[0])

    @pl.when(e == pl.num_programs(1) - 1)
    def _():
        y_ref[...] = _rms(acc_scr[...], gf_ref[...])


def _moe(xn, comb, h, w_gate, w_up, w_down, g_final, tm):
    n = xn.shape[0]
    row = lambda i, e: (i, 0)
    return pl.pallas_call(
        _moe_kernel,
        grid=(n // tm, N_EXPERTS),
        in_specs=[pl.BlockSpec((tm, D_MODEL), row), pl.BlockSpec((tm, LANES), row), pl.BlockSpec((tm, D_MODEL), row),
                  pl.BlockSpec((1, D_MODEL, D_EXPERT), lambda i, e: (e, 0, 0)),
                  pl.BlockSpec((1, D_MODEL, D_EXPERT), lambda i, e: (e, 0, 0)),
                  pl.BlockSpec((1, D_EXPERT, D_MODEL), lambda i, e: (e, 0, 0)),
                  pl.BlockSpec((1, D_MODEL), lambda i, e: (0, 0))],
        out_specs=pl.BlockSpec((tm, D_MODEL), row),
        out_shape=jax.ShapeDtypeStruct((n, D_MODEL), F32),
        scratch_shapes=[pltpu.VMEM((tm, D_MODEL), F32)],
        compiler_params=pltpu.CompilerParams(dimension_semantics=("arbitrary", "arbitrary"),
                                             vmem_limit_bytes=VMEM_LIMIT),
        name="moe",
    )(xn, comb, h, w_gate, w_up, w_down, g_final)


def _block_diag_tiles(w):
    per = 256 // RNN_BLOCK_DIM
    w4 = w.reshape(RNN_BLOCKS // per, per, RNN_BLOCK_DIM, RNN_BLOCK_DIM)
    eye = jnp.eye(per, dtype=w.dtype)
    tiles = jnp.einsum('tpde,pq->tpdqe', w4, eye)
    return tiles.reshape(RNN_BLOCKS // per, 256, 256).astype(BF16)


def _layer(l, xp, xs, cache_cmp_kv, cache_sel_kv, cache_win_kv, state_conv, state_h, page_table,
           norm_mix, w_in, cmp_pool_w, conv_w, conv_b, lru_wa, lru_ba, lru_wx, lru_bx, lru_lambda, w_out,
           norm_ffn, w_router_group, b_router_group, w_router_expert, b_router_expert,
           w_exp_gate, w_exp_up, w_exp_down, final_gain):
    bp, sp, _ = xp.shape
    bs, ts, _ = xs.shape
    n_pages = page_table.shape[1]
    past = n_pages * PAGE_SIZE

    wi = w_in[l]
    gt_cols = wi[:, 2560:2584].reshape(D_MODEL, N_KV, GQA * 3)
    gt_cols = jnp.pad(gt_cols, ((0, 0), (0, 0), (0, LANES - GQA * 3))).reshape(D_MODEL, N_KV * LANES)
    w_proj = jnp.concatenate([wi[:, :2560], wi[:, 2584:], gt_cols], axis=1).astype(BF16)
    g_mix = norm_mix[l].reshape(1, D_MODEL)
    wfull = jnp.tile(jnp.repeat(cmp_pool_w[l], HEAD_DIM, axis=1), (1, N_KV))
    w2 = jnp.broadcast_to(cmp_pool_w[l].T[:, :, None], (2, CMP_BLOCK, HEAD_DIM))
    wa_t = _block_diag_tiles(lru_wa[l])
    wx_t = _block_diag_tiles(lru_wx[l])
    row = lambda v: v.reshape(1, -1)
    slopes = jnp.exp2(-8.0 * jnp.arange(1, N_HEADS + 1, dtype=F32) / N_HEADS)
    w_o = w_out[l].astype(BF16)
    w_router = jnp.pad(jnp.concatenate([w_router_expert[l], w_router_group[l]], axis=1),
                       ((0, 0), (0, LANES - N_EXPERTS - N_GROUPS)))
    b_router = jnp.pad(jnp.concatenate([b_router_expert[l], b_router_group[l]]),
                       (0, LANES - N_EXPERTS - N_GROUPS)).reshape(1, LANES)
    wg, wu, wd = w_exp_gate[l].astype(BF16), w_exp_up[l].astype(BF16), w_exp_down[l].astype(BF16)
    lru = (conv_w[l], row(conv_b[l]), wa_t, row(lru_ba[l]), wx_t, row(lru_bx[l]), row(lru_lambda[l]))

    def tail(x, attn_o, rnn_o, tm_f, tm_m):
        h, xn, comb = _finish(x, attn_o, rnn_o, w_o, row(norm_ffn[l]), w_router, b_router, tm_f)
        return _moe(xn, comb, h, wg, wu, wd, final_gain, tm_m)

    np_ = bp * sp
    q, kc, ks, kw, xr, xg, gt = _project(xp.reshape(np_, D_MODEL), g_mix, w_proj, 256)
    shp = lambda a: a.reshape(bp, sp, a.shape[-1])
    ident = jnp.arange(np_ // 1024, dtype=jnp.int32)
    cmp_p = _compress(ident, kc.reshape(np_ // 1024, 1024, KV_W), wfull, bp, sp // 1024, 1, 1024)
    attn_p = _nsa_prompt(slopes, shp(q), shp(gt), cmp_p, shp(ks), shp(kw))
    rnn_p, h_p = _rglru(shp(xr), shp(xg), jnp.zeros((bp, 8, D_RNN), F32), jnp.zeros((bp, 1, D_RNN), F32), *lru, tc=512)
    y_p = tail(xp.reshape(np_, D_MODEL), attn_p.reshape(np_, D_ATTN), rnn_p.reshape(np_, D_RNN), 256, 512)
    kv6 = lambda a, b_, t_: a.reshape(b_, t_, N_KV, 2, HEAD_DIM)
    outs_p = (y_p.reshape(bp, sp, D_MODEL), kv6(kc, bp, sp), kv6(ks, bp, sp),
              kv6(kw, bp, sp)[:, -min(WINDOW, sp):], shp(xr)[:, sp - (CONV_W - 1):], h_p.reshape(bp, D_RNN))

    ns_ = bs * ts
    q, kc, ks, kw, xr, xg, gt = _project(xs.reshape(ns_, D_MODEL), g_mix, w_proj, ns_)
    shs = lambda a: a.reshape(bs, ts, a.shape[-1])
    cmp_s = _compress_paged(page_table.reshape(-1), cache_cmp_kv[l].reshape(-1, HEAD_DIM), w2, bs, n_pages, 8)
    idx, o_cw = _nsa_sample_a(past, shs(q), shs(gt), cmp_s, cache_win_kv[l].reshape(-1, HEAD_DIM), shs(kw))
    idx = idx.reshape(bs, N_KV, ts, LANES)[..., :N_GATHER].reshape(-1)
    gate_sel = gt.reshape(ns_, N_KV, LANES)[:, :, :GQA * 3].reshape(ns_, N_HEADS, 3)[:, :, 1:2]
    gate_sel = jnp.broadcast_to(gate_sel, (ns_, N_HEADS, HEAD_DIM))
    slopes8 = jnp.broadcast_to(slopes.reshape(N_HEADS, 1), (N_HEADS, LANES))
    new_blocks = jnp.pad(ks.reshape(bs, ts * KV_ROWS, HEAD_DIM), ((0, 0), (0, BLOCK_ROWS - ts * KV_ROWS), (0, 0)))
    attn_s = _nsa_sample_b(past, idx, page_table.reshape(-1), q.reshape(ns_, N_HEADS, HEAD_DIM), gate_sel, slopes8,
                           o_cw.reshape(ns_, N_HEADS, HEAD_DIM), new_blocks, cache_sel_kv[l].reshape(-1, HEAD_DIM))
    conv8 = jnp.pad(state_conv[l], ((0, 0), (8 - (CONV_W - 1), 0), (0, 0)))
    rnn_s, h_s = _rglru(shs(xr), shs(xg), conv8, state_h[l].reshape(bs, 1, D_RNN), *lru, tc=ts)
    y_s = tail(xs.reshape(ns_, D_MODEL), attn_s.reshape(ns_, D_ATTN), rnn_s.reshape(ns_, D_RNN), ns_, ns_)
    win_s = jnp.concatenate([cache_win_kv[l], kv6(kw, bs, ts)], axis=1)[:, ts:]
    conv_s = jnp.concatenate([state_conv[l], shs(xr)], axis=1)[:, ts:]
    outs_s = (y_s.reshape(bs, ts, D_MODEL), kv6(kc, bs, ts), kv6(ks, bs, ts),
              win_s, conv_s, h_s.reshape(bs, D_RNN))
    return outs_p, outs_s


def kernel(x_prompt, x_sample, cache_cmp_kv, cache_sel_kv, cache_win_kv, state_conv, state_h, page_table, norm_mix, w_in, cmp_pool_w, conv_w, conv_b, lru_wa, lru_ba, lru_wx, lru_bx, lru_lambda, w_out, norm_ffn, w_router_group, b_router_group, w_router_expert, b_router_expert, w_exp_gate, w_exp_up, w_exp_down, norm_final):
    depth = w_in.shape[0]
    assert depth == 1, "the final norm is fused into the single layer's expert kernel"
    p, s = _layer(0, x_prompt, x_sample, cache_cmp_kv, cache_sel_kv, cache_win_kv, state_conv, state_h, page_table,
                  norm_mix, w_in, cmp_pool_w, conv_w, conv_b, lru_wa, lru_ba, lru_wx, lru_bx, lru_lambda, w_out,
                  norm_ffn, w_router_group, b_router_group, w_router_expert, b_router_expert,
                  w_exp_gate, w_exp_up, w_exp_down, norm_final.reshape(1, D_MODEL))
    st = lambda a: a[None]
    return (p[0], s[0], st(p[1]), st(s[1]), st(p[2]), st(s[2]), st(p[3]), st(s[3]),
            st(p[4]), st(s[4]), st(p[5]), st(s[5]))
```

```python
import functools

import jax
import jax.numpy as jnp
from jax import lax
from jax.experimental import pallas as pl
from jax.experimental.pallas import tpu as pltpu

F32 = jnp.float32
BF16 = jnp.bfloat16

D_MODEL = 2048
D_ATTN = 1024
D_RNN = 1024
N_HEADS = 8
HEAD_DIM = 128
N_KV = 2
GQA = 4
KV_W = 512
CMP_BLOCK = 32
SEL_BLOCK = 64
TOP_N = 16
WINDOW = 512
FORCE_SCORE = 1e4
RNN_BLOCKS = 16
RNN_BLOCK_DIM = 64
CONV_W = 4
LRU_C = 8.0
N_GROUPS = 4
EXP_PER_GROUP = 4
N_EXPERTS = 16
D_EXPERT = 512
RMS_EPS = 1e-6
PAGE_SIZE = 128

LANES = 128
NEG = -1e30
VMEM_LIMIT = 56 * 1024 * 1024

_SEG_Q = (0, 1024)
_SEG_KC = (1024, 1536)
_SEG_KS = (1536, 2048)
_SEG_KW = (2048, 2560)
_SEG_XR = (2560, 3584)
_SEG_XG = (3584, 4608)
_SEG_GT = (4608, 4864)
_PROJ_W = 4864


def _dot(a, b):
    return jnp.dot(a, b, preferred_element_type=F32)


def _dot_nt(a, b):
    return lax.dot_general(a, b, (((1,), (1,)), ((), ())), preferred_element_type=F32)


def _rms(x, g):
    return x * lax.rsqrt(jnp.mean(x * x, axis=-1, keepdims=True) + RMS_EPS) * g


def _split3(x):
    h1 = x.astype(BF16)
    r1 = x - h1.astype(F32)
    h2 = r1.astype(BF16)
    h3 = (r1 - h2.astype(F32)).astype(BF16)
    return h1, h2, h3


def _dot_f32(a, b):
    a1, a2, a3 = _split3(a)
    b1, b2, b3 = _split3(b)
    return (_dot(a3, b1) + _dot(a2, b2) + _dot(a1, b3)) + (_dot(a2, b1) + _dot(a1, b2)) + _dot(a1, b1)


def _dot_sel(a, m01):
    a1, a2, a3 = _split3(a)
    return (_dot(a3, m01) + _dot(a2, m01)) + _dot(a1, m01)


def _proj_kernel(x_ref, g_ref, w_ref, q_ref, kc_ref, ks_ref, kw_ref, xr_ref, xg_ref, gt_ref):
    xn = _rms(x_ref[...], g_ref[...]).astype(BF16)
    outs = ((q_ref, _SEG_Q), (kc_ref, _SEG_KC), (ks_ref, _SEG_KS), (kw_ref, _SEG_KW),
            (xr_ref, _SEG_XR), (xg_ref, _SEG_XG), (gt_ref, _SEG_GT))
    for ref, (lo, hi) in outs:
        ref[...] = _dot(xn, w_ref[:, lo:hi])


def _project(x, g, w, tm):
    n = x.shape[0]
    widths = [hi - lo for lo, hi in (_SEG_Q, _SEG_KC, _SEG_KS, _SEG_KW, _SEG_XR, _SEG_XG, _SEG_GT)]
    return pl.pallas_call(
        _proj_kernel,
        grid=(n // tm,),
        in_specs=[pl.BlockSpec((tm, D_MODEL), lambda i: (i, 0)),
                  pl.BlockSpec((1, D_MODEL), lambda i: (0, 0)),
                  pl.BlockSpec((D_MODEL, _PROJ_W), lambda i: (0, 0), pipeline_mode=pl.Buffered(1))],
        out_specs=[pl.BlockSpec((tm, w_), lambda i: (i, 0)) for w_ in widths],
        out_shape=[jax.ShapeDtypeStruct((n, w_), F32) for w_ in widths],
        compiler_params=pltpu.CompilerParams(dimension_semantics=("arbitrary",), vmem_limit_bytes=VMEM_LIMIT),
        name="proj",
    )(x, g, w)


def _compress_kernel(n_pages, pt_ref, *refs):
    del pt_ref
    pages, w_ref, o_ref = refs[:n_pages], refs[n_pages], refs[n_pages + 1]
    if n_pages == 1:
        x = pages[0][0]
    else:
        x = jnp.concatenate([p[0] for p in pages], axis=0)
    rows = x.shape[0]
    xb = x.reshape(rows // CMP_BLOCK, CMP_BLOCK, KV_W) * w_ref[...][None]
    o_ref[0] = jnp.sum(xb, axis=1)


def _compress(table, src, wfull, n_batch, steps, n_pages, page_rows):
    out_rows = n_pages * page_rows // CMP_BLOCK
    per_b = steps * n_pages

    def page_spec(k):
        return pl.BlockSpec((1, page_rows, KV_W), lambda b, j, pt: (pt[b * per_b + j * n_pages + k], 0, 0))

    return pl.pallas_call(
        functools.partial(_compress_kernel, n_pages),
        grid_spec=pltpu.PrefetchScalarGridSpec(
            num_scalar_prefetch=1,
            grid=(n_batch, steps),
            in_specs=[page_spec(k) for k in range(n_pages)] + [pl.BlockSpec((CMP_BLOCK, KV_W), lambda b, j, pt: (0, 0))],
            out_specs=pl.BlockSpec((1, out_rows, KV_W), lambda b, j, pt: (b, j, 0)),
        ),
        out_shape=jax.ShapeDtypeStruct((n_batch, steps * out_rows, KV_W), F32),
        compiler_params=pltpu.CompilerParams(dimension_semantics=("arbitrary", "arbitrary"), vmem_limit_bytes=VMEM_LIMIT),
        name="compress",
    )(table, *([src] * n_pages), wfull)


def _softplus(x):
    return jnp.maximum(x, 0.0) + jnp.log1p(jnp.exp(-jnp.abs(x)))


def _rglru_kernel(tc, xr_ref, xg_ref, cs_ref, h0_ref, cw_ref, cb_ref, wa_ref, ba_ref, wx_ref, bx_ref, lam_ref,
                  o_ref, hl_ref, tail_scr, h_scr):
    @pl.when(pl.program_id(2) == 0)
    def _():
        tail_scr[...] = cs_ref[0]
        h_scr[...] = h0_ref[0]

    x = xr_ref[0]
    xp = jnp.concatenate([tail_scr[...], x], axis=0)
    w = cw_ref[...]
    xc = cb_ref[...] + pltpu.roll(xp, 3, axis=0)[8:] * w[0:1]
    xc = xc + pltpu.roll(xp, 2, axis=0)[8:] * w[1:2]
    xc = xc + pltpu.roll(xp, 1, axis=0)[8:] * w[2:3]
    xc = xc + x * w[3:4]
    tail_scr[...] = xp[tc:]

    xcb = xc.astype(BF16)
    r = jax.nn.sigmoid(_dot(xcb, wa_ref[0]) + ba_ref[...])
    gi = jax.nn.sigmoid(_dot(xcb, wx_ref[0]) + bx_ref[...])
    log_a = -LRU_C * r * _softplus(-lam_ref[...])
    a = jnp.exp(log_a)
    u = jnp.sqrt(-jnp.tanh(log_a) * (a * a + 1.0)) * (gi * xc)

    row = lax.broadcasted_iota(jnp.int32, a.shape, 0)
    s = 1
    while s < tc:
        keep = row >= s
        a_sh = jnp.where(keep, pltpu.roll(a, s, axis=0), 1.0)
        u_sh = jnp.where(keep, pltpu.roll(u, s, axis=0), 0.0)
        u = a * u_sh + u
        a = a * a_sh
        s *= 2
    h = a * h_scr[...] + u
    h_last = h[tc - 1:tc]
    h_scr[...] = h_last
    hl_ref[0] = h_last
    o_ref[0] = h * jax.nn.gelu(xg_ref[0])


def _rglru(xr, xg, conv_state8, h0, conv_w, conv_b, wa_t, ba, wx_t, bx, lam, tc):
    b, t, _ = xr.shape
    ct = 256
    n_ct = D_RNN // ct
    seq = lambda bi, c, j: (bi, j, c)
    per_b = lambda bi, c, j: (bi, 0, c)
    par = lambda bi, c, j: (0, c)
    return pl.pallas_call(
        functools.partial(_rglru_kernel, tc),
        grid=(b, n_ct, t // tc),
        in_specs=[pl.BlockSpec((1, tc, ct), seq), pl.BlockSpec((1, tc, ct), seq),
                  pl.BlockSpec((1, 8, ct), per_b), pl.BlockSpec((1, 1, ct), per_b),
                  pl.BlockSpec((CONV_W, ct), par), pl.BlockSpec((1, ct), par),
                  pl.BlockSpec((1, ct, ct), lambda bi, c, j: (c, 0, 0)), pl.BlockSpec((1, ct), par),
                  pl.BlockSpec((1, ct, ct), lambda bi, c, j: (c, 0, 0)), pl.BlockSpec((1, ct), par),
                  pl.BlockSpec((1, ct), par)],
        out_specs=[pl.BlockSpec((1, tc, ct), seq), pl.BlockSpec((1, 1, ct), per_b)],
        out_shape=[jax.ShapeDtypeStruct((b, t, D_RNN), F32), jax.ShapeDtypeStruct((b, 1, D_RNN), F32)],
        scratch_shapes=[pltpu.VMEM((8, ct), F32), pltpu.VMEM((1, ct), F32)],
        compiler_params=pltpu.CompilerParams(dimension_semantics=("arbitrary", "arbitrary", "arbitrary"),
                                             vmem_limit_bytes=VMEM_LIMIT),
        name="rglru",
    )(xr, xg, conv_state8, h0, conv_w, conv_b, wa_t, ba, wx_t, bx, lam)


TQ = 128
W4 = GQA * TQ
WIN_KEYS = WINDOW + TQ


def _nsa_prompt_kernel(slopes_ref, q_ref, gt_ref, kc_ref, vc_ref, ks_ref, vs_ref, kw_ref, vw_ref, o_ref,
                       q_scr, kcb, vct, ksb, vst, kwb, vwt, bias0, caus, wlow,
                       impt_scr, selb_scr, m_scr, l_scr, acc_scr, out_scr):
    g = pl.program_id(1)
    i = pl.program_id(2)
    s_len = ks_ref.shape[1]
    n_cmp = kc_ref.shape[1]
    n_sel = n_cmp // 2
    lane = lax.broadcasted_iota(jnp.int32, (1, W4), 1)
    tl_row = (lane % TQ).astype(F32)
    slope_row = jnp.full((1, W4), slopes_ref[g * GQA + GQA - 1], F32)
    for r in reversed(range(GQA - 1)):
        slope_row = jnp.where(lane < (r + 1) * TQ, slopes_ref[g * GQA + r], slope_row)

    @pl.when(i == 0)
    def _prepare():
        kcb[...] = kc_ref[0].astype(BF16)
        vct[...] = vc_ref[0].T.astype(BF16)

        def cast(j, carry):
            off = pl.multiple_of(j * LANES, LANES)
            ksb[pl.ds(off, LANES), :] = ks_ref[0, pl.ds(off, LANES), :].astype(BF16)
            kwb[pl.ds(off, LANES), :] = kw_ref[0, pl.ds(off, LANES), :].astype(BF16)
            vst[:, pl.ds(off, LANES)] = vs_ref[0, pl.ds(off, LANES), :].T.astype(BF16)
            vwt[:, pl.ds(off, LANES)] = vw_ref[0, pl.ds(off, LANES), :].T.astype(BF16)
            return carry

        lax.fori_loop(0, s_len // LANES, cast, 0)
        rel = tl_row - lax.broadcasted_iota(jnp.int32, (WIN_KEYS, W4), 0).astype(F32)
        bias0[...] = slope_row * rel
        caus[...] = jnp.where(rel[0:LANES] >= 0, 0.0, NEG)
        wlow[...] = jnp.where(rel[0:LANES] <= 0, 0.0, NEG)

    start_f = (i * TQ).astype(F32)
    scale = HEAD_DIM ** -0.5
    for r in range(GQA):
        q_scr[r * TQ:(r + 1) * TQ, :] = (q_ref[0, :, r * HEAD_DIM:(r + 1) * HEAD_DIM] * scale).astype(BF16)
    gate_t = jax.nn.sigmoid(gt_ref[0]).T

    def gate_row(branch):
        return jnp.concatenate([gate_t[3 * r + branch:3 * r + branch + 1, :] for r in range(GQA)], axis=1)

    t_row = start_f + tl_row
    c_end = ((lax.broadcasted_iota(jnp.int32, (n_cmp, W4), 0) + 1) * CMP_BLOCK - 1).astype(F32)
    dist_c = t_row - c_end
    ok_c = dist_c >= 0
    x = jnp.where(ok_c, _dot_nt(kcb[...], q_scr[...]) - slope_row * dist_c, NEG)
    e = jnp.where(ok_c, jnp.exp(x - jnp.max(x, axis=0, keepdims=True)), 0.0)
    p = e * (1.0 / jnp.maximum(jnp.sum(e, axis=0, keepdims=True), 1e-30))
    out_scr[...] = gate_row(0) * _dot(vct[...], p.astype(BF16))

    p_heads = p[:, 0:TQ]
    for r in range(1, GQA):
        p_heads = p_heads + p[:, r * TQ:(r + 1) * TQ]
    impt_scr[...] = p_heads
    imp = impt_scr[pl.ds(0, n_sel, stride=2), :] + impt_scr[pl.ds(1, n_sel, stride=2), :]
    blk = lax.broadcasted_iota(jnp.int32, (n_sel, TQ), 0)
    cur = (i * TQ + lax.broadcasted_iota(jnp.int32, (n_sel, TQ), 1)) // SEL_BLOCK
    forced = (blk == 0) | (blk == cur) | (blk == cur - 1)
    imp = jnp.where(forced, FORCE_SCORE, imp)
    imp = jnp.where(blk > cur, NEG, imp)
    rank = jnp.zeros((n_sel, TQ), jnp.int32)
    for j in range(n_sel):
        row = imp[j:j + 1, :]
        tie = jnp.where(blk > j, jnp.where(row == imp, 1, 0), 0)
        rank = rank + jnp.where(row > imp, 1, tie)
    selb = jnp.where(rank < TOP_N, 0.0, NEG)
    for j in range(n_sel):
        selb_scr[j] = selb[j:j + 1, :]

    def scores(k_b, c, nk):
        off = pl.multiple_of(c * LANES, LANES)
        return _dot_nt(k_b[pl.ds(off, nk), :], q_scr[...]) - bias0[0:nk, :]

    def block_mask(c, nk):
        rows = [jnp.broadcast_to(selb_scr[2 * c + k], (SEL_BLOCK, TQ)) for k in range(nk // SEL_BLOCK)]
        return jnp.concatenate([jnp.concatenate(rows, axis=0)] * GQA, axis=1)

    def update(v_t, c, nk, x):
        off = pl.multiple_of(c * LANES, LANES)
        r_c = slope_row * ((c * LANES).astype(F32) - start_f)
        m_old = m_scr[...]
        m_new = jnp.maximum(m_old, jnp.max(x, axis=0, keepdims=True) + r_c)
        pr = jnp.exp(x + (r_c - m_new))
        alpha = jnp.exp(m_old - m_new)
        l_scr[...] = alpha * l_scr[...] + jnp.sum(pr, axis=0, keepdims=True)
        acc_scr[...] = alpha * acc_scr[...] + _dot(v_t[:, pl.ds(off, nk)], pr.astype(BF16))
        m_scr[...] = m_new

    def reset():
        m_scr[...] = jnp.full((1, W4), NEG, F32)
        l_scr[...] = jnp.zeros((1, W4), F32)
        acc_scr[...] = jnp.zeros((HEAD_DIM, W4), F32)

    def result():
        return acc_scr[...] * (1.0 / jnp.maximum(l_scr[...], 1e-30))

    reset()
    n_big = i // 4

    def big(c4, carry):
        update(vst, 4 * c4, 4 * LANES, scores(ksb, 4 * c4, 4 * LANES) + block_mask(4 * c4, 4 * LANES))
        return carry

    lax.fori_loop(0, n_big, big, 0)
    rem = i - 4 * n_big

    @pl.when(rem >= 2)
    def _():
        update(vst, 4 * n_big, 2 * LANES, scores(ksb, 4 * n_big, 2 * LANES) + block_mask(4 * n_big, 2 * LANES))

    @pl.when(rem % 2 == 1)
    def _():
        update(vst, i - 1, LANES, scores(ksb, i - 1, LANES) + block_mask(i - 1, LANES))

    update(vst, i, LANES, scores(ksb, i, LANES) + block_mask(i, LANES) + caus[...])
    out_scr[...] += gate_row(1) * result()

    n_back = WINDOW // LANES

    @pl.when(i >= n_back)
    def _():
        c = i - n_back
        x = scores(kwb, c, WIN_KEYS)
        x = jnp.concatenate([x[0:LANES] + wlow[...], x[LANES:WINDOW], x[WINDOW:WIN_KEYS] + caus[...]], axis=0)
        pr = jnp.exp(x - jnp.max(x, axis=0, keepdims=True))
        o_w = _dot(vwt[:, pl.ds(pl.multiple_of(c * LANES, LANES), WIN_KEYS)], pr.astype(BF16))
        out_scr[...] += gate_row(2) * (o_w * (1.0 / jnp.maximum(jnp.sum(pr, axis=0, keepdims=True), 1e-30)))

    @pl.when(i < n_back)
    def _():
        reset()

        def body(c, carry):
            update(vwt, c, LANES, scores(kwb, c, LANES))
            return carry

        lax.fori_loop(0, i, body, 0)
        update(vwt, i, LANES, scores(kwb, i, LANES) + caus[...])
        out_scr[...] += gate_row(2) * result()

    o_t = out_scr[...]
    for r in range(GQA):
        o_ref[0, :, r * HEAD_DIM:(r + 1) * HEAD_DIM] = o_t[:, r * TQ:(r + 1) * TQ].T


def _nsa_prompt(slopes, q, gt, cmp_kv, ks, kw):
    b, s, _ = q.shape
    n_cmp = cmp_kv.shape[1]
    k_of = lambda bi, g, i: (bi, 0, 2 * g)
    v_of = lambda bi, g, i: (bi, 0, 2 * g + 1)
    tile = pltpu.VMEM((LANES, W4), F32)
    return pl.pallas_call(
        _nsa_prompt_kernel,
        grid=(b, N_KV, s // TQ),
        in_specs=[pl.BlockSpec(memory_space=pltpu.SMEM),
                  pl.BlockSpec((1, TQ, GQA * HEAD_DIM), lambda bi, g, i: (bi, i, g)),
                  pl.BlockSpec((1, TQ, LANES), lambda bi, g, i: (bi, i, g)),
                  pl.BlockSpec((1, n_cmp, HEAD_DIM), k_of), pl.BlockSpec((1, n_cmp, HEAD_DIM), v_of),
                  pl.BlockSpec((1, s, HEAD_DIM), k_of), pl.BlockSpec((1, s, HEAD_DIM), v_of),
                  pl.BlockSpec((1, s, HEAD_DIM), k_of), pl.BlockSpec((1, s, HEAD_DIM), v_of)],
        out_specs=pl.BlockSpec((1, TQ, GQA * HEAD_DIM), lambda bi, g, i: (bi, i, g)),
        out_shape=jax.ShapeDtypeStruct((b, s, D_ATTN), F32),
        scratch_shapes=[pltpu.VMEM((W4, HEAD_DIM), BF16),
                        pltpu.VMEM((n_cmp, HEAD_DIM), BF16), pltpu.VMEM((HEAD_DIM, n_cmp), BF16),
                        pltpu.VMEM((s, HEAD_DIM), BF16), pltpu.VMEM((HEAD_DIM, s), BF16),
                        pltpu.VMEM((s, HEAD_DIM), BF16), pltpu.VMEM((HEAD_DIM, s), BF16),
                        pltpu.VMEM((WIN_KEYS, W4), F32), tile, tile,
                        pltpu.VMEM((n_cmp, TQ), F32),
                        pltpu.VMEM((n_cmp // 2, 1, TQ), F32),
                        pltpu.VMEM((1, W4), F32), pltpu.VMEM((1, W4), F32),
                        pltpu.VMEM((HEAD_DIM, W4), F32), pltpu.VMEM((HEAD_DIM, W4), F32)],
        compiler_params=pltpu.CompilerParams(dimension_semantics=("arbitrary", "arbitrary", "arbitrary"),
                                             vmem_limit_bytes=VMEM_LIMIT),
        name="nsa_prompt",
    )(slopes, q, gt, cmp_kv, cmp_kv, ks, ks, kw, kw)


KV_ROWS = 2 * N_KV


def _compress_paged_kernel(n_pages, pt_ref, *refs):
    del pt_ref
    pages, w_ref, o_ref = refs[:n_pages], refs[n_pages], refs[n_pages + 1]
    for gc in range(KV_ROWS):
        x = jnp.concatenate([p[pl.ds(gc, PAGE_SIZE, stride=KV_ROWS), :] for p in pages], axis=0)
        xb = x.reshape(n_pages * PAGE_SIZE // CMP_BLOCK, CMP_BLOCK, HEAD_DIM) * w_ref[gc % 2][None]
        o_ref[0, gc] = jnp.sum(xb, axis=1)


def _compress_paged(table, pool2d, w2, n_batch, pages_per_b, n_pages):
    steps = pages_per_b // n_pages
    out_rows = n_pages * PAGE_SIZE // CMP_BLOCK
    rows = PAGE_SIZE * KV_ROWS

    def page_spec(k):
        return pl.BlockSpec((rows, HEAD_DIM), lambda b, j, pt: (pt[b * pages_per_b + j * n_pages + k], 0))

    return pl.pallas_call(
        functools.partial(_compress_paged_kernel, n_pages),
        grid_spec=pltpu.PrefetchScalarGridSpec(
            num_scalar_prefetch=1,
            grid=(n_batch, steps),
            in_specs=[page_spec(k) for k in range(n_pages)]
            + [pl.BlockSpec((2, CMP_BLOCK, HEAD_DIM), lambda b, j, pt: (0, 0, 0))],
            out_specs=pl.BlockSpec((1, KV_ROWS, out_rows, HEAD_DIM), lambda b, j, pt: (b, 0, j, 0)),
        ),
        out_shape=jax.ShapeDtypeStruct((n_batch, KV_ROWS, steps * out_rows, HEAD_DIM), F32),
        compiler_params=pltpu.CompilerParams(dimension_semantics=("arbitrary", "arbitrary"), vmem_limit_bytes=VMEM_LIMIT),
        name="compress_paged",
    )(table, *([pool2d] * n_pages), w2)


def _nsa_sample_a_kernel(past, q_ref, gt_ref, cmp_ref, cw_ref, kwn_ref, idx_ref, o_ref):
    t_new = q_ref.shape[1]
    n_cmp = cmp_ref.shape[2]
    n_past_sel = n_cmp // 2
    n_win = cw_ref.shape[0] // KV_ROWS
    rows = GQA * t_new
    gates = jax.nn.sigmoid(gt_ref[0])
    t_row = past + lax.broadcasted_iota(jnp.int32, (rows, 1), 0) % t_new
    scale = HEAD_DIM ** -0.5
    pair = jnp.where(lax.broadcasted_iota(jnp.int32, (n_cmp, n_past_sel), 0) // 2
                     == lax.broadcasted_iota(jnp.int32, (n_cmp, n_past_sel), 1), 1.0, 0.0).astype(BF16)
    imps = []
    for g in range(N_KV):
        qg = jnp.concatenate([q_ref[0, :, (g * GQA + r) * HEAD_DIM:(g * GQA + r + 1) * HEAD_DIM]
                              for r in range(GQA)], axis=0)
        qg = (qg * scale).astype(BF16)
        slope = jnp.concatenate([jnp.full((t_new, 1), 2.0 ** -(g * GQA + r + 1), F32) for r in range(GQA)], axis=0)
        kcol = g * 2 * HEAD_DIM
        kc = cmp_ref[0, 2 * g].astype(BF16)
        vc = cmp_ref[0, 2 * g + 1].astype(BF16)
        c_end = (lax.broadcasted_iota(jnp.int32, (1, n_cmp), 1) + 1) * CMP_BLOCK - 1
        dist_c = t_row - c_end
        mask_c = dist_c >= 0
        s = _dot_nt(qg, kc) - slope * dist_c.astype(F32)
        s = jnp.where(mask_c, s, NEG)
        m = jnp.max(s, axis=-1, keepdims=True)
        p = jnp.where(mask_c, jnp.exp(s - m), 0.0)
        p = p / jnp.maximum(jnp.sum(p, axis=-1, keepdims=True), 1e-30)
        o_c = _dot(p.astype(BF16), vc)
        p_heads = p[0:t_new]
        for r in range(1, GQA):
            p_heads = p_heads + p[r * t_new:(r + 1) * t_new]
        imps.append(_dot_sel(p_heads, pair))
        n_pad = LANES - t_new
        kw = jnp.concatenate([cw_ref[pl.ds(2 * g, n_win, stride=KV_ROWS), :], kwn_ref[0, :, kcol:kcol + HEAD_DIM],
                              jnp.zeros((n_pad, HEAD_DIM), F32)], axis=0).astype(BF16)
        vw = jnp.concatenate([cw_ref[pl.ds(2 * g + 1, n_win, stride=KV_ROWS), :],
                              kwn_ref[0, :, kcol + HEAD_DIM:kcol + 2 * HEAD_DIM],
                              jnp.zeros((n_pad, HEAD_DIM), F32)], axis=0).astype(BF16)
        win_pos = past - n_win + lax.broadcasted_iota(jnp.int32, (1, n_win + LANES), 1)
        dist_w = t_row - win_pos
        mask_w = (dist_w >= 0) & (dist_w <= WINDOW)
        s = _dot_nt(qg, kw) - slope * dist_w.astype(F32)
        s = jnp.where(mask_w, s, NEG)
        m = jnp.max(s, axis=-1, keepdims=True)
        p = jnp.where(mask_w, jnp.exp(s - m), 0.0)
        p = p / jnp.maximum(jnp.sum(p, axis=-1, keepdims=True), 1e-30)
        o_w = _dot(p.astype(BF16), vw)
        for r in range(GQA):
            h = g * GQA + r
            gl = g * LANES + 3 * r
            o_ref[0, :, h * HEAD_DIM:(h + 1) * HEAD_DIM] = (
                gates[:, gl:gl + 1] * o_c[r * t_new:(r + 1) * t_new]
                + gates[:, gl + 2:gl + 3] * o_w[r * t_new:(r + 1) * t_new])

    imp = jnp.concatenate(imps, axis=0)
    n_rows = N_KV * t_new
    lane = lax.broadcasted_iota(jnp.int32, (n_rows, n_past_sel), 1)
    lane_f = lane.astype(F32)
    cur = (past + lax.broadcasted_iota(jnp.int32, (n_rows, n_past_sel), 0) % t_new) // SEL_BLOCK
    forced = (lane == 0) | (lane == cur) | (lane == cur - 1)
    imp = jnp.where(forced, FORCE_SCORE, imp)
    imp = jnp.where(lane > cur, NEG, imp)
    out_lane = lax.broadcasted_iota(jnp.int32, (n_rows, LANES), 1)
    idx = jnp.zeros((n_rows, LANES), F32)
    for k in range(TOP_N - 1):
        m = jnp.max(imp, axis=-1, keepdims=True)
        j = jnp.min(jnp.where(imp == m, lane_f, float(n_past_sel)), axis=-1, keepdims=True)
        idx = jnp.where(out_lane == k, j, idx)
        imp = jnp.where(lane_f == j, -3e38, imp)
    idx_ref[0] = idx.astype(jnp.int32)


def _nsa_sample_a(past, q, gt, cmp_kv, cache_w2d, kw_new):
    b, t_new, _ = q.shape
    win_rows = cache_w2d.shape[0] // b
    per_b3 = lambda bi: (bi, 0, 0)
    return pl.pallas_call(
        functools.partial(_nsa_sample_a_kernel, past),
        grid=(b,),
        in_specs=[pl.BlockSpec((1, t_new, D_ATTN), per_b3), pl.BlockSpec((1, t_new, 2 * LANES), per_b3),
                  pl.BlockSpec((1,) + cmp_kv.shape[1:], lambda bi: (bi, 0, 0, 0)),
                  pl.BlockSpec((win_rows, HEAD_DIM), lambda bi: (bi, 0)),
                  pl.BlockSpec((1, t_new, KV_W), per_b3)],
        out_specs=[pl.BlockSpec((1, N_KV * t_new, LANES), per_b3), pl.BlockSpec((1, t_new, D_ATTN), per_b3)],
        out_shape=[jax.ShapeDtypeStruct((b, N_KV * t_new, LANES), jnp.int32),
                   jax.ShapeDtypeStruct((b, t_new, D_ATTN), F32)],
        compiler_params=pltpu.CompilerParams(dimension_semantics=("arbitrary",), vmem_limit_bytes=VMEM_LIMIT),
        name="nsa_sample_a",
    )(q, gt, cmp_kv, cache_w2d, kw_new)


N_GATHER = TOP_N - 1
BLOCK_ROWS = SEL_BLOCK * KV_ROWS
GATHER_KEYS = TOP_N * SEL_BLOCK


def _nsa_sample_b_kernel(past, t_new, idx_ref, pt_ref, q_ref, gs_ref, slope_ref, ocw_ref, new_ref, pool_ref,
                         o_ref, kv_buf, sems):
    n_steps = pl.num_programs(0) * t_new
    step = pl.program_id(0) * t_new + pl.program_id(1)
    slot = step % 2
    pages_per_b = past // PAGE_SIZE
    sel_per_page = PAGE_SIZE // SEL_BLOCK

    def block_index(st, g, k):
        return idx_ref[((st // t_new * N_KV + g) * t_new + st % t_new) * N_GATHER + k]

    def copies(st, sl):
        out = []
        for g in range(N_KV):
            for k in range(N_GATHER):
                lp = block_index(st, g, k)
                phys = pt_ref[st // t_new * pages_per_b + lp // sel_per_page] * sel_per_page + lp % sel_per_page
                out.append(pltpu.make_async_copy(
                    pool_ref.at[pl.ds(pl.multiple_of(phys * BLOCK_ROWS, BLOCK_ROWS), BLOCK_ROWS), :],
                    kv_buf.at[sl, g, pl.ds(k * BLOCK_ROWS, BLOCK_ROWS), :], sems.at[sl, g, k]))
            out.append(pltpu.make_async_copy(
                new_ref.at[st // t_new], kv_buf.at[sl, g, pl.ds(N_GATHER * BLOCK_ROWS, BLOCK_ROWS), :],
                sems.at[sl, g, N_GATHER]))
        return out

    @pl.when(step == 0)
    def _():
        for cp in copies(step, slot):
            cp.start()

    @pl.when(step + 1 < n_steps)
    def _():
        for cp in copies(step + 1, 1 - slot):
            cp.start()

    t_pos = past + pl.program_id(1)
    q8 = (q_ref[0] * HEAD_DIM ** -0.5).astype(BF16)
    slope = slope_ref[:, 0:1]
    lane = lax.broadcasted_iota(jnp.int32, (1, GATHER_KEYS), 1)
    for cp in copies(step, slot):
        cp.wait()
    outs = []
    for g in range(N_KV):
        pos = past + (lane - N_GATHER * SEL_BLOCK)
        for k in range(N_GATHER):
            pos = jnp.where(lane // SEL_BLOCK == k, block_index(step, g, k) * SEL_BLOCK + lane % SEL_BLOCK, pos)
        kk = kv_buf[slot, g, pl.ds(2 * g, GATHER_KEYS, stride=KV_ROWS), :].astype(BF16)
        vv = kv_buf[slot, g, pl.ds(2 * g + 1, GATHER_KEYS, stride=KV_ROWS), :].astype(BF16)
        dist = t_pos - pos
        mask = dist >= 0
        s = _dot_nt(q8, kk) - slope * dist.astype(F32)
        s = jnp.where(mask, s, NEG)
        m = jnp.max(s, axis=-1, keepdims=True)
        p = jnp.where(mask, jnp.exp(s - m), 0.0)
        p = p / jnp.maximum(jnp.sum(p, axis=-1, keepdims=True), 1e-30)
        outs.append(_dot(p.astype(BF16), vv))
    head = lax.broadcasted_iota(jnp.int32, (N_HEADS, HEAD_DIM), 0)
    o_s = jnp.where(head < GQA, outs[0], outs[1])
    o_ref[0] = ocw_ref[0] + jax.nn.sigmoid(gs_ref[0]) * o_s


def _nsa_sample_b(past, idx, page_table, q, gate_sel, slopes8, o_cw, new_blocks, pool2d):
    n_tok = q.shape[0]
    b = new_blocks.shape[0]
    t_new = n_tok // b
    tok = lambda bi, ti, *_: (bi * t_new + ti, 0, 0)
    return pl.pallas_call(
        functools.partial(_nsa_sample_b_kernel, past, t_new),
        grid_spec=pltpu.PrefetchScalarGridSpec(
            num_scalar_prefetch=2,
            grid=(b, t_new),
            in_specs=[pl.BlockSpec((1, N_HEADS, HEAD_DIM), tok), pl.BlockSpec((1, N_HEADS, HEAD_DIM), tok),
                      pl.BlockSpec((N_HEADS, LANES), lambda bi, ti, *_: (0, 0)),
                      pl.BlockSpec((1, N_HEADS, HEAD_DIM), tok),
                      pl.BlockSpec(memory_space=pl.ANY), pl.BlockSpec(memory_space=pl.ANY)],
            out_specs=pl.BlockSpec((1, N_HEADS, HEAD_DIM), tok),
            scratch_shapes=[pltpu.VMEM((2, N_KV, TOP_N * BLOCK_ROWS, HEAD_DIM), F32),
                            pltpu.SemaphoreType.DMA((2, N_KV, TOP_N))],
        ),
        out_shape=jax.ShapeDtypeStruct((n_tok, N_HEADS, HEAD_DIM), F32),
        compiler_params=pltpu.CompilerParams(dimension_semantics=("arbitrary", "arbitrary"),
                                             vmem_limit_bytes=VMEM_LIMIT),
        name="nsa_sample_b",
    )(idx, page_table, q, gate_sel, slopes8, o_cw, new_blocks, pool2d)


def _finish_kernel(x_ref, a_ref, r_ref, wo_ref, gn_ref, wr_ref, br_ref, h_ref, xn_ref, comb_ref):
    h = (x_ref[...] + _dot(a_ref[...].astype(BF16), wo_ref[0:D_ATTN, :])
         + _dot(r_ref[...].astype(BF16), wo_ref[D_ATTN:D_MODEL, :]))
    h_ref[...] = h
    xn = _rms(h, gn_ref[...])
    xn_ref[...] = xn.astype(BF16)
    logit = _dot_f32(xn, wr_ref[...]) + br_ref[...]
    lane = lax.broadcasted_iota(jnp.int32, logit.shape, 1)
    lane_f = lane.astype(F32)
    is_g = (lane >= N_EXPERTS) & (lane < N_EXPERTS + N_GROUPS)
    gl = jnp.where(is_g, logit, NEG)
    g_max = jnp.max(gl, axis=-1, keepdims=True)
    g_star = jnp.min(jnp.where(gl == g_max, lane_f, 1e9), axis=-1, keepdims=True) - N_EXPERTS
    g_prob = 1.0 / jnp.sum(jnp.where(is_g, jnp.exp(gl - g_max), 0.0), axis=-1, keepdims=True)
    in_grp = (lane < N_EXPERTS) & ((lane // EXP_PER_GROUP).astype(F32) == g_star)
    el = jnp.where(in_grp, logit, NEG)
    e_max = jnp.max(el, axis=-1, keepdims=True)
    ee = jnp.where(in_grp, jnp.exp(el - e_max), 0.0)
    ep = jnp.where(in_grp, ee / jnp.sum(ee, axis=-1, keepdims=True), -1.0)
    p1 = jnp.max(ep, axis=-1, keepdims=True)
    i1 = jnp.min(jnp.where(ep == p1, lane_f, 1e9), axis=-1, keepdims=True)
    ep2 = jnp.where(lane_f == i1, -1.0, ep)
    p2 = jnp.max(ep2, axis=-1, keepdims=True)
    i2 = jnp.min(jnp.where(ep2 == p2, lane_f, 1e9), axis=-1, keepdims=True)
    tot = p1 + p2
    comb_ref[...] = (jnp.where(lane_f == i1, p1 / tot * g_prob, 0.0)
                     + jnp.where(lane_f == i2, p2 / tot * g_prob, 0.0))


def _finish(x, attn_o, rnn_o, w_out, g_ffn, w_router, b_router, tm):
    n = x.shape[0]
    row = lambda i: (i, 0)
    fixed = lambda i: (0, 0)
    return pl.pallas_call(
        _finish_kernel,
        grid=(n // tm,),
        in_specs=[pl.BlockSpec((tm, D_MODEL), row), pl.BlockSpec((tm, D_ATTN), row), pl.BlockSpec((tm, D_RNN), row),
                  pl.BlockSpec((D_MODEL, D_MODEL), fixed, pipeline_mode=pl.Buffered(1)),
                  pl.BlockSpec((1, D_MODEL), fixed),
                  pl.BlockSpec((D_MODEL, LANES), fixed), pl.BlockSpec((1, LANES), fixed)],
        out_specs=[pl.BlockSpec((tm, D_MODEL), row), pl.BlockSpec((tm, D_MODEL), row), pl.BlockSpec((tm, LANES), row)],
        out_shape=[jax.ShapeDtypeStruct((n, D_MODEL), F32), jax.ShapeDtypeStruct((n, D_MODEL), BF16),
                   jax.ShapeDtypeStruct((n, LANES), F32)],
        compiler_params=pltpu.CompilerParams(dimension_semantics=("arbitrary",), vmem_limit_bytes=VMEM_LIMIT),
        name="finish",
    )(x, attn_o, rnn_o, w_out, g_ffn, w_router, b_router)


def _moe_kernel(xn_ref, comb_ref, h_ref, wg_ref, wu_ref, wd_ref, gf_ref, y_ref, acc_scr):
    e = pl.program_id(1)

    @pl.when(e == 0)
    def _():
        acc_scr[...] = h_ref[...]

    xn = xn_ref[...]
    hg = _dot(xn, wg_ref[0])
    hu = _dot(xn, wu_ref[0])
    lane = lax.broadcasted_iota(jnp.int32, comb_ref.shape, 1)
    cw = jnp.sum(jnp.where(lane == e, comb_ref[...], 0.0), axis=-1, keepdims=True)
    hh = jax.nn.silu(hg) * hu * cw
    acc_scr[...] += _dot(hh.astype(BF16), wd_ref[0])

    @pl.when(e == pl.num_programs(1) - 1)
    def _():
        y_ref[...] = _rms(acc_scr[...], gf_ref[...])


def _moe(xn, comb, h, w_gate, w_up, w_down, g_final, tm):
    n = xn.shape[0]
    row = lambda i, e: (i, 0)
    return pl.pallas_call(
        _moe_kernel,
        grid=(n // tm, N_EXPERTS),
        in_specs=[pl.BlockSpec((tm, D_MODEL), row), pl.BlockSpec((tm, LANES), row), pl.BlockSpec((tm, D_MODEL), row),
                  pl.BlockSpec((1, D_MODEL, D_EXPERT), lambda i, e: (e, 0, 0)),
                  pl.BlockSpec((1, D_MODEL, D_EXPERT), lambda i, e: (e, 0, 0)),
                  pl.BlockSpec((1, D_EXPERT, D_MODEL), lambda i, e: (e, 0, 0)),
                  pl.BlockSpec((1, D_MODEL), lambda i, e: (0, 0))],
        out_specs=pl.BlockSpec((tm, D_MODEL), row),
        out_shape=jax.ShapeDtypeStruct((n, D_MODEL), F32),
        scratch_shapes=[pltpu.VMEM((tm, D_MODEL), F32)],
        compiler_params=pltpu.CompilerParams(dimension_semantics=("arbitrary", "arbitrary"),
                                             vmem_limit_bytes=VMEM_LIMIT),
        name="moe",
    )(xn, comb, h, w_gate, w_up, w_down, g_final)


def _block_diag_tiles(w):
    per = 256 // RNN_BLOCK_DIM
    w4 = w.reshape(RNN_BLOCKS // per, per, RNN_BLOCK_DIM, RNN_BLOCK_DIM)
    eye = jnp.eye(per, dtype=w.dtype)
    tiles = jnp.einsum('tpde,pq->tpdqe', w4, eye)
    return tiles.reshape(RNN_BLOCKS // per, 256, 256).astype(BF16)


def _layer(l, xp, xs, cache_cmp_kv, cache_sel_kv, cache_win_kv, state_conv, state_h, page_table,
           norm_mix, w_in, cmp_pool_w, conv_w, conv_b, lru_wa, lru_ba, lru_wx, lru_bx, lru_lambda, w_out,
           norm_ffn, w_router_group, b_router_group, w_router_expert, b_router_expert,
           w_exp_gate, w_exp_up, w_exp_down, final_gain):
    bp, sp, _ = xp.shape
    bs, ts, _ = xs.shape
    n_pages = page_table.shape[1]
    past = n_pages * PAGE_SIZE

    wi = w_in[l]
    gt_cols = wi[:, 2560:2584].reshape(D_MODEL, N_KV, GQA * 3)
    gt_cols = jnp.pad(gt_cols, ((0, 0), (0, 0), (0, LANES - GQA * 3))).reshape(D_MODEL, N_KV * LANES)
    w_proj = jnp.concatenate([wi[:, :2560], wi[:, 2584:], gt_cols], axis=1).astype(BF16)
    g_mix = norm_mix[l].reshape(1, D_MODEL)
    wfull = jnp.tile(jnp.repeat(cmp_pool_w[l], HEAD_DIM, axis=1), (1, N_KV))
    w2 = jnp.broadcast_to(cmp_pool_w[l].T[:, :, None], (2, CMP_BLOCK, HEAD_DIM))
    wa_t = _block_diag_tiles(lru_wa[l])
    wx_t = _block_diag_tiles(lru_wx[l])
    row = lambda v: v.reshape(1, -1)
    slopes = jnp.exp2(-8.0 * jnp.arange(1, N_HEADS + 1, dtype=F32) / N_HEADS)
    w_o = w_out[l].astype(BF16)
    w_router = jnp.pad(jnp.concatenate([w_router_expert[l], w_router_group[l]], axis=1),
                       ((0, 0), (0, LANES - N_EXPERTS - N_GROUPS)))
    b_router = jnp.pad(jnp.concatenate([b_router_expert[l], b_router_group[l]]),
                       (0, LANES - N_EXPERTS - N_GROUPS)).reshape(1, LANES)
    wg, wu, wd = w_exp_gate[l].astype(BF16), w_exp_up[l].astype(BF16), w_exp_down[l].astype(BF16)
    lru = (conv_w[l], row(conv_b[l]), wa_t, row(lru_ba[l]), wx_t, row(lru_bx[l]), row(lru_lambda[l]))

    def tail(x, attn_o, rnn_o, tm_f, tm_m):
        h, xn, comb = _finish(x, attn_o, rnn_o, w_o, row(norm_ffn[l]), w_router, b_router, tm_f)
        return _moe(xn, comb, h, wg, wu, wd, final_gain, tm_m)

    np_ = bp * sp
    q, kc, ks, kw, xr, xg, gt = _project(xp.reshape(np_, D_MODEL), g_mix, w_proj, 256)
    shp = lambda a: a.reshape(bp, sp, a.shape[-1])
    ident = jnp.arange(np_ // 1024, dtype=jnp.int32)
    cmp_p = _compress(ident, kc.reshape(np_ // 1024, 1024, KV_W), wfull, bp, sp // 1024, 1, 1024)
    attn_p = _nsa_prompt(slopes, shp(q), shp(gt), cmp_p, shp(ks), shp(kw))
    rnn_p, h_p = _rglru(shp(xr), shp(xg), jnp.zeros((bp, 8, D_RNN), F32), jnp.zeros((bp, 1, D_RNN), F32), *lru, tc=512)
    y_p = tail(xp.reshape(np_, D_MODEL), attn_p.reshape(np_, D_ATTN), rnn_p.reshape(np_, D_RNN), 256, 512)
    kv6 = lambda a, b_, t_: a.reshape(b_, t_, N_KV, 2, HEAD_DIM)
    outs_p = (y_p.reshape(bp, sp, D_MODEL), kv6(kc, bp, sp), kv6(ks, bp, sp),
              kv6(kw, bp, sp)[:, -min(WINDOW, sp):], shp(xr)[:, sp - (CONV_W - 1):], h_p.reshape(bp, D_RNN))

    ns_ = bs * ts
    q, kc, ks, kw, xr, xg, gt = _project(xs.reshape(ns_, D_MODEL), g_mix, w_proj, ns_)
    shs = lambda a: a.reshape(bs, ts, a.shape[-1])
    cmp_s = _compress_paged(page_table.reshape(-1), cache_cmp_kv[l].reshape(-1, HEAD_DIM), w2, bs, n_pages, 16)
    idx, o_cw = _nsa_sample_a(past, shs(q), shs(gt), cmp_s, cache_win_kv[l].reshape(-1, HEAD_DIM), shs(kw))
    idx = idx.reshape(bs, N_KV, ts, LANES)[..., :N_GATHER].reshape(-1)
    gate_sel = gt.reshape(ns_, N_KV, LANES)[:, :, :GQA * 3].reshape(ns_, N_HEADS, 3)[:, :, 1:2]
    gate_sel = jnp.broadcast_to(gate_sel, (ns_, N_HEADS, HEAD_DIM))
    slopes8 = jnp.broadcast_to(slopes.reshape(N_HEADS, 1), (N_HEADS, LANES))
    new_blocks = jnp.pad(ks.reshape(bs, ts * KV_ROWS, HEAD_DIM), ((0, 0), (0, BLOCK_ROWS - ts * KV_ROWS), (0, 0)))
    attn_s = _nsa_sample_b(past, idx, page_table.reshape(-1), q.reshape(ns_, N_HEADS, HEAD_DIM), gate_sel, slopes8,
                           o_cw.reshape(ns_, N_HEADS, HEAD_DIM), new_blocks, cache_sel_kv[l].reshape(-1, HEAD_DIM))
    conv8 = jnp.pad(state_conv[l], ((0, 0), (8 - (CONV_W - 1), 0), (0, 0)))
    rnn_s, h_s = _rglru(shs(xr), shs(xg), conv8, state_h[l].reshape(bs, 1, D_RNN), *lru, tc=ts)
    y_s = tail(xs.reshape(ns_, D_MODEL), attn_s.reshape(ns_, D_ATTN), rnn_s.reshape(ns_, D_RNN), ns_, ns_)
    win_s = jnp.concatenate([cache_win_kv[l], kv6(kw, bs, ts)], axis=1)[:, ts:]
    conv_s = jnp.concatenate([state_conv[l], shs(xr)], axis=1)[:, ts:]
    outs_s = (y_s.reshape(bs, ts, D_MODEL), kv6(kc, bs, ts), kv6(ks, bs, ts),
              win_s, conv_s, h_s.reshape(bs, D_RNN))
    return outs_p, outs_s


def kernel(x_prompt, x_sample, cache_cmp_kv, cache_sel_kv, cache_win_kv, state_conv, state_h, page_table, norm_mix, w_in, cmp_pool_w, conv_w, conv_b, lru_wa, lru_ba, lru_wx, lru_bx, lru_lambda, w_out, norm_ffn, w_router_group, b_router_group, w_router_expert, b_router_expert, w_exp_gate, w_exp_up, w_exp_down, norm_final):
    depth = w_in.shape[0]
    assert depth == 1, "the final norm is fused into the single layer's expert kernel"
    p, s = _layer(0, x_prompt, x_sample, cache_cmp_kv, cache_sel_kv, cache_win_kv, state_conv, state_h, page_table,
                  norm_mix, w_in, cmp_pool_w, conv_w, conv_b, lru_wa, lru_ba, lru_wx, lru_bx, lru_lambda, w_out,
                  norm_ffn, w_router_group, b_router_group, w_router_expert, b_router_expert,
                  w_exp_gate, w_exp_up, w_exp_down, norm_final.reshape(1, D_MODEL))
    st = lambda a: a[None]
    return (p[0], s[0], st(p[1]), st(s[1]), st(p[2]), st(s[2]), st(p[3]), st(s[3]),
            st(p[4]), st(s[4]), st(p[5]), st(s[5]))
```

```python
import functools

import jax
import jax.numpy as jnp
from jax import lax
from jax.experimental import pallas as pl
from jax.experimental.pallas import tpu as pltpu

F32 = jnp.float32
BF16 = jnp.bfloat16

D_MODEL = 2048
D_ATTN = 1024
D_RNN = 1024
N_HEADS = 8
HEAD_DIM = 128
N_KV = 2
GQA = 4
KV_W = 512
CMP_BLOCK = 32
SEL_BLOCK = 64
TOP_N = 16
WINDOW = 512
FORCE_SCORE = 1e4
RNN_BLOCKS = 16
RNN_BLOCK_DIM = 64
CONV_W = 4
LRU_C = 8.0
N_GROUPS = 4
EXP_PER_GROUP = 4
N_EXPERTS = 16
D_EXPERT = 512
RMS_EPS = 1e-6
PAGE_SIZE = 128

GROUP_LANE = N_EXPERTS
LANES = 128
NEG = -1e30
VMEM_LIMIT = 56 * 1024 * 1024

_SEG_Q = (0, 1024)
_SEG_KC = (1024, 1536)
_SEG_KS = (1536, 2048)
_SEG_KW = (2048, 2560)
_SEG_XR = (2560, 3584)
_SEG_XG = (3584, 4608)
_SEG_GT = (4608, 4864)
_PROJ_W = 4864


def _dot(a, b):
    return jnp.dot(a, b, preferred_element_type=F32)


def _dot_nt(a, b):
    return lax.dot_general(a, b, (((1,), (1,)), ((), ())), preferred_element_type=F32)


def _rms(x, g):
    return x * lax.rsqrt(jnp.mean(x * x, axis=-1, keepdims=True) + RMS_EPS) * g


def _split3(x):
    h1 = x.astype(BF16)
    r1 = x - h1.astype(F32)
    h2 = r1.astype(BF16)
    h3 = (r1 - h2.astype(F32)).astype(BF16)
    return h1, h2, h3


def _dot_f32(a, b, dot=_dot):
    a1, a2, a3 = _split3(a)
    b1, b2, b3 = _split3(b)
    return (dot(a3, b1) + dot(a2, b2) + dot(a1, b3)) + (dot(a2, b1) + dot(a1, b2)) + dot(a1, b1)


def _dot_x3(a, b, dot=_dot):
    a1, a2, _ = _split3(a)
    b1, b2, _ = _split3(b)
    return (dot(a2, b1) + dot(a1, b2)) + dot(a1, b1)


def _mm(a, w):
    if w.dtype == BF16:
        return _dot(a.astype(BF16), w)
    return _dot_f32(a, w)


def _dot_sel(a, m01):
    a1, a2, a3 = _split3(a)
    return (_dot(a3, m01) + _dot(a2, m01)) + _dot(a1, m01)


def _proj_kernel(x_ref, g_ref, w_ref, q_ref, kc_ref, ks_ref, kw_ref, xr_ref, xg_ref, gt_ref):
    xn = _rms(x_ref[...], g_ref[...]).astype(BF16)
    outs = ((q_ref, _SEG_Q), (kc_ref, _SEG_KC), (ks_ref, _SEG_KS), (kw_ref, _SEG_KW),
            (xr_ref, _SEG_XR), (xg_ref, _SEG_XG), (gt_ref, _SEG_GT))
    for ref, (lo, hi) in outs:
        ref[...] = _dot(xn, w_ref[:, lo:hi])


def _project(x, g, w, tm):
    n = x.shape[0]
    widths = [hi - lo for lo, hi in (_SEG_Q, _SEG_KC, _SEG_KS, _SEG_KW, _SEG_XR, _SEG_XG, _SEG_GT)]
    return pl.pallas_call(
        _proj_kernel,
        grid=(n // tm,),
        in_specs=[pl.BlockSpec((tm, D_MODEL), lambda i: (i, 0)),
                  pl.BlockSpec((1, D_MODEL), lambda i: (0, 0)),
                  pl.BlockSpec((D_MODEL, _PROJ_W), lambda i: (0, 0), pipeline_mode=pl.Buffered(1))],
        out_specs=[pl.BlockSpec((tm, w_), lambda i: (i, 0)) for w_ in widths],
        out_shape=[jax.ShapeDtypeStruct((n, w_), F32) for w_ in widths],
        compiler_params=pltpu.CompilerParams(dimension_semantics=("arbitrary",), vmem_limit_bytes=VMEM_LIMIT),
        name="proj",
    )(x, g, w)


def _proj_precise_kernel(x_ref, g_ref, w_ref, z_ref, xn_scr):
    @pl.when(pl.program_id(0) == 0)
    def _():
        xn_scr[...] = _rms(x_ref[...], g_ref[...])

    z_ref[...] = _dot_f32(xn_scr[...], w_ref[...])


def _project_precise(x, g, w_f32):
    n = x.shape[0]
    z = pl.pallas_call(
        _proj_precise_kernel,
        grid=(_PROJ_W // LANES,),
        in_specs=[pl.BlockSpec((n, D_MODEL), lambda j: (0, 0)),
                  pl.BlockSpec((1, D_MODEL), lambda j: (0, 0)),
                  pl.BlockSpec((D_MODEL, LANES), lambda j: (0, j))],
        out_specs=pl.BlockSpec((n, LANES), lambda j: (0, j)),
        out_shape=jax.ShapeDtypeStruct((n, _PROJ_W), F32),
        scratch_shapes=[pltpu.VMEM((n, D_MODEL), F32)],
        compiler_params=pltpu.CompilerParams(dimension_semantics=("arbitrary",), vmem_limit_bytes=VMEM_LIMIT),
        name="proj_precise",
    )(x, g, w_f32)
    return [z[:, lo:hi] for lo, hi in (_SEG_Q, _SEG_KC, _SEG_KS, _SEG_KW, _SEG_XR, _SEG_XG, _SEG_GT)]


def _compress_kernel(n_pages, pt_ref, *refs):
    del pt_ref
    pages, w_ref, o_ref = refs[:n_pages], refs[n_pages], refs[n_pages + 1]
    if n_pages == 1:
        x = pages[0][0]
    else:
        x = jnp.concatenate([p[0] for p in pages], axis=0)
    rows = x.shape[0]
    xb = x.reshape(rows // CMP_BLOCK, CMP_BLOCK, KV_W) * w_ref[...][None]
    o_ref[0] = jnp.sum(xb, axis=1)


def _compress(table, src, wfull, n_batch, steps, n_pages, page_rows):
    out_rows = n_pages * page_rows // CMP_BLOCK
    per_b = steps * n_pages

    def page_spec(k):
        return pl.BlockSpec((1, page_rows, KV_W), lambda b, j, pt: (pt[b * per_b + j * n_pages + k], 0, 0))

    return pl.pallas_call(
        functools.partial(_compress_kernel, n_pages),
        grid_spec=pltpu.PrefetchScalarGridSpec(
            num_scalar_prefetch=1,
            grid=(n_batch, steps),
            in_specs=[page_spec(k) for k in range(n_pages)] + [pl.BlockSpec((CMP_BLOCK, KV_W), lambda b, j, pt: (0, 0))],
            out_specs=pl.BlockSpec((1, out_rows, KV_W), lambda b, j, pt: (b, j, 0)),
        ),
        out_shape=jax.ShapeDtypeStruct((n_batch, steps * out_rows, KV_W), F32),
        compiler_params=pltpu.CompilerParams(dimension_semantics=("arbitrary", "arbitrary"), vmem_limit_bytes=VMEM_LIMIT),
        name="compress",
    )(table, *([src] * n_pages), wfull)


def _softplus(x):
    return jnp.maximum(x, 0.0) + jnp.log1p(jnp.exp(-jnp.abs(x)))


def _rglru_kernel(tc, xr_ref, xg_ref, cs_ref, h0_ref, cw_ref, cb_ref, wa_ref, ba_ref, wx_ref, bx_ref, lam_ref,
                  o_ref, hl_ref, tail_scr, h_scr):
    @pl.when(pl.program_id(2) == 0)
    def _():
        tail_scr[...] = cs_ref[0]
        h_scr[...] = h0_ref[0]

    x = xr_ref[0]
    xp = jnp.concatenate([tail_scr[...], x], axis=0)
    w = cw_ref[...]
    xc = cb_ref[...] + pltpu.roll(xp, 3, axis=0)[8:] * w[0:1]
    xc = xc + pltpu.roll(xp, 2, axis=0)[8:] * w[1:2]
    xc = xc + pltpu.roll(xp, 1, axis=0)[8:] * w[2:3]
    xc = xc + x * w[3:4]
    tail_scr[...] = xp[tc:]

    r = jax.nn.sigmoid(_mm(xc, wa_ref[0]) + ba_ref[...])
    gi = jax.nn.sigmoid(_mm(xc, wx_ref[0]) + bx_ref[...])
    log_a = -LRU_C * r * _softplus(-lam_ref[...])
    a = jnp.exp(log_a)
    u = jnp.sqrt(-jnp.tanh(log_a) * (a * a + 1.0)) * (gi * xc)

    row = lax.broadcasted_iota(jnp.int32, a.shape, 0)
    s = 1
    while s < tc:
        keep = row >= s
        a_sh = jnp.where(keep, pltpu.roll(a, s, axis=0), 1.0)
        u_sh = jnp.where(keep, pltpu.roll(u, s, axis=0), 0.0)
        u = a * u_sh + u
        a = a * a_sh
        s *= 2
    h = a * h_scr[...] + u
    h_last = h[tc - 1:tc]
    h_scr[...] = h_last
    hl_ref[0] = h_last
    o_ref[0] = h * jax.nn.gelu(xg_ref[0])


def _rglru(xr, xg, conv_state8, h0, conv_w, conv_b, wa_t, ba, wx_t, bx, lam, tc):
    b, t, _ = xr.shape
    ct = 256
    n_ct = D_RNN // ct
    seq = lambda bi, c, j: (bi, j, c)
    per_b = lambda bi, c, j: (bi, 0, c)
    par = lambda bi, c, j: (0, c)
    return pl.pallas_call(
        functools.partial(_rglru_kernel, tc),
        grid=(b, n_ct, t // tc),
        in_specs=[pl.BlockSpec((1, tc, ct), seq), pl.BlockSpec((1, tc, ct), seq),
                  pl.BlockSpec((1, 8, ct), per_b), pl.BlockSpec((1, 1, ct), per_b),
                  pl.BlockSpec((CONV_W, ct), par), pl.BlockSpec((1, ct), par),
                  pl.BlockSpec((1, ct, ct), lambda bi, c, j: (c, 0, 0)), pl.BlockSpec((1, ct), par),
                  pl.BlockSpec((1, ct, ct), lambda bi, c, j: (c, 0, 0)), pl.BlockSpec((1, ct), par),
                  pl.BlockSpec((1, ct), par)],
        out_specs=[pl.BlockSpec((1, tc, ct), seq), pl.BlockSpec((1, 1, ct), per_b)],
        out_shape=[jax.ShapeDtypeStruct((b, t, D_RNN), F32), jax.ShapeDtypeStruct((b, 1, D_RNN), F32)],
        scratch_shapes=[pltpu.VMEM((8, ct), F32), pltpu.VMEM((1, ct), F32)],
        compiler_params=pltpu.CompilerParams(dimension_semantics=("arbitrary", "arbitrary", "arbitrary"),
                                             vmem_limit_bytes=VMEM_LIMIT),
        name="rglru",
    )(xr, xg, conv_state8, h0, conv_w, conv_b, wa_t, ba, wx_t, bx, lam)


TQ = 128
W4 = GQA * TQ
WIN_KEYS = WINDOW + TQ


def _nsa_prompt_kernel(slopes_ref, q_ref, gt_ref, kc_ref, vc_ref, ks_ref, vs_ref, kw_ref, vw_ref, o_ref,
                       q_scr, kcb, vct, ksb, vst, kwb, vwt, bias0, caus, wlow,
                       impt_scr, selb_scr, m_scr, l_scr, acc_scr, out_scr):
    g = pl.program_id(1)
    i = pl.program_id(2)
    s_len = ks_ref.shape[1]
    n_cmp = kc_ref.shape[1]
    n_sel = n_cmp // 2
    lane = lax.broadcasted_iota(jnp.int32, (1, W4), 1)
    tl_row = (lane % TQ).astype(F32)
    slope_row = jnp.full((1, W4), slopes_ref[g * GQA + GQA - 1], F32)
    for r in reversed(range(GQA - 1)):
        slope_row = jnp.where(lane < (r + 1) * TQ, slopes_ref[g * GQA + r], slope_row)

    @pl.when(i == 0)
    def _prepare():
        kcb[...] = kc_ref[0].astype(BF16)
        vct[...] = vc_ref[0].T.astype(BF16)

        def cast(j, carry):
            off = pl.multiple_of(j * LANES, LANES)
            ksb[pl.ds(off, LANES), :] = ks_ref[0, pl.ds(off, LANES), :].astype(BF16)
            kwb[pl.ds(off, LANES), :] = kw_ref[0, pl.ds(off, LANES), :].astype(BF16)
            vst[:, pl.ds(off, LANES)] = vs_ref[0, pl.ds(off, LANES), :].T.astype(BF16)
            vwt[:, pl.ds(off, LANES)] = vw_ref[0, pl.ds(off, LANES), :].T.astype(BF16)
            return carry

        lax.fori_loop(0, s_len // LANES, cast, 0)
        rel = tl_row - lax.broadcasted_iota(jnp.int32, (WIN_KEYS, W4), 0).astype(F32)
        bias0[...] = slope_row * rel
        caus[...] = jnp.where(rel[0:LANES] >= 0, 0.0, NEG)
        wlow[...] = jnp.where(rel[0:LANES] <= 0, 0.0, NEG)

    start_f = (i * TQ).astype(F32)
    scale = HEAD_DIM ** -0.5
    for r in range(GQA):
        q_scr[r * TQ:(r + 1) * TQ, :] = (q_ref[0, :, r * HEAD_DIM:(r + 1) * HEAD_DIM] * scale).astype(BF16)
    gate_t = jax.nn.sigmoid(gt_ref[0]).T

    def gate_row(branch):
        return jnp.concatenate([gate_t[3 * r + branch:3 * r + branch + 1, :] for r in range(GQA)], axis=1)

    t_row = start_f + tl_row
    c_end = ((lax.broadcasted_iota(jnp.int32, (n_cmp, W4), 0) + 1) * CMP_BLOCK - 1).astype(F32)
    dist_c = t_row - c_end
    ok_c = dist_c >= 0
    x = jnp.where(ok_c, _dot_nt(kcb[...], q_scr[...]) - slope_row * dist_c, NEG)
    e = jnp.where(ok_c, jnp.exp(x - jnp.max(x, axis=0, keepdims=True)), 0.0)
    p = e * (1.0 / jnp.maximum(jnp.sum(e, axis=0, keepdims=True), 1e-30))
    out_scr[...] = gate_row(0) * _dot(vct[...], p.astype(BF16))

    p_heads = p[:, 0:TQ]
    for r in range(1, GQA):
        p_heads = p_heads + p[:, r * TQ:(r + 1) * TQ]
    impt_scr[...] = p_heads
    imp = impt_scr[pl.ds(0, n_sel, stride=2), :] + impt_scr[pl.ds(1, n_sel, stride=2), :]
    blk = lax.broadcasted_iota(jnp.int32, (n_sel, TQ), 0)
    cur = (i * TQ + lax.broadcasted_iota(jnp.int32, (n_sel, TQ), 1)) // SEL_BLOCK
    forced = (blk == 0) | (blk == cur) | (blk == cur - 1)
    imp = jnp.where(forced, FORCE_SCORE, imp)
    imp = jnp.where(blk > cur, NEG, imp)
    rank = jnp.zeros((n_sel, TQ), jnp.int32)
    for j in range(n_sel):
        row = imp[j:j + 1, :]
        tie = jnp.where(blk > j, jnp.where(row == imp, 1, 0), 0)
        rank = rank + jnp.where(row > imp, 1, tie)
    selb = jnp.where(rank < TOP_N, 0.0, NEG)
    for j in range(n_sel):
        selb_scr[j] = selb[j:j + 1, :]

    def scores(k_b, c, nk):
        off = pl.multiple_of(c * LANES, LANES)
        return _dot_nt(k_b[pl.ds(off, nk), :], q_scr[...]) - bias0[0:nk, :]

    def block_mask(c, nk):
        rows = [jnp.broadcast_to(selb_scr[2 * c + k], (SEL_BLOCK, TQ)) for k in range(nk // SEL_BLOCK)]
        return jnp.concatenate([jnp.concatenate(rows, axis=0)] * GQA, axis=1)

    def update(v_t, c, nk, x):
        off = pl.multiple_of(c * LANES, LANES)
        r_c = slope_row * ((c * LANES).astype(F32) - start_f)
        m_old = m_scr[...]
        m_new = jnp.maximum(m_old, jnp.max(x, axis=0, keepdims=True) + r_c)
        pr = jnp.exp(x + (r_c - m_new))
        alpha = jnp.exp(m_old - m_new)
        l_scr[...] = alpha * l_scr[...] + jnp.sum(pr, axis=0, keepdims=True)
        acc_scr[...] = alpha * acc_scr[...] + _dot(v_t[:, pl.ds(off, nk)], pr.astype(BF16))
        m_scr[...] = m_new

    def reset():
        m_scr[...] = jnp.full((1, W4), NEG, F32)
        l_scr[...] = jnp.zeros((1, W4), F32)
        acc_scr[...] = jnp.zeros((HEAD_DIM, W4), F32)

    def result():
        return acc_scr[...] * (1.0 / jnp.maximum(l_scr[...], 1e-30))

    reset()
    n_big = i // 4

    def big(c4, carry):
        update(vst, 4 * c4, 4 * LANES, scores(ksb, 4 * c4, 4 * LANES) + block_mask(4 * c4, 4 * LANES))
        return carry

    lax.fori_loop(0, n_big, big, 0)
    rem = i - 4 * n_big

    @pl.when(rem >= 2)
    def _():
        update(vst, 4 * n_big, 2 * LANES, scores(ksb, 4 * n_big, 2 * LANES) + block_mask(4 * n_big, 2 * LANES))

    @pl.when(rem % 2 == 1)
    def _():
        update(vst, i - 1, LANES, scores(ksb, i - 1, LANES) + block_mask(i - 1, LANES))

    update(vst, i, LANES, scores(ksb, i, LANES) + block_mask(i, LANES) + caus[...])
    out_scr[...] += gate_row(1) * result()

    n_back = WINDOW // LANES

    @pl.when(i >= n_back)
    def _():
        c = i - n_back
        x = scores(kwb, c, WIN_KEYS)
        x = jnp.concatenate([x[0:LANES] + wlow[...], x[LANES:WINDOW], x[WINDOW:WIN_KEYS] + caus[...]], axis=0)
        pr = jnp.exp(x - jnp.max(x, axis=0, keepdims=True))
        o_w = _dot(vwt[:, pl.ds(pl.multiple_of(c * LANES, LANES), WIN_KEYS)], pr.astype(BF16))
        out_scr[...] += gate_row(2) * (o_w * (1.0 / jnp.maximum(jnp.sum(pr, axis=0, keepdims=True), 1e-30)))

    @pl.when(i < n_back)
    def _():
        reset()

        def body(c, carry):
            update(vwt, c, LANES, scores(kwb, c, LANES))
            return carry

        lax.fori_loop(0, i, body, 0)
        update(vwt, i, LANES, scores(kwb, i, LANES) + caus[...])
        out_scr[...] += gate_row(2) * result()

    o_t = out_scr[...]
    for r in range(GQA):
        o_ref[0, :, r * HEAD_DIM:(r + 1) * HEAD_DIM] = o_t[:, r * TQ:(r + 1) * TQ].T


def _nsa_prompt(slopes, q, gt, cmp_kv, ks, kw):
    b, s, _ = q.shape
    n_cmp = cmp_kv.shape[1]
    k_of = lambda bi, g, i: (bi, 0, 2 * g)
    v_of = lambda bi, g, i: (bi, 0, 2 * g + 1)
    tile = pltpu.VMEM((LANES, W4), F32)
    return pl.pallas_call(
        _nsa_prompt_kernel,
        grid=(b, N_KV, s // TQ),
        in_specs=[pl.BlockSpec(memory_space=pltpu.SMEM),
                  pl.BlockSpec((1, TQ, GQA * HEAD_DIM), lambda bi, g, i: (bi, i, g)),
                  pl.BlockSpec((1, TQ, LANES), lambda bi, g, i: (bi, i, g)),
                  pl.BlockSpec((1, n_cmp, HEAD_DIM), k_of), pl.BlockSpec((1, n_cmp, HEAD_DIM), v_of),
                  pl.BlockSpec((1, s, HEAD_DIM), k_of), pl.BlockSpec((1, s, HEAD_DIM), v_of),
                  pl.BlockSpec((1, s, HEAD_DIM), k_of), pl.BlockSpec((1, s, HEAD_DIM), v_of)],
        out_specs=pl.BlockSpec((1, TQ, GQA * HEAD_DIM), lambda bi, g, i: (bi, i, g)),
        out_shape=jax.ShapeDtypeStruct((b, s, D_ATTN), F32),
        scratch_shapes=[pltpu.VMEM((W4, HEAD_DIM), BF16),
                        pltpu.VMEM((n_cmp, HEAD_DIM), BF16), pltpu.VMEM((HEAD_DIM, n_cmp), BF16),
                        pltpu.VMEM((s, HEAD_DIM), BF16), pltpu.VMEM((HEAD_DIM, s), BF16),
                        pltpu.VMEM((s, HEAD_DIM), BF16), pltpu.VMEM((HEAD_DIM, s), BF16),
                        pltpu.VMEM((WIN_KEYS, W4), F32), tile, tile,
                        pltpu.VMEM((n_cmp, TQ), F32),
                        pltpu.VMEM((n_cmp // 2, 1, TQ), F32),
                        pltpu.VMEM((1, W4), F32), pltpu.VMEM((1, W4), F32),
                        pltpu.VMEM((HEAD_DIM, W4), F32), pltpu.VMEM((HEAD_DIM, W4), F32)],
        compiler_params=pltpu.CompilerParams(dimension_semantics=("arbitrary", "arbitrary", "arbitrary"),
                                             vmem_limit_bytes=VMEM_LIMIT),
        name="nsa_prompt",
    )(slopes, q, gt, cmp_kv, cmp_kv, ks, ks, kw, kw)


KV_ROWS = 2 * N_KV


def _compress_paged_kernel(n_pages, pt_ref, *refs):
    del pt_ref
    pages, w_ref, o_ref = refs[:n_pages], refs[n_pages], refs[n_pages + 1]
    for gc in range(KV_ROWS):
        x = jnp.concatenate([p[pl.ds(gc, PAGE_SIZE, stride=KV_ROWS), :] for p in pages], axis=0)
        xb = x.reshape(n_pages * PAGE_SIZE // CMP_BLOCK, CMP_BLOCK, HEAD_DIM) * w_ref[gc % 2][None]
        o_ref[0, gc] = jnp.sum(xb, axis=1)


def _compress_paged(table, pool2d, w2, n_batch, pages_per_b, n_pages):
    steps = pages_per_b // n_pages
    out_rows = n_pages * PAGE_SIZE // CMP_BLOCK
    rows = PAGE_SIZE * KV_ROWS

    def page_spec(k):
        return pl.BlockSpec((rows, HEAD_DIM), lambda b, j, pt: (pt[b * pages_per_b + j * n_pages + k], 0))

    return pl.pallas_call(
        functools.partial(_compress_paged_kernel, n_pages),
        grid_spec=pltpu.PrefetchScalarGridSpec(
            num_scalar_prefetch=1,
            grid=(n_batch, steps),
            in_specs=[page_spec(k) for k in range(n_pages)]
            + [pl.BlockSpec((2, CMP_BLOCK, HEAD_DIM), lambda b, j, pt: (0, 0, 0))],
            out_specs=pl.BlockSpec((1, KV_ROWS, out_rows, HEAD_DIM), lambda b, j, pt: (b, 0, j, 0)),
        ),
        out_shape=jax.ShapeDtypeStruct((n_batch, KV_ROWS, steps * out_rows, HEAD_DIM), F32),
        compiler_params=pltpu.CompilerParams(dimension_semantics=("arbitrary", "arbitrary"), vmem_limit_bytes=VMEM_LIMIT),
        name="compress_paged",
    )(table, *([pool2d] * n_pages), w2)


def _nsa_sample_a_kernel(past, q_ref, gt_ref, cmp_ref, cw_ref, kwn_ref, idx_ref, o_ref):
    t_new = q_ref.shape[1]
    n_cmp = cmp_ref.shape[2]
    n_past_sel = n_cmp // 2
    n_win = cw_ref.shape[0] // KV_ROWS
    rows = GQA * t_new
    gates = jax.nn.sigmoid(gt_ref[0])
    t_row = past + lax.broadcasted_iota(jnp.int32, (rows, 1), 0) % t_new
    scale = HEAD_DIM ** -0.5
    pair = jnp.where(lax.broadcasted_iota(jnp.int32, (n_cmp, n_past_sel), 0) // 2
                     == lax.broadcasted_iota(jnp.int32, (n_cmp, n_past_sel), 1), 1.0, 0.0).astype(BF16)
    imps = []
    for g in range(N_KV):
        qg = jnp.concatenate([q_ref[0, :, (g * GQA + r) * HEAD_DIM:(g * GQA + r + 1) * HEAD_DIM]
                              for r in range(GQA)], axis=0)
        qg = qg * scale
        slope = jnp.concatenate([jnp.full((t_new, 1), 2.0 ** -(g * GQA + r + 1), F32) for r in range(GQA)], axis=0)
        kcol = g * 2 * HEAD_DIM
        kc = cmp_ref[0, 2 * g]
        vc = cmp_ref[0, 2 * g + 1]
        c_end = (lax.broadcasted_iota(jnp.int32, (1, n_cmp), 1) + 1) * CMP_BLOCK - 1
        dist_c = t_row - c_end
        mask_c = dist_c >= 0
        s = _dot_f32(qg, kc, _dot_nt) - slope * dist_c.astype(F32)
        s = jnp.where(mask_c, s, NEG)
        m = jnp.max(s, axis=-1, keepdims=True)
        p = jnp.where(mask_c, jnp.exp(s - m), 0.0)
        p = p / jnp.maximum(jnp.sum(p, axis=-1, keepdims=True), 1e-30)
        o_c = _dot_f32(p, vc)
        p_heads = p[0:t_new]
        for r in range(1, GQA):
            p_heads = p_heads + p[r * t_new:(r + 1) * t_new]
        imps.append(_dot_sel(p_heads, pair))
        n_pad = LANES - t_new
        kw = jnp.concatenate([cw_ref[pl.ds(2 * g, n_win, stride=KV_ROWS), :], kwn_ref[0, :, kcol:kcol + HEAD_DIM],
                              jnp.zeros((n_pad, HEAD_DIM), F32)], axis=0)
        vw = jnp.concatenate([cw_ref[pl.ds(2 * g + 1, n_win, stride=KV_ROWS), :],
                              kwn_ref[0, :, kcol + HEAD_DIM:kcol + 2 * HEAD_DIM],
                              jnp.zeros((n_pad, HEAD_DIM), F32)], axis=0)
        win_pos = past - n_win + lax.broadcasted_iota(jnp.int32, (1, n_win + LANES), 1)
        dist_w = t_row - win_pos
        mask_w = (dist_w >= 0) & (dist_w <= WINDOW)
        s = _dot_f32(qg, kw, _dot_nt) - slope * dist_w.astype(F32)
        s = jnp.where(mask_w, s, NEG)
        m = jnp.max(s, axis=-1, keepdims=True)
        p = jnp.where(mask_w, jnp.exp(s - m), 0.0)
        p = p / jnp.maximum(jnp.sum(p, axis=-1, keepdims=True), 1e-30)
        o_w = _dot_f32(p, vw)
        for r in range(GQA):
            h = g * GQA + r
            gl = g * LANES + 3 * r
            o_ref[0, :, h * HEAD_DIM:(h + 1) * HEAD_DIM] = (
                gates[:, gl:gl + 1] * o_c[r * t_new:(r + 1) * t_new]
                + gates[:, gl + 2:gl + 3] * o_w[r * t_new:(r + 1) * t_new])

    imp = jnp.concatenate(imps, axis=0)
    n_rows = N_KV * t_new
    lane = lax.broadcasted_iota(jnp.int32, (n_rows, n_past_sel), 1)
    lane_f = lane.astype(F32)
    cur = (past + lax.broadcasted_iota(jnp.int32, (n_rows, n_past_sel), 0) % t_new) // SEL_BLOCK
    forced = (lane == 0) | (lane == cur) | (lane == cur - 1)
    imp = jnp.where(forced, FORCE_SCORE, imp)
    imp = jnp.where(lane > cur, NEG, imp)
    out_lane = lax.broadcasted_iota(jnp.int32, (n_rows, LANES), 1)
    idx = jnp.zeros((n_rows, LANES), F32)
    for k in range(TOP_N - 1):
        m = jnp.max(imp, axis=-1, keepdims=True)
        j = jnp.min(jnp.where(imp == m, lane_f, float(n_past_sel)), axis=-1, keepdims=True)
        idx = jnp.where(out_lane == k, j, idx)
        imp = jnp.where(lane_f == j, -3e38, imp)
    idx_ref[0] = idx.astype(jnp.int32)


def _nsa_sample_a(past, q, gt, cmp_kv, cache_w2d, kw_new):
    b, t_new, _ = q.shape
    win_rows = cache_w2d.shape[0] // b
    per_b3 = lambda bi: (bi, 0, 0)
    return pl.pallas_call(
        functools.partial(_nsa_sample_a_kernel, past),
        grid=(b,),
        in_specs=[pl.BlockSpec((1, t_new, D_ATTN), per_b3), pl.BlockSpec((1, t_new, 2 * LANES), per_b3),
                  pl.BlockSpec((1,) + cmp_kv.shape[1:], lambda bi: (bi, 0, 0, 0)),
                  pl.BlockSpec((win_rows, HEAD_DIM), lambda bi: (bi, 0)),
                  pl.BlockSpec((1, t_new, KV_W), per_b3)],
        out_specs=[pl.BlockSpec((1, N_KV * t_new, LANES), per_b3), pl.BlockSpec((1, t_new, D_ATTN), per_b3)],
        out_shape=[jax.ShapeDtypeStruct((b, N_KV * t_new, LANES), jnp.int32),
                   jax.ShapeDtypeStruct((b, t_new, D_ATTN), F32)],
        compiler_params=pltpu.CompilerParams(dimension_semantics=("arbitrary",), vmem_limit_bytes=VMEM_LIMIT),
        name="nsa_sample_a",
    )(q, gt, cmp_kv, cache_w2d, kw_new)


N_GATHER = TOP_N - 1
BLOCK_ROWS = SEL_BLOCK * KV_ROWS
GATHER_KEYS = TOP_N * SEL_BLOCK


def _nsa_sample_b_kernel(past, t_new, idx_ref, pt_ref, q_ref, gs_ref, slope_ref, ocw_ref, new_ref, pool_ref,
                         o_ref, kv_buf, sems):
    n_steps = pl.num_programs(0) * t_new
    step = pl.program_id(0) * t_new + pl.program_id(1)
    slot = step % 2
    pages_per_b = past // PAGE_SIZE
    sel_per_page = PAGE_SIZE // SEL_BLOCK

    def block_index(st, g, k):
        return idx_ref[((st // t_new * N_KV + g) * t_new + st % t_new) * N_GATHER + k]

    def copies(st, sl):
        out = []
        for g in range(N_KV):
            for k in range(N_GATHER):
                lp = block_index(st, g, k)
                phys = pt_ref[st // t_new * pages_per_b + lp // sel_per_page] * sel_per_page + lp % sel_per_page
                out.append(pltpu.make_async_copy(
                    pool_ref.at[pl.ds(pl.multiple_of(phys * BLOCK_ROWS, BLOCK_ROWS), BLOCK_ROWS), :],
                    kv_buf.at[sl, g, pl.ds(k * BLOCK_ROWS, BLOCK_ROWS), :], sems.at[sl, g, k]))
            out.append(pltpu.make_async_copy(
                new_ref.at[st // t_new], kv_buf.at[sl, g, pl.ds(N_GATHER * BLOCK_ROWS, BLOCK_ROWS), :],
                sems.at[sl, g, N_GATHER]))
        return out

    @pl.when(step == 0)
    def _():
        for cp in copies(step, slot):
            cp.start()

    @pl.when(step + 1 < n_steps)
    def _():
        for cp in copies(step + 1, 1 - slot):
            cp.start()

    t_pos = past + pl.program_id(1)
    q8 = q_ref[0] * HEAD_DIM ** -0.5
    slope = slope_ref[:, 0:1]
    lane = lax.broadcasted_iota(jnp.int32, (1, GATHER_KEYS), 1)
    for cp in copies(step, slot):
        cp.wait()
    outs = []
    for g in range(N_KV):
        pos = past + (lane - N_GATHER * SEL_BLOCK)
        for k in range(N_GATHER):
            pos = jnp.where(lane // SEL_BLOCK == k, block_index(step, g, k) * SEL_BLOCK + lane % SEL_BLOCK, pos)
        kk = kv_buf[slot, g, pl.ds(2 * g, GATHER_KEYS, stride=KV_ROWS), :]
        vv = kv_buf[slot, g, pl.ds(2 * g + 1, GATHER_KEYS, stride=KV_ROWS), :]
        dist = t_pos - pos
        mask = dist >= 0
        s = _dot_x3(q8, kk, _dot_nt) - slope * dist.astype(F32)
        s = jnp.where(mask, s, NEG)
        m = jnp.max(s, axis=-1, keepdims=True)
        p = jnp.where(mask, jnp.exp(s - m), 0.0)
        p = p / jnp.maximum(jnp.sum(p, axis=-1, keepdims=True), 1e-30)
        outs.append(_dot_x3(p, vv))
    head = lax.broadcasted_iota(jnp.int32, (N_HEADS, HEAD_DIM), 0)
    o_s = jnp.where(head < GQA, outs[0], outs[1])
    o_ref[0] = ocw_ref[0] + jax.nn.sigmoid(gs_ref[0]) * o_s


def _nsa_sample_b(past, idx, page_table, q, gate_sel, slopes8, o_cw, new_blocks, pool2d):
    n_tok = q.shape[0]
    b = new_blocks.shape[0]
    t_new = n_tok // b
    tok = lambda bi, ti, *_: (bi * t_new + ti, 0, 0)
    return pl.pallas_call(
        functools.partial(_nsa_sample_b_kernel, past, t_new),
        grid_spec=pltpu.PrefetchScalarGridSpec(
            num_scalar_prefetch=2,
            grid=(b, t_new),
            in_specs=[pl.BlockSpec((1, N_HEADS, HEAD_DIM), tok), pl.BlockSpec((1, N_HEADS, HEAD_DIM), tok),
                      pl.BlockSpec((N_HEADS, LANES), lambda bi, ti, *_: (0, 0)),
                      pl.BlockSpec((1, N_HEADS, HEAD_DIM), tok),
                      pl.BlockSpec(memory_space=pl.ANY), pl.BlockSpec(memory_space=pl.ANY)],
            out_specs=pl.BlockSpec((1, N_HEADS, HEAD_DIM), tok),
            scratch_shapes=[pltpu.VMEM((2, N_KV, TOP_N * BLOCK_ROWS, HEAD_DIM), F32),
                            pltpu.SemaphoreType.DMA((2, N_KV, TOP_N))],
        ),
        out_shape=jax.ShapeDtypeStruct((n_tok, N_HEADS, HEAD_DIM), F32),
        compiler_params=pltpu.CompilerParams(dimension_semantics=("arbitrary", "arbitrary"),
                                             vmem_limit_bytes=VMEM_LIMIT),
        name="nsa_sample_b",
    )(idx, page_table, q, gate_sel, slopes8, o_cw, new_blocks, pool2d)


def _finish_kernel(x_ref, a_ref, r_ref, wo_ref, gn_ref, wr_ref, br_ref, h_ref, xn_ref, comb_ref):
    h = x_ref[...] + _mm(a_ref[...], wo_ref[0:D_ATTN, :]) + _mm(r_ref[...], wo_ref[D_ATTN:D_MODEL, :])
    h_ref[...] = h
    xn = _rms(h, gn_ref[...])
    xn_ref[...] = xn.astype(BF16)
    logit = _dot_f32(xn, wr_ref[...]) + br_ref[...]
    lane = lax.broadcasted_iota(jnp.int32, logit.shape, 1)
    lane_f = lane.astype(F32)
    is_g = (lane >= N_EXPERTS) & (lane < N_EXPERTS + N_GROUPS)
    gl = jnp.where(is_g, logit, NEG)
    g_max = jnp.max(gl, axis=-1, keepdims=True)
    g_star = jnp.min(jnp.where(gl == g_max, lane_f, 1e9), axis=-1, keepdims=True) - N_EXPERTS
    g_prob = 1.0 / jnp.sum(jnp.where(is_g, jnp.exp(gl - g_max), 0.0), axis=-1, keepdims=True)
    in_grp = (lane < N_EXPERTS) & ((lane // EXP_PER_GROUP).astype(F32) == g_star)
    el = jnp.where(in_grp, logit, NEG)
    e_max = jnp.max(el, axis=-1, keepdims=True)
    ee = jnp.where(in_grp, jnp.exp(el - e_max), 0.0)
    ep = jnp.where(in_grp, ee / jnp.sum(ee, axis=-1, keepdims=True), -1.0)
    p1 = jnp.max(ep, axis=-1, keepdims=True)
    i1 = jnp.min(jnp.where(ep == p1, lane_f, 1e9), axis=-1, keepdims=True)
    ep2 = jnp.where(lane_f == i1, -1.0, ep)
    p2 = jnp.max(ep2, axis=-1, keepdims=True)
    i2 = jnp.min(jnp.where(ep2 == p2, lane_f, 1e9), axis=-1, keepdims=True)
    tot = p1 + p2
    comb_ref[...] = (jnp.where(lane_f == i1, p1 / tot * g_prob, 0.0)
                     + jnp.where(lane_f == i2, p2 / tot * g_prob, 0.0)
                     + jnp.where(lane == GROUP_LANE, g_star, 0.0))


def _finish(x, attn_o, rnn_o, w_out, g_ffn, w_router, b_router, tm):
    n = x.shape[0]
    row = lambda i: (i, 0)
    fixed = lambda i: (0, 0)
    return pl.pallas_call(
        _finish_kernel,
        grid=(n // tm,),
        in_specs=[pl.BlockSpec((tm, D_MODEL), row), pl.BlockSpec((tm, D_ATTN), row), pl.BlockSpec((tm, D_RNN), row),
                  pl.BlockSpec((D_MODEL, D_MODEL), fixed, pipeline_mode=pl.Buffered(1)),
                  pl.BlockSpec((1, D_MODEL), fixed),
                  pl.BlockSpec((D_MODEL, LANES), fixed), pl.BlockSpec((1, LANES), fixed)],
        out_specs=[pl.BlockSpec((tm, D_MODEL), row), pl.BlockSpec((tm, D_MODEL), row), pl.BlockSpec((tm, LANES), row)],
        out_shape=[jax.ShapeDtypeStruct((n, D_MODEL), F32), jax.ShapeDtypeStruct((n, D_MODEL), BF16),
                   jax.ShapeDtypeStruct((n, LANES), F32)],
        compiler_params=pltpu.CompilerParams(dimension_semantics=("arbitrary",), vmem_limit_bytes=VMEM_LIMIT),
        name="finish",
    )(x, attn_o, rnn_o, w_out, g_ffn, w_router, b_router)


MOE_SUB = 128


def _moe_kernel(xn_ref, comb_ref, h_ref, wg_ref, wu_ref, wd_ref, gf_ref, y_ref, xs_scr, cs_scr, pt_scr, acc_scr):
    e = pl.program_id(1)
    tm = xn_ref.shape[0]

    @pl.when(e == 0)
    def _sort_rows():
        comb = comb_ref[...]
        lane = lax.broadcasted_iota(jnp.int32, comb.shape, 1)
        grp = jnp.sum(jnp.where(lane == GROUP_LANE, comb, 0.0), axis=-1, keepdims=True)
        onehot = jnp.where(lane.astype(F32) == grp, 1.0, 0.0)
        earlier = jnp.where(lax.broadcasted_iota(jnp.int32, (tm, tm), 1)
                            < lax.broadcasted_iota(jnp.int32, (tm, tm), 0), 1.0, 0.0).astype(BF16)
        before = _dot(earlier, onehot.astype(BF16))
        rank = jnp.sum(onehot * before, axis=-1, keepdims=True)
        count = jnp.sum(onehot, axis=0, keepdims=True)
        first = jnp.sum(jnp.where(lane.astype(F32) < grp, count, 0.0), axis=-1, keepdims=True)
        pos = first + rank
        to_sorted_t = jnp.where(lax.broadcasted_iota(jnp.int32, (tm, tm), 1).astype(F32) == pos, 1.0, 0.0)
        pt_scr[...] = to_sorted_t.astype(BF16)
        to_sorted = to_sorted_t.T.astype(BF16)
        xs_scr[...] = _dot(to_sorted, xn_ref[...]).astype(BF16)
        c1, c2, c3 = _split3(comb)
        cs_scr[...] = (_dot(to_sorted, c3) + _dot(to_sorted, c2)) + _dot(to_sorted, c1)
        acc_scr[...] = jnp.zeros(acc_scr.shape, F32)

    lane = lax.broadcasted_iota(jnp.int32, (MOE_SUB, LANES), 1)
    for s in range(tm // MOE_SUB):
        rows = slice(s * MOE_SUB, (s + 1) * MOE_SUB)
        cw = jnp.sum(jnp.where(lane == e, cs_scr[rows, :], 0.0), axis=-1, keepdims=True)

        @pl.when(jnp.max(cw) > 0.0)
        def _():
            x = xs_scr[rows, :]
            hh = jax.nn.silu(_dot(x, wg_ref[0])) * _dot(x, wu_ref[0]) * cw
            acc_scr[rows, :] += _dot(hh.astype(BF16), wd_ref[0])

    @pl.when(e == pl.num_programs(1) - 1)
    def _():
        a1, a2, _ = _split3(acc_scr[...])
        ffn = _dot(pt_scr[...], a2) + _dot(pt_scr[...], a1)
        y_ref[...] = _rms(h_ref[...] + ffn, gf_ref[...])


def _moe(xn, comb, h, w_gate, w_up, w_down, g_final, tm):
    n = xn.shape[0]
    row = lambda i, e: (i, 0)
    return pl.pallas_call(
        _moe_kernel,
        grid=(n // tm, N_EXPERTS),
        in_specs=[pl.BlockSpec((tm, D_MODEL), row), pl.BlockSpec((tm, LANES), row), pl.BlockSpec((tm, D_MODEL), row),
                  pl.BlockSpec((1, D_MODEL, D_EXPERT), lambda i, e: (e, 0, 0)),
                  pl.BlockSpec((1, D_MODEL, D_EXPERT), lambda i, e: (e, 0, 0)),
                  pl.BlockSpec((1, D_EXPERT, D_MODEL), lambda i, e: (e, 0, 0)),
                  pl.BlockSpec((1, D_MODEL), lambda i, e: (0, 0))],
        out_specs=pl.BlockSpec((tm, D_MODEL), row),
        out_shape=jax.ShapeDtypeStruct((n, D_MODEL), F32),
        scratch_shapes=[pltpu.VMEM((tm, D_MODEL), BF16), pltpu.VMEM((tm, LANES), F32),
                        pltpu.VMEM((tm, tm), BF16), pltpu.VMEM((tm, D_MODEL), F32)],
        compiler_params=pltpu.CompilerParams(dimension_semantics=("arbitrary", "arbitrary"),
                                             vmem_limit_bytes=VMEM_LIMIT),
        name="moe",
    )(xn, comb, h, w_gate, w_up, w_down, g_final)


def _block_diag_tiles(w):
    per = 256 // RNN_BLOCK_DIM
    w4 = w.reshape(RNN_BLOCKS // per, per, RNN_BLOCK_DIM, RNN_BLOCK_DIM)
    eye = jnp.eye(per, dtype=w.dtype)
    tiles = jnp.einsum('tpde,pq->tpdqe', w4, eye)
    return tiles.reshape(RNN_BLOCKS // per, 256, 256)


def _layer(l, xp, xs, cache_cmp_kv, cache_sel_kv, cache_win_kv, state_conv, state_h, page_table,
           norm_mix, w_in, cmp_pool_w, conv_w, conv_b, lru_wa, lru_ba, lru_wx, lru_bx, lru_lambda, w_out,
           norm_ffn, w_router_group, b_router_group, w_router_expert, b_router_expert,
           w_exp_gate, w_exp_up, w_exp_down, final_gain):
    bp, sp, _ = xp.shape
    bs, ts, _ = xs.shape
    n_pages = page_table.shape[1]
    past = n_pages * PAGE_SIZE

    wi = w_in[l]
    gt_cols = wi[:, 2560:2584].reshape(D_MODEL, N_KV, GQA * 3)
    gt_cols = jnp.pad(gt_cols, ((0, 0), (0, 0), (0, LANES - GQA * 3))).reshape(D_MODEL, N_KV * LANES)
    w_proj_f32 = jnp.concatenate([wi[:, :2560], wi[:, 2584:], gt_cols], axis=1)
    w_proj = w_proj_f32.astype(BF16)
    g_mix = norm_mix[l].reshape(1, D_MODEL)
    wfull = jnp.tile(jnp.repeat(cmp_pool_w[l], HEAD_DIM, axis=1), (1, N_KV))
    w2 = jnp.broadcast_to(cmp_pool_w[l].T[:, :, None], (2, CMP_BLOCK, HEAD_DIM))
    wa_f32 = _block_diag_tiles(lru_wa[l])
    wx_f32 = _block_diag_tiles(lru_wx[l])
    row = lambda v: v.reshape(1, -1)
    slopes = jnp.exp2(-8.0 * jnp.arange(1, N_HEADS + 1, dtype=F32) / N_HEADS)
    w_o = w_out[l].astype(BF16)
    w_router = jnp.pad(jnp.concatenate([w_router_expert[l], w_router_group[l]], axis=1),
                       ((0, 0), (0, LANES - N_EXPERTS - N_GROUPS)))
    b_router = jnp.pad(jnp.concatenate([b_router_expert[l], b_router_group[l]]),
                       (0, LANES - N_EXPERTS - N_GROUPS)).reshape(1, LANES)
    wg, wu, wd = w_exp_gate[l].astype(BF16), w_exp_up[l].astype(BF16), w_exp_down[l].astype(BF16)
    lru = lambda wa, wx: (conv_w[l], row(conv_b[l]), wa, row(lru_ba[l]), wx, row(lru_bx[l]), row(lru_lambda[l]))

    def tail(x, attn_o, rnn_o, w_out_l, tm_f, tm_m):
        h, xn, comb = _finish(x, attn_o, rnn_o, w_out_l, row(norm_ffn[l]), w_router, b_router, tm_f)
        return _moe(xn, comb, h, wg, wu, wd, final_gain, tm_m)

    np_ = bp * sp
    q, kc, ks, kw, xr, xg, gt = _project(xp.reshape(np_, D_MODEL), g_mix, w_proj, 256)
    shp = lambda a: a.reshape(bp, sp, a.shape[-1])
    ident = jnp.arange(np_ // 1024, dtype=jnp.int32)
    cmp_p = _compress(ident, kc.reshape(np_ // 1024, 1024, KV_W), wfull, bp, sp // 1024, 1, 1024)
    attn_p = _nsa_prompt(slopes, shp(q), shp(gt), cmp_p, shp(ks), shp(kw))
    rnn_p, h_p = _rglru(shp(xr), shp(xg), jnp.zeros((bp, 8, D_RNN), F32), jnp.zeros((bp, 1, D_RNN), F32),
                       *lru(wa_f32.astype(BF16), wx_f32.astype(BF16)), tc=512)
    y_p = tail(xp.reshape(np_, D_MODEL), attn_p.reshape(np_, D_ATTN), rnn_p.reshape(np_, D_RNN), w_o, 256, 512)
    kv6 = lambda a, b_, t_: a.reshape(b_, t_, N_KV, 2, HEAD_DIM)
    outs_p = (y_p.reshape(bp, sp, D_MODEL), kv6(kc, bp, sp), kv6(ks, bp, sp),
              kv6(kw, bp, sp)[:, -min(WINDOW, sp):], shp(xr)[:, sp - (CONV_W - 1):], h_p.reshape(bp, D_RNN))

    ns_ = bs * ts
    q, kc, ks, kw, xr, xg, gt = _project_precise(xs.reshape(ns_, D_MODEL), g_mix, w_proj_f32)
    shs = lambda a: a.reshape(bs, ts, a.shape[-1])
    cmp_s = _compress_paged(page_table.reshape(-1), cache_cmp_kv[l].reshape(-1, HEAD_DIM), w2, bs, n_pages, 16)
    idx, o_cw = _nsa_sample_a(past, shs(q), shs(gt), cmp_s, cache_win_kv[l].reshape(-1, HEAD_DIM), shs(kw))
    idx = idx.reshape(bs, N_KV, ts, LANES)[..., :N_GATHER].reshape(-1)
    gate_sel = gt.reshape(ns_, N_KV, LANES)[:, :, :GQA * 3].reshape(ns_, N_HEADS, 3)[:, :, 1:2]
    gate_sel = jnp.broadcast_to(gate_sel, (ns_, N_HEADS, HEAD_DIM))
    slopes8 = jnp.broadcast_to(slopes.reshape(N_HEADS, 1), (N_HEADS, LANES))
    new_blocks = jnp.pad(ks.reshape(bs, ts * KV_ROWS, HEAD_DIM), ((0, 0), (0, BLOCK_ROWS - ts * KV_ROWS), (0, 0)))
    attn_s = _nsa_sample_b(past, idx, page_table.reshape(-1), q.reshape(ns_, N_HEADS, HEAD_DIM), gate_sel, slopes8,
                           o_cw.reshape(ns_, N_HEADS, HEAD_DIM), new_blocks, cache_sel_kv[l].reshape(-1, HEAD_DIM))
    conv8 = jnp.pad(state_conv[l], ((0, 0), (8 - (CONV_W - 1), 0), (0, 0)))
    rnn_s, h_s = _rglru(shs(xr), shs(xg), conv8, state_h[l].reshape(bs, 1, D_RNN), *lru(wa_f32, wx_f32), tc=ts)
    y_s = tail(xs.reshape(ns_, D_MODEL), attn_s.reshape(ns_, D_ATTN), rnn_s.reshape(ns_, D_RNN), w_out[l], ns_, ns_)
    win_s = jnp.concatenate([cache_win_kv[l], kv6(kw, bs, ts)], axis=1)[:, ts:]
    conv_s = jnp.concatenate([state_conv[l], shs(xr)], axis=1)[:, ts:]
    outs_s = (y_s.reshape(bs, ts, D_MODEL), kv6(kc, bs, ts), kv6(ks, bs, ts),
              win_s, conv_s, h_s.reshape(bs, D_RNN))
    return outs_p, outs_s


def kernel(x_prompt, x_sample, cache_cmp_kv, cache_sel_kv, cache_win_kv, state_conv, state_h, page_table, norm_mix, w_in, cmp_pool_w, conv_w, conv_b, lru_wa, lru_ba, lru_wx, lru_bx, lru_lambda, w_out, norm_ffn, w_router_group, b_router_group, w_router_expert, b_router_expert, w_exp_gate, w_exp_up, w_exp_down, norm_final):
    depth = w_in.shape[0]
    assert depth == 1, "the final norm is fused into the single layer's expert kernel"
    p, s = _layer(0, x_prompt, x_sample, cache_cmp_kv, cache_sel_kv, cache_win_kv, state_conv, state_h, page_table,
                  norm_mix, w_in, cmp_pool_w, conv_w, conv_b, lru_wa, lru_ba, lru_wx, lru_bx, lru_lambda, w_out,
                  norm_ffn, w_router_group, b_router_group, w_router_expert, b_router_expert,
                  w_exp_gate, w_exp_up, w_exp_down, norm_final.reshape(1, D_MODEL))
    st = lambda a: a[None]
    return (p[0], s[0], st(p[1]), st(s[1]), st(p[2]), st(s[2]), st(p[3]), st(s[3]),
            st(p[4]), st(s[4]), st(p[5]), st(s[5]))
```

```python
import functools

import jax
import jax.numpy as jnp
from jax import lax
from jax.experimental import pallas as pl
from jax.experimental.pallas import tpu as pltpu

F32 = jnp.float32
BF16 = jnp.bfloat16

D_MODEL = 2048
D_ATTN = 1024
D_RNN = 1024
N_HEADS = 8
HEAD_DIM = 128
N_KV = 2
GQA = 4
KV_W = 512
CMP_BLOCK = 32
SEL_BLOCK = 64
TOP_N = 16
WINDOW = 512
FORCE_SCORE = 1e4
RNN_BLOCKS = 16
RNN_BLOCK_DIM = 64
CONV_W = 4
LRU_C = 8.0
N_GROUPS = 4
EXP_PER_GROUP = 4
N_EXPERTS = 16
D_EXPERT = 512
RMS_EPS = 1e-6
PAGE_SIZE = 128

GROUP_LANE = N_EXPERTS
LANES = 128
NEG = -1e30
VMEM_LIMIT = 56 * 1024 * 1024

_SEG_Q = (0, 1024)
_SEG_KC = (1024, 1536)
_SEG_KS = (1536, 2048)
_SEG_KW = (2048, 2560)
_SEG_XR = (2560, 3584)
_SEG_XG = (3584, 4608)
_SEG_GT = (4608, 4864)
_PROJ_W = 4864


def _dot(a, b):
    return jnp.dot(a, b, preferred_element_type=F32)


def _dot_nt(a, b):
    return lax.dot_general(a, b, (((1,), (1,)), ((), ())), preferred_element_type=F32)


def _rms(x, g):
    return x * lax.rsqrt(jnp.mean(x * x, axis=-1, keepdims=True) + RMS_EPS) * g


def _split3(x):
    h1 = x.astype(BF16)
    r1 = x - h1.astype(F32)
    h2 = r1.astype(BF16)
    h3 = (r1 - h2.astype(F32)).astype(BF16)
    return h1, h2, h3


def _dot_f32(a, b, dot=_dot):
    a1, a2, a3 = _split3(a)
    b1, b2, b3 = _split3(b)
    return (dot(a3, b1) + dot(a2, b2) + dot(a1, b3)) + (dot(a2, b1) + dot(a1, b2)) + dot(a1, b1)


def _dot_x3(a, b, dot=_dot):
    a1, a2, _ = _split3(a)
    b1, b2, _ = _split3(b)
    return (dot(a2, b1) + dot(a1, b2)) + dot(a1, b1)


def _mm(a, w):
    if w.dtype == BF16:
        return _dot(a.astype(BF16), w)
    return _dot_x3(a, w)


def _dot_sel(a, m01):
    a1, a2, a3 = _split3(a)
    return (_dot(a3, m01) + _dot(a2, m01)) + _dot(a1, m01)


def _proj_kernel(x_ref, g_ref, w_ref, q_ref, kc_ref, ks_ref, kw_ref, xr_ref, xg_ref, gt_ref):
    xn = _rms(x_ref[...], g_ref[...]).astype(BF16)
    outs = ((q_ref, _SEG_Q), (kc_ref, _SEG_KC), (ks_ref, _SEG_KS), (kw_ref, _SEG_KW),
            (xr_ref, _SEG_XR), (xg_ref, _SEG_XG), (gt_ref, _SEG_GT))
    for ref, (lo, hi) in outs:
        ref[...] = _dot(xn, w_ref[:, lo:hi])


def _project(x, g, w, tm):
    n = x.shape[0]
    widths = [hi - lo for lo, hi in (_SEG_Q, _SEG_KC, _SEG_KS, _SEG_KW, _SEG_XR, _SEG_XG, _SEG_GT)]
    return pl.pallas_call(
        _proj_kernel,
        grid=(n // tm,),
        in_specs=[pl.BlockSpec((tm, D_MODEL), lambda i: (i, 0)),
                  pl.BlockSpec((1, D_MODEL), lambda i: (0, 0)),
                  pl.BlockSpec((D_MODEL, _PROJ_W), lambda i: (0, 0), pipeline_mode=pl.Buffered(1))],
        out_specs=[pl.BlockSpec((tm, w_), lambda i: (i, 0)) for w_ in widths],
        out_shape=[jax.ShapeDtypeStruct((n, w_), F32) for w_ in widths],
        compiler_params=pltpu.CompilerParams(dimension_semantics=("arbitrary",), vmem_limit_bytes=VMEM_LIMIT),
        name="proj",
    )(x, g, w)


def _proj_precise_kernel(x_ref, g_ref, w_ref, z_ref, xn_scr):
    @pl.when(pl.program_id(0) == 0)
    def _():
        xn_scr[...] = _rms(x_ref[...], g_ref[...])

    z_ref[...] = _dot_x3(xn_scr[...], w_ref[...])


def _project_precise(x, g, w_f32):
    n = x.shape[0]
    tn = 2 * LANES
    z = pl.pallas_call(
        _proj_precise_kernel,
        grid=(_PROJ_W // tn,),
        in_specs=[pl.BlockSpec((n, D_MODEL), lambda j: (0, 0)),
                  pl.BlockSpec((1, D_MODEL), lambda j: (0, 0)),
                  pl.BlockSpec((D_MODEL, tn), lambda j: (0, j))],
        out_specs=pl.BlockSpec((n, tn), lambda j: (0, j)),
        out_shape=jax.ShapeDtypeStruct((n, _PROJ_W), F32),
        scratch_shapes=[pltpu.VMEM((n, D_MODEL), F32)],
        compiler_params=pltpu.CompilerParams(dimension_semantics=("arbitrary",), vmem_limit_bytes=VMEM_LIMIT),
        name="proj_precise",
    )(x, g, w_f32)
    return [z[:, lo:hi] for lo, hi in (_SEG_Q, _SEG_KC, _SEG_KS, _SEG_KW, _SEG_XR, _SEG_XG, _SEG_GT)]


def _compress_kernel(n_pages, pt_ref, *refs):
    del pt_ref
    pages, w_ref, o_ref = refs[:n_pages], refs[n_pages], refs[n_pages + 1]
    if n_pages == 1:
        x = pages[0][0]
    else:
        x = jnp.concatenate([p[0] for p in pages], axis=0)
    rows = x.shape[0]
    xb = x.reshape(rows // CMP_BLOCK, CMP_BLOCK, KV_W) * w_ref[...][None]
    o_ref[0] = jnp.sum(xb, axis=1)


def _compress(table, src, wfull, n_batch, steps, n_pages, page_rows):
    out_rows = n_pages * page_rows // CMP_BLOCK
    per_b = steps * n_pages

    def page_spec(k):
        return pl.BlockSpec((1, page_rows, KV_W), lambda b, j, pt: (pt[b * per_b + j * n_pages + k], 0, 0))

    return pl.pallas_call(
        functools.partial(_compress_kernel, n_pages),
        grid_spec=pltpu.PrefetchScalarGridSpec(
            num_scalar_prefetch=1,
            grid=(n_batch, steps),
            in_specs=[page_spec(k) for k in range(n_pages)] + [pl.BlockSpec((CMP_BLOCK, KV_W), lambda b, j, pt: (0, 0))],
            out_specs=pl.BlockSpec((1, out_rows, KV_W), lambda b, j, pt: (b, j, 0)),
        ),
        out_shape=jax.ShapeDtypeStruct((n_batch, steps * out_rows, KV_W), F32),
        compiler_params=pltpu.CompilerParams(dimension_semantics=("arbitrary", "arbitrary"), vmem_limit_bytes=VMEM_LIMIT),
        name="compress",
    )(table, *([src] * n_pages), wfull)


def _softplus(x):
    return jnp.maximum(x, 0.0) + jnp.log1p(jnp.exp(-jnp.abs(x)))


def _rglru_kernel(tc, xr_ref, xg_ref, cs_ref, h0_ref, cw_ref, cb_ref, wa_ref, ba_ref, wx_ref, bx_ref, lam_ref,
                  o_ref, hl_ref, tail_scr, h_scr):
    @pl.when(pl.program_id(2) == 0)
    def _():
        tail_scr[...] = cs_ref[0]
        h_scr[...] = h0_ref[0]

    x = xr_ref[0]
    xp = jnp.concatenate([tail_scr[...], x], axis=0)
    w = cw_ref[...]
    xc = cb_ref[...] + pltpu.roll(xp, 3, axis=0)[8:] * w[0:1]
    xc = xc + pltpu.roll(xp, 2, axis=0)[8:] * w[1:2]
    xc = xc + pltpu.roll(xp, 1, axis=0)[8:] * w[2:3]
    xc = xc + x * w[3:4]
    tail_scr[...] = xp[tc:]

    r = jax.nn.sigmoid(_mm(xc, wa_ref[0]) + ba_ref[...])
    gi = jax.nn.sigmoid(_mm(xc, wx_ref[0]) + bx_ref[...])
    log_a = -LRU_C * r * _softplus(-lam_ref[...])
    a = jnp.exp(log_a)
    u = jnp.sqrt(-jnp.tanh(log_a) * (a * a + 1.0)) * (gi * xc)

    row = lax.broadcasted_iota(jnp.int32, a.shape, 0)
    s = 1
    while s < tc:
        keep = row >= s
        a_sh = jnp.where(keep, pltpu.roll(a, s, axis=0), 1.0)
        u_sh = jnp.where(keep, pltpu.roll(u, s, axis=0), 0.0)
        u = a * u_sh + u
        a = a * a_sh
        s *= 2
    h = a * h_scr[...] + u
    h_last = h[tc - 1:tc]
    h_scr[...] = h_last
    hl_ref[0] = h_last
    o_ref[0] = h * jax.nn.gelu(xg_ref[0])


def _rglru(xr, xg, conv_state8, h0, conv_w, conv_b, wa_t, ba, wx_t, bx, lam, tc):
    b, t, _ = xr.shape
    ct = 256
    n_ct = D_RNN // ct
    seq = lambda bi, c, j: (bi, j, c)
    per_b = lambda bi, c, j: (bi, 0, c)
    par = lambda bi, c, j: (0, c)
    return pl.pallas_call(
        functools.partial(_rglru_kernel, tc),
        grid=(b, n_ct, t // tc),
        in_specs=[pl.BlockSpec((1, tc, ct), seq), pl.BlockSpec((1, tc, ct), seq),
                  pl.BlockSpec((1, 8, ct), per_b), pl.BlockSpec((1, 1, ct), per_b),
                  pl.BlockSpec((CONV_W, ct), par), pl.BlockSpec((1, ct), par),
                  pl.BlockSpec((1, ct, ct), lambda bi, c, j: (c, 0, 0)), pl.BlockSpec((1, ct), par),
                  pl.BlockSpec((1, ct, ct), lambda bi, c, j: (c, 0, 0)), pl.BlockSpec((1, ct), par),
                  pl.BlockSpec((1, ct), par)],
        out_specs=[pl.BlockSpec((1, tc, ct), seq), pl.BlockSpec((1, 1, ct), per_b)],
        out_shape=[jax.ShapeDtypeStruct((b, t, D_RNN), F32), jax.ShapeDtypeStruct((b, 1, D_RNN), F32)],
        scratch_shapes=[pltpu.VMEM((8, ct), F32), pltpu.VMEM((1, ct), F32)],
        compiler_params=pltpu.CompilerParams(dimension_semantics=("arbitrary", "arbitrary", "arbitrary"),
                                             vmem_limit_bytes=VMEM_LIMIT),
        name="rglru",
    )(xr, xg, conv_state8, h0, conv_w, conv_b, wa_t, ba, wx_t, bx, lam)


def _rglru_short_kernel(t, xr_ref, xg_ref, cs_ref, h0_ref, cw_ref, cb_ref, wa_ref, ba_ref, wx_ref, bx_ref, lam_ref,
                        o_ref, h_ref):
    x = xr_ref[...]
    st = cs_ref[...]
    n = x.shape[0]
    step = lax.broadcasted_iota(jnp.int32, x.shape, 0) % t
    w = cw_ref[...]
    xc = cb_ref[...] + x * w[CONV_W - 1:CONV_W]
    for sh in range(1, CONV_W):
        prev = jnp.where(step >= sh, pltpu.roll(x, sh, axis=0), pltpu.roll(st, n - (t - sh), axis=0))
        xc = xc + prev * w[CONV_W - 1 - sh:CONV_W - sh]
    r = jax.nn.sigmoid(_mm(xc, wa_ref[0]) + ba_ref[...])
    gi = jax.nn.sigmoid(_mm(xc, wx_ref[0]) + bx_ref[...])
    log_a = -LRU_C * r * _softplus(-lam_ref[...])
    a = jnp.exp(log_a)
    u = jnp.sqrt(-jnp.tanh(log_a) * (a * a + 1.0)) * (gi * xc)
    s = 1
    while s < t:
        keep = step >= s
        a_sh = jnp.where(keep, pltpu.roll(a, s, axis=0), 1.0)
        u_sh = jnp.where(keep, pltpu.roll(u, s, axis=0), 0.0)
        u = a * u_sh + u
        a = a * a_sh
        s *= 2
    h = a * h0_ref[...] + u
    h_ref[...] = h
    o_ref[...] = h * jax.nn.gelu(xg_ref[...])


def _rglru_short(xr, xg, conv_rows, h0_rows, conv_w, conv_b, wa_t, ba, wx_t, bx, lam, t):
    n = xr.shape[0]
    ct = 256
    blk = pl.BlockSpec((n, ct), lambda c: (0, c))
    par = lambda c: (0, c)
    tile = pl.BlockSpec((1, ct, ct), lambda c: (c, 0, 0))
    return pl.pallas_call(
        functools.partial(_rglru_short_kernel, t),
        grid=(D_RNN // ct,),
        in_specs=[blk, blk, blk, blk, pl.BlockSpec((CONV_W, ct), par), pl.BlockSpec((1, ct), par),
                  tile, pl.BlockSpec((1, ct), par), tile, pl.BlockSpec((1, ct), par), pl.BlockSpec((1, ct), par)],
        out_specs=[blk, blk],
        out_shape=[jax.ShapeDtypeStruct((n, D_RNN), F32), jax.ShapeDtypeStruct((n, D_RNN), F32)],
        compiler_params=pltpu.CompilerParams(dimension_semantics=("arbitrary",), vmem_limit_bytes=VMEM_LIMIT),
        name="rglru_short",
    )(xr, xg, conv_rows, h0_rows, conv_w, conv_b, wa_t, ba, wx_t, bx, lam)


TQ = 128
W4 = GQA * TQ
WIN_KEYS = WINDOW + TQ


def _nsa_prompt_kernel(slopes_ref, q_ref, gt_ref, kc_ref, vc_ref, ks_ref, vs_ref, kw_ref, vw_ref, o_ref,
                       q_scr, kcb, vct, ksb, vst, kwb, vwt, bias0, caus, wlow,
                       impt_scr, selb_scr, m_scr, l_scr, acc_scr, out_scr):
    g = pl.program_id(1)
    i = pl.program_id(2)
    s_len = ks_ref.shape[1]
    n_cmp = kc_ref.shape[1]
    n_sel = n_cmp // 2
    lane = lax.broadcasted_iota(jnp.int32, (1, W4), 1)
    tl_row = (lane % TQ).astype(F32)
    slope_row = jnp.full((1, W4), slopes_ref[g * GQA + GQA - 1], F32)
    for r in reversed(range(GQA - 1)):
        slope_row = jnp.where(lane < (r + 1) * TQ, slopes_ref[g * GQA + r], slope_row)

    @pl.when(i == 0)
    def _prepare():
        kcb[...] = kc_ref[0].astype(BF16)
        vct[...] = vc_ref[0].T.astype(BF16)

        def cast(j, carry):
            off = pl.multiple_of(j * LANES, LANES)
            ksb[pl.ds(off, LANES), :] = ks_ref[0, pl.ds(off, LANES), :].astype(BF16)
            kwb[pl.ds(off, LANES), :] = kw_ref[0, pl.ds(off, LANES), :].astype(BF16)
            vst[:, pl.ds(off, LANES)] = vs_ref[0, pl.ds(off, LANES), :].T.astype(BF16)
            vwt[:, pl.ds(off, LANES)] = vw_ref[0, pl.ds(off, LANES), :].T.astype(BF16)
            return carry

        lax.fori_loop(0, s_len // LANES, cast, 0)
        rel = tl_row - lax.broadcasted_iota(jnp.int32, (WIN_KEYS, W4), 0).astype(F32)
        bias0[...] = slope_row * rel
        caus[...] = jnp.where(rel[0:LANES] >= 0, 0.0, NEG)
        wlow[...] = jnp.where(rel[0:LANES] <= 0, 0.0, NEG)

    start_f = (i * TQ).astype(F32)
    scale = HEAD_DIM ** -0.5
    for r in range(GQA):
        q_scr[r * TQ:(r + 1) * TQ, :] = (q_ref[0, :, r * HEAD_DIM:(r + 1) * HEAD_DIM] * scale).astype(BF16)
    gate_t = jax.nn.sigmoid(gt_ref[0]).T

    def gate_row(branch):
        return jnp.concatenate([gate_t[3 * r + branch:3 * r + branch + 1, :] for r in range(GQA)], axis=1)

    t_row = start_f + tl_row
    c_end = ((lax.broadcasted_iota(jnp.int32, (n_cmp, W4), 0) + 1) * CMP_BLOCK - 1).astype(F32)
    dist_c = t_row - c_end
    ok_c = dist_c >= 0
    x = jnp.where(ok_c, _dot_nt(kcb[...], q_scr[...]) - slope_row * dist_c, NEG)
    e = jnp.where(ok_c, jnp.exp(x - jnp.max(x, axis=0, keepdims=True)), 0.0)
    p = e * (1.0 / jnp.maximum(jnp.sum(e, axis=0, keepdims=True), 1e-30))
    out_scr[...] = gate_row(0) * _dot(vct[...], p.astype(BF16))

    p_heads = p[:, 0:TQ]
    for r in range(1, GQA):
        p_heads = p_heads + p[:, r * TQ:(r + 1) * TQ]
    impt_scr[...] = p_heads
    imp = impt_scr[pl.ds(0, n_sel, stride=2), :] + impt_scr[pl.ds(1, n_sel, stride=2), :]
    blk = lax.broadcasted_iota(jnp.int32, (n_sel, TQ), 0)
    cur = (i * TQ + lax.broadcasted_iota(jnp.int32, (n_sel, TQ), 1)) // SEL_BLOCK
    forced = (blk == 0) | (blk == cur) | (blk == cur - 1)
    imp = jnp.where(forced, FORCE_SCORE, imp)
    imp = jnp.where(blk > cur, NEG, imp)
    rank = jnp.zeros((n_sel, TQ), jnp.int32)
    for j in range(n_sel):
        row = imp[j:j + 1, :]
        tie = jnp.where(blk > j, jnp.where(row == imp, 1, 0), 0)
        rank = rank + jnp.where(row > imp, 1, tie)
    selb = jnp.where(rank < TOP_N, 0.0, NEG)
    for j in range(n_sel):
        selb_scr[j] = selb[j:j + 1, :]

    def scores(k_b, c, nk):
        off = pl.multiple_of(c * LANES, LANES)
        return _dot_nt(k_b[pl.ds(off, nk), :], q_scr[...]) - bias0[0:nk, :]

    def block_mask(c, nk):
        rows = [jnp.broadcast_to(selb_scr[2 * c + k], (SEL_BLOCK, TQ)) for k in range(nk // SEL_BLOCK)]
        return jnp.concatenate([jnp.concatenate(rows, axis=0)] * GQA, axis=1)

    def update(v_t, c, nk, x):
        off = pl.multiple_of(c * LANES, LANES)
        r_c = slope_row * ((c * LANES).astype(F32) - start_f)
        m_old = m_scr[...]
        m_new = jnp.maximum(m_old, jnp.max(x, axis=0, keepdims=True) + r_c)
        pr = jnp.exp(x + (r_c - m_new))
        alpha = jnp.exp(m_old - m_new)
        l_scr[...] = alpha * l_scr[...] + jnp.sum(pr, axis=0, keepdims=True)
        acc_scr[...] = alpha * acc_scr[...] + _dot(v_t[:, pl.ds(off, nk)], pr.astype(BF16))
        m_scr[...] = m_new

    def reset():
        m_scr[...] = jnp.full((1, W4), NEG, F32)
        l_scr[...] = jnp.zeros((1, W4), F32)
        acc_scr[...] = jnp.zeros((HEAD_DIM, W4), F32)

    def result():
        return acc_scr[...] * (1.0 / jnp.maximum(l_scr[...], 1e-30))

    reset()
    n_big = i // 4

    def big(c4, carry):
        update(vst, 4 * c4, 4 * LANES, scores(ksb, 4 * c4, 4 * LANES) + block_mask(4 * c4, 4 * LANES))
        return carry

    lax.fori_loop(0, n_big, big, 0)
    rem = i - 4 * n_big

    @pl.when(rem >= 2)
    def _():
        update(vst, 4 * n_big, 2 * LANES, scores(ksb, 4 * n_big, 2 * LANES) + block_mask(4 * n_big, 2 * LANES))

    @pl.when(rem % 2 == 1)
    def _():
        update(vst, i - 1, LANES, scores(ksb, i - 1, LANES) + block_mask(i - 1, LANES))

    update(vst, i, LANES, scores(ksb, i, LANES) + block_mask(i, LANES) + caus[...])
    out_scr[...] += gate_row(1) * result()

    n_back = WINDOW // LANES

    @pl.when(i >= n_back)
    def _():
        c = i - n_back
        x = scores(kwb, c, WIN_KEYS)
        x = jnp.concatenate([x[0:LANES] + wlow[...], x[LANES:WINDOW], x[WINDOW:WIN_KEYS] + caus[...]], axis=0)
        pr = jnp.exp(x - jnp.max(x, axis=0, keepdims=True))
        o_w = _dot(vwt[:, pl.ds(pl.multiple_of(c * LANES, LANES), WIN_KEYS)], pr.astype(BF16))
        out_scr[...] += gate_row(2) * (o_w * (1.0 / jnp.maximum(jnp.sum(pr, axis=0, keepdims=True), 1e-30)))

    @pl.when(i < n_back)
    def _():
        reset()

        def body(c, carry):
            update(vwt, c, LANES, scores(kwb, c, LANES))
            return carry

        lax.fori_loop(0, i, body, 0)
        update(vwt, i, LANES, scores(kwb, i, LANES) + caus[...])
        out_scr[...] += gate_row(2) * result()

    o_t = out_scr[...]
    for r in range(GQA):
        o_ref[0, :, r * HEAD_DIM:(r + 1) * HEAD_DIM] = o_t[:, r * TQ:(r + 1) * TQ].T


def _nsa_prompt(slopes, q, gt, cmp_kv, ks, kw):
    b, s, _ = q.shape
    n_cmp = cmp_kv.shape[1]
    k_of = lambda bi, g, i: (bi, 0, 2 * g)
    v_of = lambda bi, g, i: (bi, 0, 2 * g + 1)
    tile = pltpu.VMEM((LANES, W4), F32)
    return pl.pallas_call(
        _nsa_prompt_kernel,
        grid=(b, N_KV, s // TQ),
        in_specs=[pl.BlockSpec(memory_space=pltpu.SMEM),
                  pl.BlockSpec((1, TQ, GQA * HEAD_DIM), lambda bi, g, i: (bi, i, g)),
                  pl.BlockSpec((1, TQ, LANES), lambda bi, g, i: (bi, i, g)),
                  pl.BlockSpec((1, n_cmp, HEAD_DIM), k_of), pl.BlockSpec((1, n_cmp, HEAD_DIM), v_of),
                  pl.BlockSpec((1, s, HEAD_DIM), k_of), pl.BlockSpec((1, s, HEAD_DIM), v_of),
                  pl.BlockSpec((1, s, HEAD_DIM), k_of), pl.BlockSpec((1, s, HEAD_DIM), v_of)],
        out_specs=pl.BlockSpec((1, TQ, GQA * HEAD_DIM), lambda bi, g, i: (bi, i, g)),
        out_shape=jax.ShapeDtypeStruct((b, s, D_ATTN), F32),
        scratch_shapes=[pltpu.VMEM((W4, HEAD_DIM), BF16),
                        pltpu.VMEM((n_cmp, HEAD_DIM), BF16), pltpu.VMEM((HEAD_DIM, n_cmp), BF16),
                        pltpu.VMEM((s, HEAD_DIM), BF16), pltpu.VMEM((HEAD_DIM, s), BF16),
                        pltpu.VMEM((s, HEAD_DIM), BF16), pltpu.VMEM((HEAD_DIM, s), BF16),
                        pltpu.VMEM((WIN_KEYS, W4), F32), tile, tile,
                        pltpu.VMEM((n_cmp, TQ), F32),
                        pltpu.VMEM((n_cmp // 2, 1, TQ), F32),
                        pltpu.VMEM((1, W4), F32), pltpu.VMEM((1, W4), F32),
                        pltpu.VMEM((HEAD_DIM, W4), F32), pltpu.VMEM((HEAD_DIM, W4), F32)],
        compiler_params=pltpu.CompilerParams(dimension_semantics=("arbitrary", "arbitrary", "arbitrary"),
                                             vmem_limit_bytes=VMEM_LIMIT),
        name="nsa_prompt",
    )(slopes, q, gt, cmp_kv, cmp_kv, ks, ks, kw, kw)


KV_ROWS = 2 * N_KV


def _compress_paged_kernel(n_pages, pt_ref, *refs):
    del pt_ref
    pages, w_ref, o_ref = refs[:n_pages], refs[n_pages], refs[n_pages + 1]
    for gc in range(KV_ROWS):
        x = jnp.concatenate([p[pl.ds(gc, PAGE_SIZE, stride=KV_ROWS), :] for p in pages], axis=0)
        xb = x.reshape(n_pages * PAGE_SIZE // CMP_BLOCK, CMP_BLOCK, HEAD_DIM) * w_ref[gc % 2][None]
        o_ref[0, gc] = jnp.sum(xb, axis=1)


def _compress_paged(table, pool2d, w2, n_batch, pages_per_b, n_pages):
    steps = pages_per_b // n_pages
    out_rows = n_pages * PAGE_SIZE // CMP_BLOCK
    rows = PAGE_SIZE * KV_ROWS

    def page_spec(k):
        return pl.BlockSpec((rows, HEAD_DIM), lambda b, j, pt: (pt[b * pages_per_b + j * n_pages + k], 0))

    return pl.pallas_call(
        functools.partial(_compress_paged_kernel, n_pages),
        grid_spec=pltpu.PrefetchScalarGridSpec(
            num_scalar_prefetch=1,
            grid=(n_batch, steps),
            in_specs=[page_spec(k) for k in range(n_pages)]
            + [pl.BlockSpec((2, CMP_BLOCK, HEAD_DIM), lambda b, j, pt: (0, 0, 0))],
            out_specs=pl.BlockSpec((1, KV_ROWS, out_rows, HEAD_DIM), lambda b, j, pt: (b, 0, j, 0)),
        ),
        out_shape=jax.ShapeDtypeStruct((n_batch, KV_ROWS, steps * out_rows, HEAD_DIM), F32),
        compiler_params=pltpu.CompilerParams(dimension_semantics=("arbitrary", "arbitrary"), vmem_limit_bytes=VMEM_LIMIT),
        name="compress_paged",
    )(table, *([pool2d] * n_pages), w2)


def _nsa_sample_a_kernel(past, q_ref, gt_ref, cmp_ref, cw_ref, kwn_ref, idx_ref, o_ref):
    t_new = q_ref.shape[1]
    n_cmp = cmp_ref.shape[2]
    n_past_sel = n_cmp // 2
    n_win = cw_ref.shape[0] // KV_ROWS
    rows = GQA * t_new
    gates = jax.nn.sigmoid(gt_ref[0])
    t_row = past + lax.broadcasted_iota(jnp.int32, (rows, 1), 0) % t_new
    scale = HEAD_DIM ** -0.5
    pair = jnp.where(lax.broadcasted_iota(jnp.int32, (n_cmp, n_past_sel), 0) // 2
                     == lax.broadcasted_iota(jnp.int32, (n_cmp, n_past_sel), 1), 1.0, 0.0).astype(BF16)
    imps = []
    for g in range(N_KV):
        qg = jnp.concatenate([q_ref[0, :, (g * GQA + r) * HEAD_DIM:(g * GQA + r + 1) * HEAD_DIM]
                              for r in range(GQA)], axis=0)
        qg = qg * scale
        slope = jnp.concatenate([jnp.full((t_new, 1), 2.0 ** -(g * GQA + r + 1), F32) for r in range(GQA)], axis=0)
        kcol = g * 2 * HEAD_DIM
        kc = cmp_ref[0, 2 * g]
        vc = cmp_ref[0, 2 * g + 1]
        c_end = (lax.broadcasted_iota(jnp.int32, (1, n_cmp), 1) + 1) * CMP_BLOCK - 1
        dist_c = t_row - c_end
        mask_c = dist_c >= 0
        s = _dot_x3(qg, kc, _dot_nt) - slope * dist_c.astype(F32)
        s = jnp.where(mask_c, s, NEG)
        m = jnp.max(s, axis=-1, keepdims=True)
        p = jnp.where(mask_c, jnp.exp(s - m), 0.0)
        p = p / jnp.maximum(jnp.sum(p, axis=-1, keepdims=True), 1e-30)
        o_c = _dot_x3(p, vc)
        p_heads = p[0:t_new]
        for r in range(1, GQA):
            p_heads = p_heads + p[r * t_new:(r + 1) * t_new]
        imps.append(_dot_sel(p_heads, pair))
        n_pad = LANES - t_new
        kw = jnp.concatenate([cw_ref[pl.ds(2 * g, n_win, stride=KV_ROWS), :], kwn_ref[0, :, kcol:kcol + HEAD_DIM],
                              jnp.zeros((n_pad, HEAD_DIM), F32)], axis=0)
        vw = jnp.concatenate([cw_ref[pl.ds(2 * g + 1, n_win, stride=KV_ROWS), :],
                              kwn_ref[0, :, kcol + HEAD_DIM:kcol + 2 * HEAD_DIM],
                              jnp.zeros((n_pad, HEAD_DIM), F32)], axis=0)
        win_pos = past - n_win + lax.broadcasted_iota(jnp.int32, (1, n_win + LANES), 1)
        dist_w = t_row - win_pos
        mask_w = (dist_w >= 0) & (dist_w <= WINDOW)
        s = _dot_x3(qg, kw, _dot_nt) - slope * dist_w.astype(F32)
        s = jnp.where(mask_w, s, NEG)
        m = jnp.max(s, axis=-1, keepdims=True)
        p = jnp.where(mask_w, jnp.exp(s - m), 0.0)
        p = p / jnp.maximum(jnp.sum(p, axis=-1, keepdims=True), 1e-30)
        o_w = _dot_x3(p, vw)
        for r in range(GQA):
            h = g * GQA + r
            gl = g * LANES + 3 * r
            o_ref[0, :, h * HEAD_DIM:(h + 1) * HEAD_DIM] = (
                gates[:, gl:gl + 1] * o_c[r * t_new:(r + 1) * t_new]
                + gates[:, gl + 2:gl + 3] * o_w[r * t_new:(r + 1) * t_new])

    imp = jnp.concatenate(imps, axis=0)
    n_rows = N_KV * t_new
    lane = lax.broadcasted_iota(jnp.int32, (n_rows, n_past_sel), 1)
    lane_f = lane.astype(F32)
    cur = (past + lax.broadcasted_iota(jnp.int32, (n_rows, n_past_sel), 0) % t_new) // SEL_BLOCK
    forced = (lane == 0) | (lane == cur) | (lane == cur - 1)
    imp = jnp.where(forced, FORCE_SCORE, imp)
    imp = jnp.where(lane > cur, NEG, imp)
    out_lane = lax.broadcasted_iota(jnp.int32, (n_rows, LANES), 1)
    idx = jnp.zeros((n_rows, LANES), F32)
    for k in range(TOP_N - 1):
        m = jnp.max(imp, axis=-1, keepdims=True)
        j = jnp.min(jnp.where(imp == m, lane_f, float(n_past_sel)), axis=-1, keepdims=True)
        idx = jnp.where(out_lane == k, j, idx)
        imp = jnp.where(lane_f == j, -3e38, imp)
    idx_ref[0] = idx.astype(jnp.int32)


def _nsa_sample_a(past, q, gt, cmp_kv, cache_w2d, kw_new):
    b, t_new, _ = q.shape
    win_rows = cache_w2d.shape[0] // b
    per_b3 = lambda bi: (bi, 0, 0)
    return pl.pallas_call(
        functools.partial(_nsa_sample_a_kernel, past),
        grid=(b,),
        in_specs=[pl.BlockSpec((1, t_new, D_ATTN), per_b3), pl.BlockSpec((1, t_new, 2 * LANES), per_b3),
                  pl.BlockSpec((1,) + cmp_kv.shape[1:], lambda bi: (bi, 0, 0, 0)),
                  pl.BlockSpec((win_rows, HEAD_DIM), lambda bi: (bi, 0)),
                  pl.BlockSpec((1, t_new, KV_W), per_b3)],
        out_specs=[pl.BlockSpec((1, N_KV * t_new, LANES), per_b3), pl.BlockSpec((1, t_new, D_ATTN), per_b3)],
        out_shape=[jax.ShapeDtypeStruct((b, N_KV * t_new, LANES), jnp.int32),
                   jax.ShapeDtypeStruct((b, t_new, D_ATTN), F32)],
        compiler_params=pltpu.CompilerParams(dimension_semantics=("arbitrary",), vmem_limit_bytes=VMEM_LIMIT),
        name="nsa_sample_a",
    )(q, gt, cmp_kv, cache_w2d, kw_new)


N_GATHER = TOP_N - 1
BLOCK_ROWS = SEL_BLOCK * KV_ROWS
GATHER_KEYS = TOP_N * SEL_BLOCK


def _nsa_sample_b_kernel(past, t_new, idx_ref, pt_ref, q_ref, gs_ref, slope_ref, ocw_ref, new_ref, pool_ref,
                         o_ref, kv_buf, sems):
    n_steps = pl.num_programs(0) * t_new
    step = pl.program_id(0) * t_new + pl.program_id(1)
    slot = step % 2
    pages_per_b = past // PAGE_SIZE
    sel_per_page = PAGE_SIZE // SEL_BLOCK

    def block_index(st, g, k):
        return idx_ref[((st // t_new * N_KV + g) * t_new + st % t_new) * N_GATHER + k]

    def copies(st, sl):
        out = []
        for g in range(N_KV):
            for k in range(N_GATHER):
                lp = block_index(st, g, k)
                phys = pt_ref[st // t_new * pages_per_b + lp // sel_per_page] * sel_per_page + lp % sel_per_page
                out.append(pltpu.make_async_copy(
                    pool_ref.at[pl.ds(pl.multiple_of(phys * BLOCK_ROWS, BLOCK_ROWS), BLOCK_ROWS), :],
                    kv_buf.at[sl, g, pl.ds(k * BLOCK_ROWS, BLOCK_ROWS), :], sems.at[sl, g, k]))
            out.append(pltpu.make_async_copy(
                new_ref.at[st // t_new], kv_buf.at[sl, g, pl.ds(N_GATHER * BLOCK_ROWS, BLOCK_ROWS), :],
                sems.at[sl, g, N_GATHER]))
        return out

    @pl.when(step == 0)
    def _():
        for cp in copies(step, slot):
            cp.start()

    @pl.when(step + 1 < n_steps)
    def _():
        for cp in copies(step + 1, 1 - slot):
            cp.start()

    t_pos = past + pl.program_id(1)
    q8 = q_ref[0] * HEAD_DIM ** -0.5
    slope = slope_ref[:, 0:1]
    lane = lax.broadcasted_iota(jnp.int32, (1, GATHER_KEYS), 1)
    for cp in copies(step, slot):
        cp.wait()
    outs = []
    for g in range(N_KV):
        pos = past + (lane - N_GATHER * SEL_BLOCK)
        for k in range(N_GATHER):
            pos = jnp.where(lane // SEL_BLOCK == k, block_index(step, g, k) * SEL_BLOCK + lane % SEL_BLOCK, pos)
        kk = kv_buf[slot, g, pl.ds(2 * g, GATHER_KEYS, stride=KV_ROWS), :]
        vv = kv_buf[slot, g, pl.ds(2 * g + 1, GATHER_KEYS, stride=KV_ROWS), :]
        dist = t_pos - pos
        mask = dist >= 0
        s = _dot_x3(q8, kk, _dot_nt) - slope * dist.astype(F32)
        s = jnp.where(mask, s, NEG)
        m = jnp.max(s, axis=-1, keepdims=True)
        p = jnp.where(mask, jnp.exp(s - m), 0.0)
        p = p / jnp.maximum(jnp.sum(p, axis=-1, keepdims=True), 1e-30)
        outs.append(_dot_x3(p, vv))
    head = lax.broadcasted_iota(jnp.int32, (N_HEADS, HEAD_DIM), 0)
    o_s = jnp.where(head < GQA, outs[0], outs[1])
    o_ref[0] = ocw_ref[0] + jax.nn.sigmoid(gs_ref[0]) * o_s


def _nsa_sample_b(past, idx, page_table, q, gate_sel, slopes8, o_cw, new_blocks, pool2d):
    n_tok = q.shape[0]
    b = new_blocks.shape[0]
    t_new = n_tok // b
    tok = lambda bi, ti, *_: (bi * t_new + ti, 0, 0)
    return pl.pallas_call(
        functools.partial(_nsa_sample_b_kernel, past, t_new),
        grid_spec=pltpu.PrefetchScalarGridSpec(
            num_scalar_prefetch=2,
            grid=(b, t_new),
            in_specs=[pl.BlockSpec((1, N_HEADS, HEAD_DIM), tok), pl.BlockSpec((1, N_HEADS, HEAD_DIM), tok),
                      pl.BlockSpec((N_HEADS, LANES), lambda bi, ti, *_: (0, 0)),
                      pl.BlockSpec((1, N_HEADS, HEAD_DIM), tok),
                      pl.BlockSpec(memory_space=pl.ANY), pl.BlockSpec(memory_space=pl.ANY)],
            out_specs=pl.BlockSpec((1, N_HEADS, HEAD_DIM), tok),
            scratch_shapes=[pltpu.VMEM((2, N_KV, TOP_N * BLOCK_ROWS, HEAD_DIM), F32),
                            pltpu.SemaphoreType.DMA((2, N_KV, TOP_N))],
        ),
        out_shape=jax.ShapeDtypeStruct((n_tok, N_HEADS, HEAD_DIM), F32),
        compiler_params=pltpu.CompilerParams(dimension_semantics=("arbitrary", "arbitrary"),
                                             vmem_limit_bytes=VMEM_LIMIT),
        name="nsa_sample_b",
    )(idx, page_table, q, gate_sel, slopes8, o_cw, new_blocks, pool2d)


def _finish_kernel(x_ref, a_ref, r_ref, wo_ref, gn_ref, wr_ref, br_ref, h_ref, xn_ref, comb_ref):
    h = x_ref[...] + _mm(a_ref[...], wo_ref[0:D_ATTN, :]) + _mm(r_ref[...], wo_ref[D_ATTN:D_MODEL, :])
    h_ref[...] = h
    xn = _rms(h, gn_ref[...])
    xn_ref[...] = xn.astype(BF16)
    logit = _dot_f32(xn, wr_ref[...]) + br_ref[...]
    lane = lax.broadcasted_iota(jnp.int32, logit.shape, 1)
    lane_f = lane.astype(F32)
    is_g = (lane >= N_EXPERTS) & (lane < N_EXPERTS + N_GROUPS)
    gl = jnp.where(is_g, logit, NEG)
    g_max = jnp.max(gl, axis=-1, keepdims=True)
    g_star = jnp.min(jnp.where(gl == g_max, lane_f, 1e9), axis=-1, keepdims=True) - N_EXPERTS
    g_prob = 1.0 / jnp.sum(jnp.where(is_g, jnp.exp(gl - g_max), 0.0), axis=-1, keepdims=True)
    in_grp = (lane < N_EXPERTS) & ((lane // EXP_PER_GROUP).astype(F32) == g_star)
    el = jnp.where(in_grp, logit, NEG)
    e_max = jnp.max(el, axis=-1, keepdims=True)
    ee = jnp.where(in_grp, jnp.exp(el - e_max), 0.0)
    ep = jnp.where(in_grp, ee / jnp.sum(ee, axis=-1, keepdims=True), -1.0)
    p1 = jnp.max(ep, axis=-1, keepdims=True)
    i1 = jnp.min(jnp.where(ep == p1, lane_f, 1e9), axis=-1, keepdims=True)
    ep2 = jnp.where(lane_f == i1, -1.0, ep)
    p2 = jnp.max(ep2, axis=-1, keepdims=True)
    i2 = jnp.min(jnp.where(ep2 == p2, lane_f, 1e9), axis=-1, keepdims=True)
    tot = p1 + p2
    comb_ref[...] = (jnp.where(lane_f == i1, p1 / tot * g_prob, 0.0)
                     + jnp.where(lane_f == i2, p2 / tot * g_prob, 0.0)
                     + jnp.where(lane == GROUP_LANE, g_star, 0.0))


def _finish(x, attn_o, rnn_o, w_out, g_ffn, w_router, b_router, tm):
    n = x.shape[0]
    row = lambda i: (i, 0)
    fixed = lambda i: (0, 0)
    return pl.pallas_call(
        _finish_kernel,
        grid=(n // tm,),
        in_specs=[pl.BlockSpec((tm, D_MODEL), row), pl.BlockSpec((tm, D_ATTN), row), pl.BlockSpec((tm, D_RNN), row),
                  pl.BlockSpec((D_MODEL, D_MODEL), fixed, pipeline_mode=pl.Buffered(1)),
                  pl.BlockSpec((1, D_MODEL), fixed),
                  pl.BlockSpec((D_MODEL, LANES), fixed), pl.BlockSpec((1, LANES), fixed)],
        out_specs=[pl.BlockSpec((tm, D_MODEL), row), pl.BlockSpec((tm, D_MODEL), row), pl.BlockSpec((tm, LANES), row)],
        out_shape=[jax.ShapeDtypeStruct((n, D_MODEL), F32), jax.ShapeDtypeStruct((n, D_MODEL), BF16),
                   jax.ShapeDtypeStruct((n, LANES), F32)],
        compiler_params=pltpu.CompilerParams(dimension_semantics=("arbitrary",), vmem_limit_bytes=VMEM_LIMIT),
        name="finish",
    )(x, attn_o, rnn_o, w_out, g_ffn, w_router, b_router)


MOE_SUB = 128


MOE_SORT = 512


def _moe_kernel(sd, xn_ref, comb_ref, h_ref, wg_ref, wu_ref, wd_ref, gf_ref, y_ref, xs_scr, cs_scr, pt_scr):
    e = pl.program_id(1)
    tm = xn_ref.shape[0]
    domains = [slice(d * sd, (d + 1) * sd) for d in range(tm // sd)]

    @pl.when(e == 0)
    def _sort_rows():
        for d, dom in enumerate(domains):
            comb = comb_ref[dom, :]
            lane = lax.broadcasted_iota(jnp.int32, comb.shape, 1)
            grp = jnp.sum(jnp.where(lane == GROUP_LANE, comb, 0.0), axis=-1, keepdims=True)
            onehot = jnp.where(lane.astype(F32) == grp, 1.0, 0.0)
            earlier = jnp.where(lax.broadcasted_iota(jnp.int32, (sd, sd), 1)
                                < lax.broadcasted_iota(jnp.int32, (sd, sd), 0), 1.0, 0.0).astype(BF16)
            before = _dot(earlier, onehot.astype(BF16))
            rank = jnp.sum(onehot * before, axis=-1, keepdims=True)
            count = jnp.sum(onehot, axis=0, keepdims=True)
            first = jnp.sum(jnp.where(lane.astype(F32) < grp, count, 0.0), axis=-1, keepdims=True)
            pos = first + rank
            to_sorted_t = jnp.where(lax.broadcasted_iota(jnp.int32, (sd, sd), 1).astype(F32) == pos, 1.0, 0.0)
            pt_scr[d] = to_sorted_t.astype(BF16)
            to_sorted = to_sorted_t.T.astype(BF16)
            xs_scr[dom, :] = _dot(to_sorted, xn_ref[dom, :]).astype(BF16)
            c1, c2, c3 = _split3(comb)
            cs_scr[dom, :] = (_dot(to_sorted, c3) + _dot(to_sorted, c2)) + _dot(to_sorted, c1)
        y_ref[...] = jnp.zeros(y_ref.shape, F32)

    lane = lax.broadcasted_iota(jnp.int32, (MOE_SUB, LANES), 1)
    for s in range(tm // MOE_SUB):
        rows = slice(s * MOE_SUB, (s + 1) * MOE_SUB)
        cw = jnp.sum(jnp.where(lane == e, cs_scr[rows, :], 0.0), axis=-1, keepdims=True)

        @pl.when(jnp.max(cw) > 0.0)
        def _():
            x = xs_scr[rows, :]
            hh = jax.nn.silu(_dot(x, wg_ref[0])) * _dot(x, wu_ref[0]) * cw
            y_ref[rows, :] += _dot(hh.astype(BF16), wd_ref[0])

    @pl.when(e == pl.num_programs(1) - 1)
    def _():
        for d, dom in enumerate(domains):
            a1, a2, _ = _split3(y_ref[dom, :])
            ffn = _dot(pt_scr[d], a2) + _dot(pt_scr[d], a1)
            y_ref[dom, :] = _rms(h_ref[dom, :] + ffn, gf_ref[...])


def _moe(xn, comb, h, w_gate, w_up, w_down, g_final, tm):
    n = xn.shape[0]
    sd = min(tm, MOE_SORT)
    row = lambda i, e: (i, 0)
    return pl.pallas_call(
        functools.partial(_moe_kernel, sd),
        grid=(n // tm, N_EXPERTS),
        in_specs=[pl.BlockSpec((tm, D_MODEL), row), pl.BlockSpec((tm, LANES), row),
                  pl.BlockSpec((tm, D_MODEL), row, pipeline_mode=pl.Buffered(1)),
                  pl.BlockSpec((1, D_MODEL, D_EXPERT), lambda i, e: (e, 0, 0)),
                  pl.BlockSpec((1, D_MODEL, D_EXPERT), lambda i, e: (e, 0, 0)),
                  pl.BlockSpec((1, D_EXPERT, D_MODEL), lambda i, e: (e, 0, 0)),
                  pl.BlockSpec((1, D_MODEL), lambda i, e: (0, 0))],
        out_specs=pl.BlockSpec((tm, D_MODEL), row),
        out_shape=jax.ShapeDtypeStruct((n, D_MODEL), F32),
        scratch_shapes=[pltpu.VMEM((tm, D_MODEL), BF16), pltpu.VMEM((tm, LANES), F32),
                        pltpu.VMEM((tm // sd, sd, sd), BF16)],
        compiler_params=pltpu.CompilerParams(dimension_semantics=("arbitrary", "arbitrary"),
                                             vmem_limit_bytes=VMEM_LIMIT),
        name="moe",
    )(xn, comb, h, w_gate, w_up, w_down, g_final)


def _block_diag_tiles(w):
    per = 256 // RNN_BLOCK_DIM
    w4 = w.reshape(RNN_BLOCKS // per, per, RNN_BLOCK_DIM, RNN_BLOCK_DIM)
    eye = jnp.eye(per, dtype=w.dtype)
    tiles = jnp.einsum('tpde,pq->tpdqe', w4, eye)
    return tiles.reshape(RNN_BLOCKS // per, 256, 256)


def _layer(l, xp, xs, cache_cmp_kv, cache_sel_kv, cache_win_kv, state_conv, state_h, page_table,
           norm_mix, w_in, cmp_pool_w, conv_w, conv_b, lru_wa, lru_ba, lru_wx, lru_bx, lru_lambda, w_out,
           norm_ffn, w_router_group, b_router_group, w_router_expert, b_router_expert,
           w_exp_gate, w_exp_up, w_exp_down, final_gain):
    bp, sp, _ = xp.shape
    bs, ts, _ = xs.shape
    n_pages = page_table.shape[1]
    past = n_pages * PAGE_SIZE

    wi = w_in[l]
    gt_cols = wi[:, 2560:2584].reshape(D_MODEL, N_KV, GQA * 3)
    gt_cols = jnp.pad(gt_cols, ((0, 0), (0, 0), (0, LANES - GQA * 3))).reshape(D_MODEL, N_KV * LANES)
    w_proj_f32 = jnp.concatenate([wi[:, :2560], wi[:, 2584:], gt_cols], axis=1)
    w_proj = w_proj_f32.astype(BF16)
    g_mix = norm_mix[l].reshape(1, D_MODEL)
    wfull = jnp.tile(jnp.repeat(cmp_pool_w[l], HEAD_DIM, axis=1), (1, N_KV))
    w2 = jnp.broadcast_to(cmp_pool_w[l].T[:, :, None], (2, CMP_BLOCK, HEAD_DIM))
    wa_f32 = _block_diag_tiles(lru_wa[l])
    wx_f32 = _block_diag_tiles(lru_wx[l])
    row = lambda v: v.reshape(1, -1)
    slopes = jnp.exp2(-8.0 * jnp.arange(1, N_HEADS + 1, dtype=F32) / N_HEADS)
    w_o = w_out[l].astype(BF16)
    w_router = jnp.pad(jnp.concatenate([w_router_expert[l], w_router_group[l]], axis=1),
                       ((0, 0), (0, LANES - N_EXPERTS - N_GROUPS)))
    b_router = jnp.pad(jnp.concatenate([b_router_expert[l], b_router_group[l]]),
                       (0, LANES - N_EXPERTS - N_GROUPS)).reshape(1, LANES)
    wg, wu, wd = w_exp_gate[l].astype(BF16), w_exp_up[l].astype(BF16), w_exp_down[l].astype(BF16)
    lru = lambda wa, wx: (conv_w[l], row(conv_b[l]), wa, row(lru_ba[l]), wx, row(lru_bx[l]), row(lru_lambda[l]))

    def tail(x, attn_o, rnn_o, w_out_l, tm_f, tm_m):
        h, xn, comb = _finish(x, attn_o, rnn_o, w_out_l, row(norm_ffn[l]), w_router, b_router, tm_f)
        return _moe(xn, comb, h, wg, wu, wd, final_gain, tm_m)

    np_ = bp * sp
    q, kc, ks, kw, xr, xg, gt = _project(xp.reshape(np_, D_MODEL), g_mix, w_proj, 256)
    shp = lambda a: a.reshape(bp, sp, a.shape[-1])
    ident = jnp.arange(np_ // 1024, dtype=jnp.int32)
    cmp_p = _compress(ident, kc.reshape(np_ // 1024, 1024, KV_W), wfull, bp, sp // 1024, 1, 1024)
    attn_p = _nsa_prompt(slopes, shp(q), shp(gt), cmp_p, shp(ks), shp(kw))
    rnn_p, h_p = _rglru(shp(xr), shp(xg), jnp.zeros((bp, 8, D_RNN), F32), jnp.zeros((bp, 1, D_RNN), F32),
                       *lru(wa_f32.astype(BF16), wx_f32.astype(BF16)), tc=512)
    y_p = tail(xp.reshape(np_, D_MODEL), attn_p.reshape(np_, D_ATTN), rnn_p.reshape(np_, D_RNN), w_o, 256, 1024)
    kv6 = lambda a, b_, t_: a.reshape(b_, t_, N_KV, 2, HEAD_DIM)
    outs_p = (y_p.reshape(bp, sp, D_MODEL), kv6(kc, bp, sp), kv6(ks, bp, sp),
              kv6(kw, bp, sp)[:, -min(WINDOW, sp):], shp(xr)[:, sp - (CONV_W - 1):], h_p.reshape(bp, D_RNN))

    ns_ = bs * ts
    q, kc, ks, kw, xr, xg, gt = _project_precise(xs.reshape(ns_, D_MODEL), g_mix, w_proj_f32)
    shs = lambda a: a.reshape(bs, ts, a.shape[-1])
    cmp_s = _compress_paged(page_table.reshape(-1), cache_cmp_kv[l].reshape(-1, HEAD_DIM), w2, bs, n_pages, 16)
    idx, o_cw = _nsa_sample_a(past, shs(q), shs(gt), cmp_s, cache_win_kv[l].reshape(-1, HEAD_DIM), shs(kw))
    idx = idx.reshape(bs, N_KV, ts, LANES)[..., :N_GATHER].reshape(-1)
    gate_sel = gt.reshape(ns_, N_KV, LANES)[:, :, :GQA * 3].reshape(ns_, N_HEADS, 3)[:, :, 1:2]
    gate_sel = jnp.broadcast_to(gate_sel, (ns_, N_HEADS, HEAD_DIM))
    slopes8 = jnp.broadcast_to(slopes.reshape(N_HEADS, 1), (N_HEADS, LANES))
    new_blocks = jnp.pad(ks.reshape(bs, ts * KV_ROWS, HEAD_DIM), ((0, 0), (0, BLOCK_ROWS - ts * KV_ROWS), (0, 0)))
    attn_s = _nsa_sample_b(past, idx, page_table.reshape(-1), q.reshape(ns_, N_HEADS, HEAD_DIM), gate_sel, slopes8,
                           o_cw.reshape(ns_, N_HEADS, HEAD_DIM), new_blocks, cache_sel_kv[l].reshape(-1, HEAD_DIM))
    conv_rows = jnp.pad(state_conv[l], ((0, 0), (ts - (CONV_W - 1), 0), (0, 0))).reshape(ns_, D_RNN)
    rnn_s, h_all = _rglru_short(xr, xg, conv_rows, jnp.repeat(state_h[l], ts, axis=0), *lru(wa_f32, wx_f32), t=ts)
    h_s = h_all.reshape(bs, ts, D_RNN)[:, -1]
    y_s = tail(xs.reshape(ns_, D_MODEL), attn_s.reshape(ns_, D_ATTN), rnn_s, w_out[l], ns_, ns_)
    win_s = jnp.concatenate([cache_win_kv[l], kv6(kw, bs, ts)], axis=1)[:, ts:]
    conv_s = jnp.concatenate([state_conv[l], shs(xr)], axis=1)[:, ts:]
    outs_s = (y_s.reshape(bs, ts, D_MODEL), kv6(kc, bs, ts), kv6(ks, bs, ts),
              win_s, conv_s, h_s.reshape(bs, D_RNN))
    return outs_p, outs_s


def kernel(x_prompt, x_sample, cache_cmp_kv, cache_sel_kv, cache_win_kv, state_conv, state_h, page_table, norm_mix, w_in, cmp_pool_w, conv_w, conv_b, lru_wa, lru_ba, lru_wx, lru_bx, lru_lambda, w_out, norm_ffn, w_router_group, b_router_group, w_router_expert, b_router_expert, w_exp_gate, w_exp_up, w_exp_down, norm_final):
    depth = w_in.shape[0]
    assert depth == 1, "the final norm is fused into the single layer's expert kernel"
    p, s = _layer(0, x_prompt, x_sample, cache_cmp_kv, cache_sel_kv, cache_win_kv, state_conv, state_h, page_table,
                  norm_mix, w_in, cmp_pool_w, conv_w, conv_b, lru_wa, lru_ba, lru_wx, lru_bx, lru_lambda, w_out,
                  norm_ffn, w_router_group, b_router_group, w_router_expert, b_router_expert,
                  w_exp_gate, w_exp_up, w_exp_down, norm_final.reshape(1, D_MODEL))
    st = lambda a: a[None]
    return (p[0], s[0], st(p[1]), st(s[1]), st(p[2]), st(s[2]), st(p[3]), st(s[3]),
            st(p[4]), st(s[4]), st(p[5]), st(s[5]))
```

```python
import functools

import jax
import jax.numpy as jnp
from jax import lax
from jax.experimental import pallas as pl
from jax.experimental.pallas import tpu as pltpu

F32 = jnp.float32
BF16 = jnp.bfloat16

D_MODEL = 2048
D_ATTN = 1024
D_RNN = 1024
N_HEADS = 8
HEAD_DIM = 128
N_KV = 2
GQA = 4
KV_W = 512
CMP_BLOCK = 32
SEL_BLOCK = 64
TOP_N = 16
WINDOW = 512
FORCE_SCORE = 1e4
RNN_BLOCKS = 16
RNN_BLOCK_DIM = 64
CONV_W = 4
LRU_C = 8.0
N_GROUPS = 4
EXP_PER_GROUP = 4
N_EXPERTS = 16
D_EXPERT = 512
RMS_EPS = 1e-6
PAGE_SIZE = 128

GROUP_LANE = N_EXPERTS
LANES = 128
NEG = -1e30
VMEM_LIMIT = 56 * 1024 * 1024

_SEG_Q = (0, 1024)
_SEG_KC = (1024, 1536)
_SEG_KS = (1536, 2048)
_SEG_KW = (2048, 2560)
_SEG_XR = (2560, 3584)
_SEG_XG = (3584, 4608)
_SEG_GT = (4608, 4864)
_PROJ_W = 4864


def _dot(a, b):
    return jnp.dot(a, b, preferred_element_type=F32)


def _dot_nt(a, b):
    return lax.dot_general(a, b, (((1,), (1,)), ((), ())), preferred_element_type=F32)


def _rms(x, g):
    return x * lax.rsqrt(jnp.mean(x * x, axis=-1, keepdims=True) + RMS_EPS) * g


def _split3(x):
    h1 = x.astype(BF16)
    r1 = x - h1.astype(F32)
    h2 = r1.astype(BF16)
    h3 = (r1 - h2.astype(F32)).astype(BF16)
    return h1, h2, h3


def _dot_f32(a, b, dot=_dot):
    a1, a2, a3 = _split3(a)
    b1, b2, b3 = _split3(b)
    return (dot(a3, b1) + dot(a2, b2) + dot(a1, b3)) + (dot(a2, b1) + dot(a1, b2)) + dot(a1, b1)


def _dot_x3(a, b, dot=_dot):
    a1, a2, _ = _split3(a)
    b1, b2, _ = _split3(b)
    return (dot(a2, b1) + dot(a1, b2)) + dot(a1, b1)


def _mm(a, w):
    if w.dtype == BF16:
        return _dot(a.astype(BF16), w)
    return _dot_x3(a, w)


def _dot_sel(a, m01):
    a1, a2, a3 = _split3(a)
    return (_dot(a3, m01) + _dot(a2, m01)) + _dot(a1, m01)


def _proj_kernel(x_ref, g_ref, w_ref, q_ref, kc_ref, ks_ref, kw_ref, xr_ref, xg_ref, gt_ref):
    xn = _rms(x_ref[...], g_ref[...]).astype(BF16)
    outs = ((q_ref, _SEG_Q), (kc_ref, _SEG_KC), (ks_ref, _SEG_KS), (kw_ref, _SEG_KW),
            (xr_ref, _SEG_XR), (xg_ref, _SEG_XG), (gt_ref, _SEG_GT))
    for ref, (lo, hi) in outs:
        ref[...] = _dot(xn, w_ref[:, lo:hi])


def _project(x, g, w, tm):
    n = x.shape[0]
    widths = [hi - lo for lo, hi in (_SEG_Q, _SEG_KC, _SEG_KS, _SEG_KW, _SEG_XR, _SEG_XG, _SEG_GT)]
    return pl.pallas_call(
        _proj_kernel,
        grid=(n // tm,),
        in_specs=[pl.BlockSpec((tm, D_MODEL), lambda i: (i, 0)),
                  pl.BlockSpec((1, D_MODEL), lambda i: (0, 0)),
                  pl.BlockSpec((D_MODEL, _PROJ_W), lambda i: (0, 0), pipeline_mode=pl.Buffered(1))],
        out_specs=[pl.BlockSpec((tm, w_), lambda i: (i, 0)) for w_ in widths],
        out_shape=[jax.ShapeDtypeStruct((n, w_), F32) for w_ in widths],
        compiler_params=pltpu.CompilerParams(dimension_semantics=("arbitrary",), vmem_limit_bytes=VMEM_LIMIT),
        name="proj",
    )(x, g, w)


def _proj_precise_kernel(x_ref, g_ref, w_ref, z_ref, xn_scr):
    @pl.when(pl.program_id(0) == 0)
    def _():
        xn_scr[...] = _rms(x_ref[...], g_ref[...])

    z_ref[...] = _dot_x3(xn_scr[...], w_ref[...])


def _project_precise(x, g, w_f32):
    n = x.shape[0]
    tn = 2 * LANES
    z = pl.pallas_call(
        _proj_precise_kernel,
        grid=(_PROJ_W // tn,),
        in_specs=[pl.BlockSpec((n, D_MODEL), lambda j: (0, 0)),
                  pl.BlockSpec((1, D_MODEL), lambda j: (0, 0)),
                  pl.BlockSpec((D_MODEL, tn), lambda j: (0, j))],
        out_specs=pl.BlockSpec((n, tn), lambda j: (0, j)),
        out_shape=jax.ShapeDtypeStruct((n, _PROJ_W), F32),
        scratch_shapes=[pltpu.VMEM((n, D_MODEL), F32)],
        compiler_params=pltpu.CompilerParams(dimension_semantics=("arbitrary",), vmem_limit_bytes=VMEM_LIMIT),
        name="proj_precise",
    )(x, g, w_f32)
    return [z[:, lo:hi] for lo, hi in (_SEG_Q, _SEG_KC, _SEG_KS, _SEG_KW, _SEG_XR, _SEG_XG, _SEG_GT)]


def _compress_kernel(n_pages, pt_ref, *refs):
    del pt_ref
    pages, w_ref, o_ref = refs[:n_pages], refs[n_pages], refs[n_pages + 1]
    if n_pages == 1:
        x = pages[0][0]
    else:
        x = jnp.concatenate([p[0] for p in pages], axis=0)
    rows = x.shape[0]
    xb = x.reshape(rows // CMP_BLOCK, CMP_BLOCK, KV_W) * w_ref[...][None]
    o_ref[0] = jnp.sum(xb, axis=1)


def _compress(table, src, wfull, n_batch, steps, n_pages, page_rows):
    out_rows = n_pages * page_rows // CMP_BLOCK
    per_b = steps * n_pages

    def page_spec(k):
        return pl.BlockSpec((1, page_rows, KV_W), lambda b, j, pt: (pt[b * per_b + j * n_pages + k], 0, 0))

    return pl.pallas_call(
        functools.partial(_compress_kernel, n_pages),
        grid_spec=pltpu.PrefetchScalarGridSpec(
            num_scalar_prefetch=1,
            grid=(n_batch, steps),
            in_specs=[page_spec(k) for k in range(n_pages)] + [pl.BlockSpec((CMP_BLOCK, KV_W), lambda b, j, pt: (0, 0))],
            out_specs=pl.BlockSpec((1, out_rows, KV_W), lambda b, j, pt: (b, j, 0)),
        ),
        out_shape=jax.ShapeDtypeStruct((n_batch, steps * out_rows, KV_W), F32),
        compiler_params=pltpu.CompilerParams(dimension_semantics=("arbitrary", "arbitrary"), vmem_limit_bytes=VMEM_LIMIT),
        name="compress",
    )(table, *([src] * n_pages), wfull)


def _softplus(x):
    return jnp.maximum(x, 0.0) + jnp.log1p(jnp.exp(-jnp.abs(x)))


def _rglru_kernel(tc, xr_ref, xg_ref, cs_ref, h0_ref, cw_ref, cb_ref, wa_ref, ba_ref, wx_ref, bx_ref, lam_ref,
                  o_ref, hl_ref, tail_scr, h_scr):
    @pl.when(pl.program_id(2) == 0)
    def _():
        tail_scr[...] = cs_ref[0]
        h_scr[...] = h0_ref[0]

    x = xr_ref[0]
    xp = jnp.concatenate([tail_scr[...], x], axis=0)
    w = cw_ref[...]
    xc = cb_ref[...] + pltpu.roll(xp, 3, axis=0)[8:] * w[0:1]
    xc = xc + pltpu.roll(xp, 2, axis=0)[8:] * w[1:2]
    xc = xc + pltpu.roll(xp, 1, axis=0)[8:] * w[2:3]
    xc = xc + x * w[3:4]
    tail_scr[...] = xp[tc:]

    r = jax.nn.sigmoid(_mm(xc, wa_ref[0]) + ba_ref[...])
    gi = jax.nn.sigmoid(_mm(xc, wx_ref[0]) + bx_ref[...])
    log_a = -LRU_C * r * _softplus(-lam_ref[...])
    a = jnp.exp(log_a)
    u = jnp.sqrt(-jnp.tanh(log_a) * (a * a + 1.0)) * (gi * xc)

    row = lax.broadcasted_iota(jnp.int32, a.shape, 0)
    s = 1
    while s < tc:
        keep = row >= s
        a_sh = jnp.where(keep, pltpu.roll(a, s, axis=0), 1.0)
        u_sh = jnp.where(keep, pltpu.roll(u, s, axis=0), 0.0)
        u = a * u_sh + u
        a = a * a_sh
        s *= 2
    h = a * h_scr[...] + u
    h_last = h[tc - 1:tc]
    h_scr[...] = h_last
    hl_ref[0] = h_last
    o_ref[0] = h * jax.nn.gelu(xg_ref[0])


def _rglru(xr, xg, conv_state8, h0, conv_w, conv_b, wa_t, ba, wx_t, bx, lam, tc):
    b, t, _ = xr.shape
    ct = 256
    n_ct = D_RNN // ct
    seq = lambda bi, c, j: (bi, j, c)
    per_b = lambda bi, c, j: (bi, 0, c)
    par = lambda bi, c, j: (0, c)
    return pl.pallas_call(
        functools.partial(_rglru_kernel, tc),
        grid=(b, n_ct, t // tc),
        in_specs=[pl.BlockSpec((1, tc, ct), seq), pl.BlockSpec((1, tc, ct), seq),
                  pl.BlockSpec((1, 8, ct), per_b), pl.BlockSpec((1, 1, ct), per_b),
                  pl.BlockSpec((CONV_W, ct), par), pl.BlockSpec((1, ct), par),
                  pl.BlockSpec((1, ct, ct), lambda bi, c, j: (c, 0, 0)), pl.BlockSpec((1, ct), par),
                  pl.BlockSpec((1, ct, ct), lambda bi, c, j: (c, 0, 0)), pl.BlockSpec((1, ct), par),
                  pl.BlockSpec((1, ct), par)],
        out_specs=[pl.BlockSpec((1, tc, ct), seq), pl.BlockSpec((1, 1, ct), per_b)],
        out_shape=[jax.ShapeDtypeStruct((b, t, D_RNN), F32), jax.ShapeDtypeStruct((b, 1, D_RNN), F32)],
        scratch_shapes=[pltpu.VMEM((8, ct), F32), pltpu.VMEM((1, ct), F32)],
        compiler_params=pltpu.CompilerParams(dimension_semantics=("arbitrary", "arbitrary", "arbitrary"),
                                             vmem_limit_bytes=VMEM_LIMIT),
        name="rglru",
    )(xr, xg, conv_state8, h0, conv_w, conv_b, wa_t, ba, wx_t, bx, lam)


def _rglru_short_kernel(t, xr_ref, xg_ref, cs_ref, h0_ref, cw_ref, cb_ref, wa_ref, ba_ref, wx_ref, bx_ref, lam_ref,
                        o_ref, h_ref):
    x = xr_ref[...]
    st = cs_ref[...]
    n = x.shape[0]
    step = lax.broadcasted_iota(jnp.int32, x.shape, 0) % t
    w = cw_ref[...]
    xc = cb_ref[...] + x * w[CONV_W - 1:CONV_W]
    for sh in range(1, CONV_W):
        prev = jnp.where(step >= sh, pltpu.roll(x, sh, axis=0), pltpu.roll(st, n - (t - sh), axis=0))
        xc = xc + prev * w[CONV_W - 1 - sh:CONV_W - sh]
    r = jax.nn.sigmoid(_mm(xc, wa_ref[0]) + ba_ref[...])
    gi = jax.nn.sigmoid(_mm(xc, wx_ref[0]) + bx_ref[...])
    log_a = -LRU_C * r * _softplus(-lam_ref[...])
    a = jnp.exp(log_a)
    u = jnp.sqrt(-jnp.tanh(log_a) * (a * a + 1.0)) * (gi * xc)
    s = 1
    while s < t:
        keep = step >= s
        a_sh = jnp.where(keep, pltpu.roll(a, s, axis=0), 1.0)
        u_sh = jnp.where(keep, pltpu.roll(u, s, axis=0), 0.0)
        u = a * u_sh + u
        a = a * a_sh
        s *= 2
    h = a * h0_ref[...] + u
    h_ref[...] = h
    o_ref[...] = h * jax.nn.gelu(xg_ref[...])


def _rglru_short(xr, xg, conv_rows, h0_rows, conv_w, conv_b, wa_t, ba, wx_t, bx, lam, t):
    n = xr.shape[0]
    ct = 256
    blk = pl.BlockSpec((n, ct), lambda c: (0, c))
    par = lambda c: (0, c)
    tile = pl.BlockSpec((1, ct, ct), lambda c: (c, 0, 0))
    return pl.pallas_call(
        functools.partial(_rglru_short_kernel, t),
        grid=(D_RNN // ct,),
        in_specs=[blk, blk, blk, blk, pl.BlockSpec((CONV_W, ct), par), pl.BlockSpec((1, ct), par),
                  tile, pl.BlockSpec((1, ct), par), tile, pl.BlockSpec((1, ct), par), pl.BlockSpec((1, ct), par)],
        out_specs=[blk, blk],
        out_shape=[jax.ShapeDtypeStruct((n, D_RNN), F32), jax.ShapeDtypeStruct((n, D_RNN), F32)],
        compiler_params=pltpu.CompilerParams(dimension_semantics=("arbitrary",), vmem_limit_bytes=VMEM_LIMIT),
        name="rglru_short",
    )(xr, xg, conv_rows, h0_rows, conv_w, conv_b, wa_t, ba, wx_t, bx, lam)


TQ = 128
W4 = GQA * TQ
WIN_KEYS = WINDOW + TQ


def _nsa_prompt_kernel(slopes_ref, q_ref, gt_ref, kc_ref, vc_ref, ks_ref, vs_ref, kw_ref, vw_ref, o_ref,
                       q_scr, kcb, vct, ksb, vst, kwb, vwt, bias0, caus, wlow,
                       impt_scr, selb_scr, m_scr, l_scr, acc_scr, out_scr):
    g = pl.program_id(1)
    i = pl.program_id(2)
    s_len = ks_ref.shape[1]
    n_cmp = kc_ref.shape[1]
    n_sel = n_cmp // 2
    lane = lax.broadcasted_iota(jnp.int32, (1, W4), 1)
    tl_row = (lane % TQ).astype(F32)
    slope_row = jnp.full((1, W4), slopes_ref[g * GQA + GQA - 1], F32)
    for r in reversed(range(GQA - 1)):
        slope_row = jnp.where(lane < (r + 1) * TQ, slopes_ref[g * GQA + r], slope_row)

    @pl.when(i == 0)
    def _prepare():
        kcb[...] = kc_ref[0].astype(BF16)
        vct[...] = vc_ref[0].T.astype(BF16)

        def cast(j, carry):
            off = pl.multiple_of(j * LANES, LANES)
            ksb[pl.ds(off, LANES), :] = ks_ref[0, pl.ds(off, LANES), :].astype(BF16)
            kwb[pl.ds(off, LANES), :] = kw_ref[0, pl.ds(off, LANES), :].astype(BF16)
            vst[:, pl.ds(off, LANES)] = vs_ref[0, pl.ds(off, LANES), :].T.astype(BF16)
            vwt[:, pl.ds(off, LANES)] = vw_ref[0, pl.ds(off, LANES), :].T.astype(BF16)
            return carry

        lax.fori_loop(0, s_len // LANES, cast, 0)
        rel = tl_row - lax.broadcasted_iota(jnp.int32, (WIN_KEYS, W4), 0).astype(F32)
        bias0[...] = slope_row * rel
        caus[...] = jnp.where(rel[0:LANES] >= 0, 0.0, NEG)
        wlow[...] = jnp.where(rel[0:LANES] <= 0, 0.0, NEG)

    start_f = (i * TQ).astype(F32)
    scale = HEAD_DIM ** -0.5
    for r in range(GQA):
        q_scr[r * TQ:(r + 1) * TQ, :] = (q_ref[0, :, r * HEAD_DIM:(r + 1) * HEAD_DIM] * scale).astype(BF16)
    gate_t = jax.nn.sigmoid(gt_ref[0]).T

    def gate_row(branch):
        return jnp.concatenate([gate_t[3 * r + branch:3 * r + branch + 1, :] for r in range(GQA)], axis=1)

    t_row = start_f + tl_row
    c_end = ((lax.broadcasted_iota(jnp.int32, (n_cmp, W4), 0) + 1) * CMP_BLOCK - 1).astype(F32)
    dist_c = t_row - c_end
    ok_c = dist_c >= 0
    x = jnp.where(ok_c, _dot_nt(kcb[...], q_scr[...]) - slope_row * dist_c, NEG)
    e = jnp.where(ok_c, jnp.exp(x - jnp.max(x, axis=0, keepdims=True)), 0.0)
    p = e * (1.0 / jnp.maximum(jnp.sum(e, axis=0, keepdims=True), 1e-30))
    out_scr[...] = gate_row(0) * _dot(vct[...], p.astype(BF16))

    p_heads = p[:, 0:TQ]
    for r in range(1, GQA):
        p_heads = p_heads + p[:, r * TQ:(r + 1) * TQ]
    impt_scr[...] = p_heads
    imp = impt_scr[pl.ds(0, n_sel, stride=2), :] + impt_scr[pl.ds(1, n_sel, stride=2), :]
    blk = lax.broadcasted_iota(jnp.int32, (n_sel, TQ), 0)
    cur = (i * TQ + lax.broadcasted_iota(jnp.int32, (n_sel, TQ), 1)) // SEL_BLOCK
    forced = (blk == 0) | (blk == cur) | (blk == cur - 1)
    imp = jnp.where(forced, FORCE_SCORE, imp)
    imp = jnp.where(blk > cur, NEG, imp)
    rank = jnp.zeros((n_sel, TQ), jnp.int32)
    for j in range(n_sel):
        row = imp[j:j + 1, :]
        tie = jnp.where(blk > j, jnp.where(row == imp, 1, 0), 0)
        rank = rank + jnp.where(row > imp, 1, tie)
    selb = jnp.where(rank < TOP_N, 0.0, NEG)
    for j in range(n_sel):
        selb_scr[j] = selb[j:j + 1, :]

    def scores(k_b, c, nk):
        off = pl.multiple_of(c * LANES, LANES)
        return _dot_nt(k_b[pl.ds(off, nk), :], q_scr[...]) - bias0[0:nk, :]

    def block_mask(c, nk):
        rows = [jnp.broadcast_to(selb_scr[2 * c + k], (SEL_BLOCK, TQ)) for k in range(nk // SEL_BLOCK)]
        return jnp.concatenate([jnp.concatenate(rows, axis=0)] * GQA, axis=1)

    def update(v_t, c, nk, x):
        off = pl.multiple_of(c * LANES, LANES)
        r_c = slope_row * ((c * LANES).astype(F32) - start_f)
        m_old = m_scr[...]
        m_new = jnp.maximum(m_old, jnp.max(x, axis=0, keepdims=True) + r_c)
        pr = jnp.exp(x + (r_c - m_new))
        alpha = jnp.exp(m_old - m_new)
        l_scr[...] = alpha * l_scr[...] + jnp.sum(pr, axis=0, keepdims=True)
        acc_scr[...] = alpha * acc_scr[...] + _dot(v_t[:, pl.ds(off, nk)], pr.astype(BF16))
        m_scr[...] = m_new

    def reset():
        m_scr[...] = jnp.full((1, W4), NEG, F32)
        l_scr[...] = jnp.zeros((1, W4), F32)
        acc_scr[...] = jnp.zeros((HEAD_DIM, W4), F32)

    def result():
        return acc_scr[...] * (1.0 / jnp.maximum(l_scr[...], 1e-30))

    reset()
    n_big = i // 4

    def big(c4, carry):
        update(vst, 4 * c4, 4 * LANES, scores(ksb, 4 * c4, 4 * LANES) + block_mask(4 * c4, 4 * LANES))
        return carry

    lax.fori_loop(0, n_big, big, 0)
    rem = i - 4 * n_big

    @pl.when(rem >= 2)
    def _():
        update(vst, 4 * n_big, 2 * LANES, scores(ksb, 4 * n_big, 2 * LANES) + block_mask(4 * n_big, 2 * LANES))

    @pl.when(rem % 2 == 1)
    def _():
        update(vst, i - 1, LANES, scores(ksb, i - 1, LANES) + block_mask(i - 1, LANES))

    update(vst, i, LANES, scores(ksb, i, LANES) + block_mask(i, LANES) + caus[...])
    out_scr[...] += gate_row(1) * result()

    n_back = WINDOW // LANES

    @pl.when(i >= n_back)
    def _():
        c = i - n_back
        x = scores(kwb, c, WIN_KEYS)
        x = jnp.concatenate([x[0:LANES] + wlow[...], x[LANES:WINDOW], x[WINDOW:WIN_KEYS] + caus[...]], axis=0)
        pr = jnp.exp(x - jnp.max(x, axis=0, keepdims=True))
        o_w = _dot(vwt[:, pl.ds(pl.multiple_of(c * LANES, LANES), WIN_KEYS)], pr.astype(BF16))
        out_scr[...] += gate_row(2) * (o_w * (1.0 / jnp.maximum(jnp.sum(pr, axis=0, keepdims=True), 1e-30)))

    @pl.when(i < n_back)
    def _():
        reset()

        def body(c, carry):
            update(vwt, c, LANES, scores(kwb, c, LANES))
            return carry

        lax.fori_loop(0, i, body, 0)
        update(vwt, i, LANES, scores(kwb, i, LANES) + caus[...])
        out_scr[...] += gate_row(2) * result()

    o_t = out_scr[...]
    for r in range(GQA):
        o_ref[0, :, r * HEAD_DIM:(r + 1) * HEAD_DIM] = o_t[:, r * TQ:(r + 1) * TQ].T


def _nsa_prompt(slopes, q, gt, cmp_kv, ks, kw):
    b, s, _ = q.shape
    n_cmp = cmp_kv.shape[1]
    k_of = lambda bi, g, i: (bi, 0, 2 * g)
    v_of = lambda bi, g, i: (bi, 0, 2 * g + 1)
    tile = pltpu.VMEM((LANES, W4), F32)
    return pl.pallas_call(
        _nsa_prompt_kernel,
        grid=(b, N_KV, s // TQ),
        in_specs=[pl.BlockSpec(memory_space=pltpu.SMEM),
                  pl.BlockSpec((1, TQ, GQA * HEAD_DIM), lambda bi, g, i: (bi, i, g)),
                  pl.BlockSpec((1, TQ, LANES), lambda bi, g, i: (bi, i, g)),
                  pl.BlockSpec((1, n_cmp, HEAD_DIM), k_of), pl.BlockSpec((1, n_cmp, HEAD_DIM), v_of),
                  pl.BlockSpec((1, s, HEAD_DIM), k_of), pl.BlockSpec((1, s, HEAD_DIM), v_of),
                  pl.BlockSpec((1, s, HEAD_DIM), k_of), pl.BlockSpec((1, s, HEAD_DIM), v_of)],
        out_specs=pl.BlockSpec((1, TQ, GQA * HEAD_DIM), lambda bi, g, i: (bi, i, g)),
        out_shape=jax.ShapeDtypeStruct((b, s, D_ATTN), F32),
        scratch_shapes=[pltpu.VMEM((W4, HEAD_DIM), BF16),
                        pltpu.VMEM((n_cmp, HEAD_DIM), BF16), pltpu.VMEM((HEAD_DIM, n_cmp), BF16),
                        pltpu.VMEM((s, HEAD_DIM), BF16), pltpu.VMEM((HEAD_DIM, s), BF16),
                        pltpu.VMEM((s, HEAD_DIM), BF16), pltpu.VMEM((HEAD_DIM, s), BF16),
                        pltpu.VMEM((WIN_KEYS, W4), F32), tile, tile,
                        pltpu.VMEM((n_cmp, TQ), F32),
                        pltpu.VMEM((n_cmp // 2, 1, TQ), F32),
                        pltpu.VMEM((1, W4), F32), pltpu.VMEM((1, W4), F32),
                        pltpu.VMEM((HEAD_DIM, W4), F32), pltpu.VMEM((HEAD_DIM, W4), F32)],
        compiler_params=pltpu.CompilerParams(dimension_semantics=("arbitrary", "arbitrary", "arbitrary"),
                                             vmem_limit_bytes=VMEM_LIMIT),
        name="nsa_prompt",
    )(slopes, q, gt, cmp_kv, cmp_kv, ks, ks, kw, kw)


KV_ROWS = 2 * N_KV


def _compress_paged_kernel(n_pages, pt_ref, *refs):
    del pt_ref
    pages, w_ref, o_ref = refs[:n_pages], refs[n_pages], refs[n_pages + 1]
    for gc in range(KV_ROWS):
        x = jnp.concatenate([p[pl.ds(gc, PAGE_SIZE, stride=KV_ROWS), :] for p in pages], axis=0)
        xb = x.reshape(n_pages * PAGE_SIZE // CMP_BLOCK, CMP_BLOCK, HEAD_DIM) * w_ref[gc % 2][None]
        o_ref[0, gc] = jnp.sum(xb, axis=1)


def _compress_paged(table, pool2d, w2, n_batch, pages_per_b, n_pages):
    steps = pages_per_b // n_pages
    out_rows = n_pages * PAGE_SIZE // CMP_BLOCK
    rows = PAGE_SIZE * KV_ROWS

    def page_spec(k):
        return pl.BlockSpec((rows, HEAD_DIM), lambda b, j, pt: (pt[b * pages_per_b + j * n_pages + k], 0))

    return pl.pallas_call(
        functools.partial(_compress_paged_kernel, n_pages),
        grid_spec=pltpu.PrefetchScalarGridSpec(
            num_scalar_prefetch=1,
            grid=(n_batch, steps),
            in_specs=[page_spec(k) for k in range(n_pages)]
            + [pl.BlockSpec((2, CMP_BLOCK, HEAD_DIM), lambda b, j, pt: (0, 0, 0))],
            out_specs=pl.BlockSpec((1, KV_ROWS, out_rows, HEAD_DIM), lambda b, j, pt: (b, 0, j, 0)),
        ),
        out_shape=jax.ShapeDtypeStruct((n_batch, KV_ROWS, steps * out_rows, HEAD_DIM), F32),
        compiler_params=pltpu.CompilerParams(dimension_semantics=("arbitrary", "arbitrary"), vmem_limit_bytes=VMEM_LIMIT),
        name="compress_paged",
    )(table, *([pool2d] * n_pages), w2)


def _nsa_sample_a_kernel(past, q_ref, gt_ref, cmp_ref, cw_ref, kwn_ref, idx_ref, o_ref):
    t_new = q_ref.shape[1]
    n_cmp = cmp_ref.shape[2]
    n_past_sel = n_cmp // 2
    n_win = cw_ref.shape[0] // KV_ROWS
    rows = GQA * t_new
    gates = jax.nn.sigmoid(gt_ref[0])
    t_row = past + lax.broadcasted_iota(jnp.int32, (rows, 1), 0) % t_new
    scale = HEAD_DIM ** -0.5
    pair = jnp.where(lax.broadcasted_iota(jnp.int32, (n_cmp, n_past_sel), 0) // 2
                     == lax.broadcasted_iota(jnp.int32, (n_cmp, n_past_sel), 1), 1.0, 0.0).astype(BF16)
    imps = []
    for g in range(N_KV):
        qg = jnp.concatenate([q_ref[0, :, (g * GQA + r) * HEAD_DIM:(g * GQA + r + 1) * HEAD_DIM]
                              for r in range(GQA)], axis=0)
        qg = qg * scale
        slope = jnp.concatenate([jnp.full((t_new, 1), 2.0 ** -(g * GQA + r + 1), F32) for r in range(GQA)], axis=0)
        kcol = g * 2 * HEAD_DIM
        kc = cmp_ref[0, 2 * g]
        vc = cmp_ref[0, 2 * g + 1]
        c_end = (lax.broadcasted_iota(jnp.int32, (1, n_cmp), 1) + 1) * CMP_BLOCK - 1
        dist_c = t_row - c_end
        mask_c = dist_c >= 0
        s = _dot_x3(qg, kc, _dot_nt) - slope * dist_c.astype(F32)
        s = jnp.where(mask_c, s, NEG)
        m = jnp.max(s, axis=-1, keepdims=True)
        p = jnp.where(mask_c, jnp.exp(s - m), 0.0)
        p = p / jnp.maximum(jnp.sum(p, axis=-1, keepdims=True), 1e-30)
        o_c = _dot_x3(p, vc)
        p_heads = p[0:t_new]
        for r in range(1, GQA):
            p_heads = p_heads + p[r * t_new:(r + 1) * t_new]
        imps.append(_dot_sel(p_heads, pair))
        n_pad = LANES - t_new
        kw = jnp.concatenate([cw_ref[pl.ds(2 * g, n_win, stride=KV_ROWS), :], kwn_ref[0, :, kcol:kcol + HEAD_DIM],
                              jnp.zeros((n_pad, HEAD_DIM), F32)], axis=0)
        vw = jnp.concatenate([cw_ref[pl.ds(2 * g + 1, n_win, stride=KV_ROWS), :],
                              kwn_ref[0, :, kcol + HEAD_DIM:kcol + 2 * HEAD_DIM],
                              jnp.zeros((n_pad, HEAD_DIM), F32)], axis=0)
        win_pos = past - n_win + lax.broadcasted_iota(jnp.int32, (1, n_win + LANES), 1)
        dist_w = t_row - win_pos
        mask_w = (dist_w >= 0) & (dist_w <= WINDOW)
        s = _dot_x3(qg, kw, _dot_nt) - slope * dist_w.astype(F32)
        s = jnp.where(mask_w, s, NEG)
        m = jnp.max(s, axis=-1, keepdims=True)
        p = jnp.where(mask_w, jnp.exp(s - m), 0.0)
        p = p / jnp.maximum(jnp.sum(p, axis=-1, keepdims=True), 1e-30)
        o_w = _dot_x3(p, vw)
        for r in range(GQA):
            h = g * GQA + r
            gl = g * LANES + 3 * r
            o_ref[0, :, h * HEAD_DIM:(h + 1) * HEAD_DIM] = (
                gates[:, gl:gl + 1] * o_c[r * t_new:(r + 1) * t_new]
                + gates[:, gl + 2:gl + 3] * o_w[r * t_new:(r + 1) * t_new])

    imp = jnp.concatenate(imps, axis=0)
    n_rows = N_KV * t_new
    lane = lax.broadcasted_iota(jnp.int32, (n_rows, n_past_sel), 1)
    lane_f = lane.astype(F32)
    cur = (past + lax.broadcasted_iota(jnp.int32, (n_rows, n_past_sel), 0) % t_new) // SEL_BLOCK
    forced = (lane == 0) | (lane == cur) | (lane == cur - 1)
    imp = jnp.where(forced, FORCE_SCORE, imp)
    imp = jnp.where(lane > cur, NEG, imp)
    out_lane = lax.broadcasted_iota(jnp.int32, (n_rows, LANES), 1)
    idx = jnp.zeros((n_rows, LANES), F32)
    for k in range(TOP_N - 1):
        m = jnp.max(imp, axis=-1, keepdims=True)
        j = jnp.min(jnp.where(imp == m, lane_f, float(n_past_sel)), axis=-1, keepdims=True)
        idx = jnp.where(out_lane == k, j, idx)
        imp = jnp.where(lane_f == j, -3e38, imp)
    idx_ref[0] = idx.astype(jnp.int32)


def _nsa_sample_a(past, q, gt, cmp_kv, cache_w2d, kw_new):
    b, t_new, _ = q.shape
    win_rows = cache_w2d.shape[0] // b
    per_b3 = lambda bi: (bi, 0, 0)
    return pl.pallas_call(
        functools.partial(_nsa_sample_a_kernel, past),
        grid=(b,),
        in_specs=[pl.BlockSpec((1, t_new, D_ATTN), per_b3), pl.BlockSpec((1, t_new, 2 * LANES), per_b3),
                  pl.BlockSpec((1,) + cmp_kv.shape[1:], lambda bi: (bi, 0, 0, 0)),
                  pl.BlockSpec((win_rows, HEAD_DIM), lambda bi: (bi, 0)),
                  pl.BlockSpec((1, t_new, KV_W), per_b3)],
        out_specs=[pl.BlockSpec((1, N_KV * t_new, LANES), per_b3), pl.BlockSpec((1, t_new, D_ATTN), per_b3)],
        out_shape=[jax.ShapeDtypeStruct((b, N_KV * t_new, LANES), jnp.int32),
                   jax.ShapeDtypeStruct((b, t_new, D_ATTN), F32)],
        compiler_params=pltpu.CompilerParams(dimension_semantics=("arbitrary",), vmem_limit_bytes=VMEM_LIMIT),
        name="nsa_sample_a",
    )(q, gt, cmp_kv, cache_w2d, kw_new)


N_GATHER = TOP_N - 1
GATHER_KEYS = TOP_N * SEL_BLOCK


def _nsa_sample_b_kernel(past, t_new, idx_ref, pt_ref, q_ref, gs_ref, slope_ref, ocw_ref, new_ref, pool_ref,
                         o_ref, kv_buf, sems):
    n_steps = pl.num_programs(0) * t_new
    step = pl.program_id(0) * t_new + pl.program_id(1)
    slot = step % 2
    pages_per_b = past // PAGE_SIZE
    sel_per_page = PAGE_SIZE // SEL_BLOCK

    def block_index(st, g, k):
        return idx_ref[((st // t_new * N_KV + g) * t_new + st % t_new) * N_GATHER + k]

    def copies(st, sl):
        out = []
        for g in range(N_KV):
            for k in range(N_GATHER):
                lp = block_index(st, g, k)
                phys = pt_ref[st // t_new * pages_per_b + lp // sel_per_page] * sel_per_page + lp % sel_per_page
                out.append(pltpu.make_async_copy(
                    pool_ref.at[pl.ds(pl.multiple_of(phys * SEL_BLOCK, SEL_BLOCK), SEL_BLOCK), g],
                    kv_buf.at[sl, g, pl.ds(k * SEL_BLOCK, SEL_BLOCK)], sems.at[sl, g, k]))
            out.append(pltpu.make_async_copy(
                new_ref.at[st // t_new, g], kv_buf.at[sl, g, pl.ds(N_GATHER * SEL_BLOCK, SEL_BLOCK)],
                sems.at[sl, g, N_GATHER]))
        return out

    @pl.when(step == 0)
    def _():
        for cp in copies(step, slot):
            cp.start()

    @pl.when(step + 1 < n_steps)
    def _():
        for cp in copies(step + 1, 1 - slot):
            cp.start()

    t_pos = past + pl.program_id(1)
    q8 = q_ref[0] * HEAD_DIM ** -0.5
    slope = slope_ref[:, 0:1]
    lane = lax.broadcasted_iota(jnp.int32, (1, GATHER_KEYS), 1)
    for cp in copies(step, slot):
        cp.wait()
    outs = []
    for g in range(N_KV):
        pos = past + (lane - N_GATHER * SEL_BLOCK)
        for k in range(N_GATHER):
            pos = jnp.where(lane // SEL_BLOCK == k, block_index(step, g, k) * SEL_BLOCK + lane % SEL_BLOCK, pos)
        kk = kv_buf[slot, g, :, 0, :]
        vv = kv_buf[slot, g, :, 1, :]
        dist = t_pos - pos
        mask = dist >= 0
        s = _dot_x3(q8, kk, _dot_nt) - slope * dist.astype(F32)
        s = jnp.where(mask, s, NEG)
        m = jnp.max(s, axis=-1, keepdims=True)
        p = jnp.where(mask, jnp.exp(s - m), 0.0)
        p = p / jnp.maximum(jnp.sum(p, axis=-1, keepdims=True), 1e-30)
        outs.append(_dot_x3(p, vv))
    head = lax.broadcasted_iota(jnp.int32, (N_HEADS, HEAD_DIM), 0)
    o_s = jnp.where(head < GQA, outs[0], outs[1])
    o_ref[0] = ocw_ref[0] + jax.nn.sigmoid(gs_ref[0]) * o_s


def _nsa_sample_b(past, idx, page_table, q, gate_sel, slopes8, o_cw, new_blocks, pool2d):
    n_tok = q.shape[0]
    b = new_blocks.shape[0]
    t_new = n_tok // b
    tok = lambda bi, ti, *_: (bi * t_new + ti, 0, 0)
    return pl.pallas_call(
        functools.partial(_nsa_sample_b_kernel, past, t_new),
        grid_spec=pltpu.PrefetchScalarGridSpec(
            num_scalar_prefetch=2,
            grid=(b, t_new),
            in_specs=[pl.BlockSpec((1, N_HEADS, HEAD_DIM), tok), pl.BlockSpec((1, N_HEADS, HEAD_DIM), tok),
                      pl.BlockSpec((N_HEADS, LANES), lambda bi, ti, *_: (0, 0)),
                      pl.BlockSpec((1, N_HEADS, HEAD_DIM), tok),
                      pl.BlockSpec(memory_space=pl.ANY), pl.BlockSpec(memory_space=pl.ANY)],
            out_specs=pl.BlockSpec((1, N_HEADS, HEAD_DIM), tok),
            scratch_shapes=[pltpu.VMEM((2, N_KV, GATHER_KEYS, 2, HEAD_DIM), F32),
                            pltpu.SemaphoreType.DMA((2, N_KV, TOP_N))],
        ),
        out_shape=jax.ShapeDtypeStruct((n_tok, N_HEADS, HEAD_DIM), F32),
        compiler_params=pltpu.CompilerParams(dimension_semantics=("arbitrary", "arbitrary"),
                                             vmem_limit_bytes=VMEM_LIMIT),
        name="nsa_sample_b",
    )(idx, page_table, q, gate_sel, slopes8, o_cw, new_blocks, pool2d)


def _finish_kernel(x_ref, a_ref, r_ref, wo_ref, gn_ref, wr_ref, br_ref, h_ref, xn_ref, comb_ref):
    h = x_ref[...] + _mm(a_ref[...], wo_ref[0:D_ATTN, :]) + _mm(r_ref[...], wo_ref[D_ATTN:D_MODEL, :])
    h_ref[...] = h
    xn = _rms(h, gn_ref[...])
    xn_ref[...] = xn.astype(BF16)
    logit = _dot_f32(xn, wr_ref[...]) + br_ref[...]
    lane = lax.broadcasted_iota(jnp.int32, logit.shape, 1)
    lane_f = lane.astype(F32)
    is_g = (lane >= N_EXPERTS) & (lane < N_EXPERTS + N_GROUPS)
    gl = jnp.where(is_g, logit, NEG)
    g_max = jnp.max(gl, axis=-1, keepdims=True)
    g_star = jnp.min(jnp.where(gl == g_max, lane_f, 1e9), axis=-1, keepdims=True) - N_EXPERTS
    g_prob = 1.0 / jnp.sum(jnp.where(is_g, jnp.exp(gl - g_max), 0.0), axis=-1, keepdims=True)
    in_grp = (lane < N_EXPERTS) & ((lane // EXP_PER_GROUP).astype(F32) == g_star)
    el = jnp.where(in_grp, logit, NEG)
    e_max = jnp.max(el, axis=-1, keepdims=True)
    ee = jnp.where(in_grp, jnp.exp(el - e_max), 0.0)
    ep = jnp.where(in_grp, ee / jnp.sum(ee, axis=-1, keepdims=True), -1.0)
    p1 = jnp.max(ep, axis=-1, keepdims=True)
    i1 = jnp.min(jnp.where(ep == p1, lane_f, 1e9), axis=-1, keepdims=True)
    ep2 = jnp.where(lane_f == i1, -1.0, ep)
    p2 = jnp.max(ep2, axis=-1, keepdims=True)
    i2 = jnp.min(jnp.where(ep2 == p2, lane_f, 1e9), axis=-1, keepdims=True)
    tot = p1 + p2
    comb_ref[...] = (jnp.where(lane_f == i1, p1 / tot * g_prob, 0.0)
                     + jnp.where(lane_f == i2, p2 / tot * g_prob, 0.0)
                     + jnp.where(lane == GROUP_LANE, g_star, 0.0))


def _finish(x, attn_o, rnn_o, w_out, g_ffn, w_router, b_router, tm):
    n = x.shape[0]
    row = lambda i: (i, 0)
    fixed = lambda i: (0, 0)
    return pl.pallas_call(
        _finish_kernel,
        grid=(n // tm,),
        in_specs=[pl.BlockSpec((tm, D_MODEL), row), pl.BlockSpec((tm, D_ATTN), row), pl.BlockSpec((tm, D_RNN), row),
                  pl.BlockSpec((D_MODEL, D_MODEL), fixed, pipeline_mode=pl.Buffered(1)),
                  pl.BlockSpec((1, D_MODEL), fixed),
                  pl.BlockSpec((D_MODEL, LANES), fixed), pl.BlockSpec((1, LANES), fixed)],
        out_specs=[pl.BlockSpec((tm, D_MODEL), row), pl.BlockSpec((tm, D_MODEL), row), pl.BlockSpec((tm, LANES), row)],
        out_shape=[jax.ShapeDtypeStruct((n, D_MODEL), F32), jax.ShapeDtypeStruct((n, D_MODEL), BF16),
                   jax.ShapeDtypeStruct((n, LANES), F32)],
        compiler_params=pltpu.CompilerParams(dimension_semantics=("arbitrary",), vmem_limit_bytes=VMEM_LIMIT),
        name="finish",
    )(x, attn_o, rnn_o, w_out, g_ffn, w_router, b_router)


MOE_SUB = 128


MOE_SORT = 512


def _moe_kernel(sd, xn_ref, comb_ref, h_ref, wg_ref, wu_ref, wd_ref, gf_ref, y_ref, xs_scr, cs_scr, pt_scr, ends_smem):
    e = pl.program_id(1)
    tm = xn_ref.shape[0]
    domains = [slice(d * sd, (d + 1) * sd) for d in range(tm // sd)]

    @pl.when(e == 0)
    def _sort_rows():
        for d, dom in enumerate(domains):
            comb = comb_ref[dom, :]
            lane = lax.broadcasted_iota(jnp.int32, comb.shape, 1)
            grp = jnp.sum(jnp.where(lane == GROUP_LANE, comb, 0.0), axis=-1, keepdims=True)
            onehot = jnp.where(lane.astype(F32) == grp, 1.0, 0.0)
            earlier = jnp.where(lax.broadcasted_iota(jnp.int32, (sd, sd), 1)
                                < lax.broadcasted_iota(jnp.int32, (sd, sd), 0), 1.0, 0.0).astype(BF16)
            before = _dot(earlier, onehot.astype(BF16))
            rank = jnp.sum(onehot * before, axis=-1, keepdims=True)
            count = jnp.sum(onehot, axis=0, keepdims=True)
            end = jnp.int32(0)
            ends_smem[d, 0] = end
            for g in range(N_GROUPS):
                end = end + jnp.sum(jnp.where(lane[0:1] == g, count, 0.0)).astype(jnp.int32)
                ends_smem[d, g + 1] = end
            first = jnp.sum(jnp.where(lane.astype(F32) < grp, count, 0.0), axis=-1, keepdims=True)
            pos = first + rank
            to_sorted_t = jnp.where(lax.broadcasted_iota(jnp.int32, (sd, sd), 1).astype(F32) == pos, 1.0, 0.0)
            pt_scr[d] = to_sorted_t.astype(BF16)
            to_sorted = to_sorted_t.T.astype(BF16)
            xs_scr[dom, :] = _dot(to_sorted, xn_ref[dom, :]).astype(BF16)
            c1, c2, c3 = _split3(comb)
            cs_scr[dom, :] = (_dot(to_sorted, c3) + _dot(to_sorted, c2)) + _dot(to_sorted, c1)
        y_ref[...] = jnp.zeros(y_ref.shape, F32)

    lane = lax.broadcasted_iota(jnp.int32, (MOE_SUB, LANES), 1)
    grp_e = e // EXP_PER_GROUP
    for s in range(tm // MOE_SUB):
        rows = slice(s * MOE_SUB, (s + 1) * MOE_SUB)
        cw = jnp.sum(jnp.where(lane == e, cs_scr[rows, :], 0.0), axis=-1, keepdims=True)
        d, lo = divmod(s * MOE_SUB, sd)
        in_range = jnp.logical_and(ends_smem[d, grp_e] < lo + MOE_SUB, ends_smem[d, grp_e + 1] > lo)

        @pl.when(in_range)
        def _():
            x = xs_scr[rows, :]
            hh = jax.nn.silu(_dot(x, wg_ref[0])) * _dot(x, wu_ref[0]) * cw
            y_ref[rows, :] += _dot(hh.astype(BF16), wd_ref[0])

    @pl.when(e == pl.num_programs(1) - 1)
    def _():
        for d, dom in enumerate(domains):
            a1, a2, _ = _split3(y_ref[dom, :])
            ffn = _dot(pt_scr[d], a2) + _dot(pt_scr[d], a1)
            y_ref[dom, :] = _rms(h_ref[dom, :] + ffn, gf_ref[...])


def _moe(xn, comb, h, w_gate, w_up, w_down, g_final, tm):
    n = xn.shape[0]
    sd = min(tm, MOE_SORT)
    row = lambda i, e: (i, 0)
    return pl.pallas_call(
        functools.partial(_moe_kernel, sd),
        grid=(n // tm, N_EXPERTS),
        in_specs=[pl.BlockSpec((tm, D_MODEL), row), pl.BlockSpec((tm, LANES), row),
                  pl.BlockSpec((tm, D_MODEL), row, pipeline_mode=pl.Buffered(1)),
                  pl.BlockSpec((1, D_MODEL, D_EXPERT), lambda i, e: (e, 0, 0)),
                  pl.BlockSpec((1, D_MODEL, D_EXPERT), lambda i, e: (e, 0, 0)),
                  pl.BlockSpec((1, D_EXPERT, D_MODEL), lambda i, e: (e, 0, 0)),
                  pl.BlockSpec((1, D_MODEL), lambda i, e: (0, 0))],
        out_specs=pl.BlockSpec((tm, D_MODEL), row),
        out_shape=jax.ShapeDtypeStruct((n, D_MODEL), F32),
        scratch_shapes=[pltpu.VMEM((tm, D_MODEL), BF16), pltpu.VMEM((tm, LANES), F32),
                        pltpu.VMEM((tm // sd, sd, sd), BF16), pltpu.SMEM((tm // sd, N_GROUPS + 1), jnp.int32)],
        compiler_params=pltpu.CompilerParams(dimension_semantics=("arbitrary", "arbitrary"),
                                             vmem_limit_bytes=VMEM_LIMIT),
        name="moe",
    )(xn, comb, h, w_gate, w_up, w_down, g_final)


def _block_diag_tiles(w):
    per = 256 // RNN_BLOCK_DIM
    w4 = w.reshape(RNN_BLOCKS // per, per, RNN_BLOCK_DIM, RNN_BLOCK_DIM)
    eye = jnp.eye(per, dtype=w.dtype)
    tiles = jnp.einsum('tpde,pq->tpdqe', w4, eye)
    return tiles.reshape(RNN_BLOCKS // per, 256, 256)


def _layer(l, xp, xs, cache_cmp_kv, cache_sel_kv, cache_win_kv, state_conv, state_h, page_table,
           norm_mix, w_in, cmp_pool_w, conv_w, conv_b, lru_wa, lru_ba, lru_wx, lru_bx, lru_lambda, w_out,
           norm_ffn, w_router_group, b_router_group, w_router_expert, b_router_expert,
           w_exp_gate, w_exp_up, w_exp_down, final_gain):
    bp, sp, _ = xp.shape
    bs, ts, _ = xs.shape
    n_pages = page_table.shape[1]
    past = n_pages * PAGE_SIZE

    wi = w_in[l]
    gt_cols = wi[:, 2560:2584].reshape(D_MODEL, N_KV, GQA * 3)
    gt_cols = jnp.pad(gt_cols, ((0, 0), (0, 0), (0, LANES - GQA * 3))).reshape(D_MODEL, N_KV * LANES)
    w_proj_f32 = jnp.concatenate([wi[:, :2560], wi[:, 2584:], gt_cols], axis=1)
    w_proj = w_proj_f32.astype(BF16)
    g_mix = norm_mix[l].reshape(1, D_MODEL)
    wfull = jnp.tile(jnp.repeat(cmp_pool_w[l], HEAD_DIM, axis=1), (1, N_KV))
    w2 = jnp.broadcast_to(cmp_pool_w[l].T[:, :, None], (2, CMP_BLOCK, HEAD_DIM))
    wa_f32 = _block_diag_tiles(lru_wa[l])
    wx_f32 = _block_diag_tiles(lru_wx[l])
    row = lambda v: v.reshape(1, -1)
    slopes = jnp.exp2(-8.0 * jnp.arange(1, N_HEADS + 1, dtype=F32) / N_HEADS)
    w_o = w_out[l].astype(BF16)
    w_router = jnp.pad(jnp.concatenate([w_router_expert[l], w_router_group[l]], axis=1),
                       ((0, 0), (0, LANES - N_EXPERTS - N_GROUPS)))
    b_router = jnp.pad(jnp.concatenate([b_router_expert[l], b_router_group[l]]),
                       (0, LANES - N_EXPERTS - N_GROUPS)).reshape(1, LANES)
    wg, wu, wd = w_exp_gate[l].astype(BF16), w_exp_up[l].astype(BF16), w_exp_down[l].astype(BF16)
    lru = lambda wa, wx: (conv_w[l], row(conv_b[l]), wa, row(lru_ba[l]), wx, row(lru_bx[l]), row(lru_lambda[l]))

    def tail(x, attn_o, rnn_o, w_out_l, tm_f, tm_m):
        h, xn, comb = _finish(x, attn_o, rnn_o, w_out_l, row(norm_ffn[l]), w_router, b_router, tm_f)
        return _moe(xn, comb, h, wg, wu, wd, final_gain, tm_m)

    np_ = bp * sp
    q, kc, ks, kw, xr, xg, gt = _project(xp.reshape(np_, D_MODEL), g_mix, w_proj, 256)
    shp = lambda a: a.reshape(bp, sp, a.shape[-1])
    ident = jnp.arange(np_ // 1024, dtype=jnp.int32)
    cmp_p = _compress(ident, kc.reshape(np_ // 1024, 1024, KV_W), wfull, bp, sp // 1024, 1, 1024)
    attn_p = _nsa_prompt(slopes, shp(q), shp(gt), cmp_p, shp(ks), shp(kw))
    rnn_p, h_p = _rglru(shp(xr), shp(xg), jnp.zeros((bp, 8, D_RNN), F32), jnp.zeros((bp, 1, D_RNN), F32),
                       *lru(wa_f32.astype(BF16), wx_f32.astype(BF16)), tc=512)
    y_p = tail(xp.reshape(np_, D_MODEL), attn_p.reshape(np_, D_ATTN), rnn_p.reshape(np_, D_RNN), w_o, 256, 1024)
    kv6 = lambda a, b_, t_: a.reshape(b_, t_, N_KV, 2, HEAD_DIM)
    outs_p = (y_p.reshape(bp, sp, D_MODEL), kv6(kc, bp, sp), kv6(ks, bp, sp),
              kv6(kw, bp, sp)[:, -min(WINDOW, sp):], shp(xr)[:, sp - (CONV_W - 1):], h_p.reshape(bp, D_RNN))

    ns_ = bs * ts
    q, kc, ks, kw, xr, xg, gt = _project_precise(xs.reshape(ns_, D_MODEL), g_mix, w_proj_f32)
    shs = lambda a: a.reshape(bs, ts, a.shape[-1])
    cmp_s = _compress_paged(page_table.reshape(-1), cache_cmp_kv[l].reshape(-1, HEAD_DIM), w2, bs, n_pages, 16)
    idx, o_cw = _nsa_sample_a(past, shs(q), shs(gt), cmp_s, cache_win_kv[l].reshape(-1, HEAD_DIM), shs(kw))
    idx = idx.reshape(bs, N_KV, ts, LANES)[..., :N_GATHER].reshape(-1)
    gate_sel = gt.reshape(ns_, N_KV, LANES)[:, :, :GQA * 3].reshape(ns_, N_HEADS, 3)[:, :, 1:2]
    gate_sel = jnp.broadcast_to(gate_sel, (ns_, N_HEADS, HEAD_DIM))
    slopes8 = jnp.broadcast_to(slopes.reshape(N_HEADS, 1), (N_HEADS, LANES))
    new_blocks = jnp.pad(kv6(ks, bs, ts).transpose(0, 2, 1, 3, 4), ((0, 0), (0, 0), (0, SEL_BLOCK - ts), (0, 0), (0, 0)))
    attn_s = _nsa_sample_b(past, idx, page_table.reshape(-1), q.reshape(ns_, N_HEADS, HEAD_DIM), gate_sel, slopes8,
                           o_cw.reshape(ns_, N_HEADS, HEAD_DIM), new_blocks, cache_sel_kv[l].reshape(-1, N_KV, 2, HEAD_DIM))
    conv_rows = jnp.pad(state_conv[l], ((0, 0), (ts - (CONV_W - 1), 0), (0, 0))).reshape(ns_, D_RNN)
    rnn_s, h_all = _rglru_short(xr, xg, conv_rows, jnp.repeat(state_h[l], ts, axis=0), *lru(wa_f32, wx_f32), t=ts)
    h_s = h_all.reshape(bs, ts, D_RNN)[:, -1]
    y_s = tail(xs.reshape(ns_, D_MODEL), attn_s.reshape(ns_, D_ATTN), rnn_s, w_out[l], ns_, ns_)
    win_s = jnp.concatenate([cache_win_kv[l], kv6(kw, bs, ts)], axis=1)[:, ts:]
    conv_s = jnp.concatenate([state_conv[l], shs(xr)], axis=1)[:, ts:]
    outs_s = (y_s.reshape(bs, ts, D_MODEL), kv6(kc, bs, ts), kv6(ks, bs, ts),
              win_s, conv_s, h_s.reshape(bs, D_RNN))
    return outs_p, outs_s


def kernel(x_prompt, x_sample, cache_cmp_kv, cache_sel_kv, cache_win_kv, state_conv, state_h, page_table, norm_mix, w_in, cmp_pool_w, conv_w, conv_b, lru_wa, lru_ba, lru_wx, lru_bx, lru_lambda, w_out, norm_ffn, w_router_group, b_router_group, w_router_expert, b_router_expert, w_exp_gate, w_exp_up, w_exp_down, norm_final):
    depth = w_in.shape[0]
    assert depth == 1, "the final norm is fused into the single layer's expert kernel"
    p, s = _layer(0, x_prompt, x_sample, cache_cmp_kv, cache_sel_kv, cache_win_kv, state_conv, state_h, page_table,
                  norm_mix, w_in, cmp_pool_w, conv_w, conv_b, lru_wa, lru_ba, lru_wx, lru_bx, lru_lambda, w_out,
                  norm_ffn, w_router_group, b_router_group, w_router_expert, b_router_expert,
                  w_exp_gate, w_exp_up, w_exp_down, norm_final.reshape(1, D_MODEL))
    st = lambda a: a[None]
    return (p[0], s[0], st(p[1]), st(s[1]), st(p[2]), st(s[2]), st(p[3]), st(s[3]),
            st(p[4]), st(s[4]), st(p[5]), st(s[5]))
```

```python
import functools

import jax
import jax.numpy as jnp
from jax import lax
from jax.experimental import pallas as pl
from jax.experimental.pallas import tpu as pltpu

F32 = jnp.float32
BF16 = jnp.bfloat16

D_MODEL = 2048
D_ATTN = 1024
D_RNN = 1024
N_HEADS = 8
HEAD_DIM = 128
N_KV = 2
GQA = 4
KV_W = 512
CMP_BLOCK = 32
SEL_BLOCK = 64
TOP_N = 16
WINDOW = 512
FORCE_SCORE = 1e4
RNN_BLOCKS = 16
RNN_BLOCK_DIM = 64
CONV_W = 4
LRU_C = 8.0
N_GROUPS = 4
EXP_PER_GROUP = 4
N_EXPERTS = 16
D_EXPERT = 512
RMS_EPS = 1e-6
PAGE_SIZE = 128

GROUP_LANE = N_EXPERTS
LANES = 128
NEG = -1e30
VMEM_LIMIT = 56 * 1024 * 1024

_SEG_Q = (0, 1024)
_SEG_KC = (1024, 1536)
_SEG_KS = (1536, 2048)
_SEG_KW = (2048, 2560)
_SEG_XR = (2560, 3584)
_SEG_XG = (3584, 4608)
_SEG_GT = (4608, 4864)
_PROJ_W = 4864


def _dot(a, b):
    return jnp.dot(a, b, preferred_element_type=F32)


def _dot_nt(a, b):
    return lax.dot_general(a, b, (((1,), (1,)), ((), ())), preferred_element_type=F32)


def _rms(x, g):
    return x * lax.rsqrt(jnp.mean(x * x, axis=-1, keepdims=True) + RMS_EPS) * g


def _split3(x):
    h1 = x.astype(BF16)
    r1 = x - h1.astype(F32)
    h2 = r1.astype(BF16)
    h3 = (r1 - h2.astype(F32)).astype(BF16)
    return h1, h2, h3


def _dot_f32(a, b, dot=_dot):
    a1, a2, a3 = _split3(a)
    b1, b2, b3 = _split3(b)
    return (dot(a3, b1) + dot(a2, b2) + dot(a1, b3)) + (dot(a2, b1) + dot(a1, b2)) + dot(a1, b1)


def _dot_x3(a, b, dot=_dot):
    a1, a2, _ = _split3(a)
    b1, b2, _ = _split3(b)
    return (dot(a2, b1) + dot(a1, b2)) + dot(a1, b1)


def _mm(a, w):
    if w.dtype == BF16:
        return _dot(a.astype(BF16), w)
    return _dot_x3(a, w)


def _dot_sel(a, m01):
    a1, a2, a3 = _split3(a)
    return (_dot(a3, m01) + _dot(a2, m01)) + _dot(a1, m01)


def _proj_kernel(x_ref, g_ref, w_ref, q_ref, kc_ref, ks_ref, kw_ref, xr_ref, xg_ref, gt_ref):
    xn = _rms(x_ref[...], g_ref[...]).astype(BF16)
    outs = ((q_ref, _SEG_Q), (kc_ref, _SEG_KC), (ks_ref, _SEG_KS), (kw_ref, _SEG_KW),
            (xr_ref, _SEG_XR), (xg_ref, _SEG_XG), (gt_ref, _SEG_GT))
    for ref, (lo, hi) in outs:
        ref[...] = _dot(xn, w_ref[:, lo:hi])


def _project(x, g, w, tm):
    n = x.shape[0]
    widths = [hi - lo for lo, hi in (_SEG_Q, _SEG_KC, _SEG_KS, _SEG_KW, _SEG_XR, _SEG_XG, _SEG_GT)]
    return pl.pallas_call(
        _proj_kernel,
        grid=(n // tm,),
        in_specs=[pl.BlockSpec((tm, D_MODEL), lambda i: (i, 0)),
                  pl.BlockSpec((1, D_MODEL), lambda i: (0, 0)),
                  pl.BlockSpec((D_MODEL, _PROJ_W), lambda i: (0, 0), pipeline_mode=pl.Buffered(1))],
        out_specs=[pl.BlockSpec((tm, w_), lambda i: (i, 0)) for w_ in widths],
        out_shape=[jax.ShapeDtypeStruct((n, w_), F32) for w_ in widths],
        compiler_params=pltpu.CompilerParams(dimension_semantics=("arbitrary",), vmem_limit_bytes=VMEM_LIMIT),
        name="proj",
    )(x, g, w)


def _proj_precise_kernel(x_ref, g_ref, w_ref, z_ref, xn_scr):
    @pl.when(pl.program_id(0) == 0)
    def _():
        xn_scr[...] = _rms(x_ref[...], g_ref[...])

    z_ref[...] = _dot_x3(xn_scr[...], w_ref[...])


def _project_precise(x, g, w_f32):
    n = x.shape[0]
    tn = 2 * LANES
    z = pl.pallas_call(
        _proj_precise_kernel,
        grid=(_PROJ_W // tn,),
        in_specs=[pl.BlockSpec((n, D_MODEL), lambda j: (0, 0)),
                  pl.BlockSpec((1, D_MODEL), lambda j: (0, 0)),
                  pl.BlockSpec((D_MODEL, tn), lambda j: (0, j))],
        out_specs=pl.BlockSpec((n, tn), lambda j: (0, j)),
        out_shape=jax.ShapeDtypeStruct((n, _PROJ_W), F32),
        scratch_shapes=[pltpu.VMEM((n, D_MODEL), F32)],
        compiler_params=pltpu.CompilerParams(dimension_semantics=("arbitrary",), vmem_limit_bytes=VMEM_LIMIT),
        name="proj_precise",
    )(x, g, w_f32)
    return [z[:, lo:hi] for lo, hi in (_SEG_Q, _SEG_KC, _SEG_KS, _SEG_KW, _SEG_XR, _SEG_XG, _SEG_GT)]


def _compress_kernel(n_pages, pt_ref, *refs):
    del pt_ref
    pages, w_ref, o_ref = refs[:n_pages], refs[n_pages], refs[n_pages + 1]
    if n_pages == 1:
        x = pages[0][0]
    else:
        x = jnp.concatenate([p[0] for p in pages], axis=0)
    rows = x.shape[0]
    xb = x.reshape(rows // CMP_BLOCK, CMP_BLOCK, KV_W) * w_ref[...][None]
    o_ref[0] = jnp.sum(xb, axis=1)


def _compress(table, src, wfull, n_batch, steps, n_pages, page_rows):
    out_rows = n_pages * page_rows // CMP_BLOCK
    per_b = steps * n_pages

    def page_spec(k):
        return pl.BlockSpec((1, page_rows, KV_W), lambda b, j, pt: (pt[b * per_b + j * n_pages + k], 0, 0))

    return pl.pallas_call(
        functools.partial(_compress_kernel, n_pages),
        grid_spec=pltpu.PrefetchScalarGridSpec(
            num_scalar_prefetch=1,
            grid=(n_batch, steps),
            in_specs=[page_spec(k) for k in range(n_pages)] + [pl.BlockSpec((CMP_BLOCK, KV_W), lambda b, j, pt: (0, 0))],
            out_specs=pl.BlockSpec((1, out_rows, KV_W), lambda b, j, pt: (b, j, 0)),
        ),
        out_shape=jax.ShapeDtypeStruct((n_batch, steps * out_rows, KV_W), F32),
        compiler_params=pltpu.CompilerParams(dimension_semantics=("arbitrary", "arbitrary"), vmem_limit_bytes=VMEM_LIMIT),
        name="compress",
    )(table, *([src] * n_pages), wfull)


def _softplus(x):
    return jnp.maximum(x, 0.0) + jnp.log1p(jnp.exp(-jnp.abs(x)))


def _rglru_kernel(tc, xr_ref, xg_ref, cs_ref, h0_ref, cw_ref, cb_ref, wa_ref, ba_ref, wx_ref, bx_ref, lam_ref,
                  o_ref, hl_ref, tail_scr, h_scr):
    @pl.when(pl.program_id(2) == 0)
    def _():
        tail_scr[...] = cs_ref[0]
        h_scr[...] = h0_ref[0]

    x = xr_ref[0]
    xp = jnp.concatenate([tail_scr[...], x], axis=0)
    w = cw_ref[...]
    xc = cb_ref[...] + pltpu.roll(xp, 3, axis=0)[8:] * w[0:1]
    xc = xc + pltpu.roll(xp, 2, axis=0)[8:] * w[1:2]
    xc = xc + pltpu.roll(xp, 1, axis=0)[8:] * w[2:3]
    xc = xc + x * w[3:4]
    tail_scr[...] = xp[tc:]

    r = jax.nn.sigmoid(_mm(xc, wa_ref[0]) + ba_ref[...])
    gi = jax.nn.sigmoid(_mm(xc, wx_ref[0]) + bx_ref[...])
    log_a = -LRU_C * r * _softplus(-lam_ref[...])
    a = jnp.exp(log_a)
    u = jnp.sqrt(-jnp.tanh(log_a) * (a * a + 1.0)) * (gi * xc)

    row = lax.broadcasted_iota(jnp.int32, a.shape, 0)
    s = 1
    while s < tc:
        keep = row >= s
        a_sh = jnp.where(keep, pltpu.roll(a, s, axis=0), 1.0)
        u_sh = jnp.where(keep, pltpu.roll(u, s, axis=0), 0.0)
        u = a * u_sh + u
        a = a * a_sh
        s *= 2
    h = a * h_scr[...] + u
    h_last = h[tc - 1:tc]
    h_scr[...] = h_last
    hl_ref[0] = h_last
    o_ref[0] = h * jax.nn.gelu(xg_ref[0])


def _rglru(xr, xg, conv_state8, h0, conv_w, conv_b, wa_t, ba, wx_t, bx, lam, tc):
    b, t, _ = xr.shape
    ct = 256
    n_ct = D_RNN // ct
    seq = lambda bi, c, j: (bi, j, c)
    per_b = lambda bi, c, j: (bi, 0, c)
    par = lambda bi, c, j: (0, c)
    return pl.pallas_call(
        functools.partial(_rglru_kernel, tc),
        grid=(b, n_ct, t // tc),
        in_specs=[pl.BlockSpec((1, tc, ct), seq), pl.BlockSpec((1, tc, ct), seq),
                  pl.BlockSpec((1, 8, ct), per_b), pl.BlockSpec((1, 1, ct), per_b),
                  pl.BlockSpec((CONV_W, ct), par), pl.BlockSpec((1, ct), par),
                  pl.BlockSpec((1, ct, ct), lambda bi, c, j: (c, 0, 0)), pl.BlockSpec((1, ct), par),
                  pl.BlockSpec((1, ct, ct), lambda bi, c, j: (c, 0, 0)), pl.BlockSpec((1, ct), par),
                  pl.BlockSpec((1, ct), par)],
        out_specs=[pl.BlockSpec((1, tc, ct), seq), pl.BlockSpec((1, 1, ct), per_b)],
        out_shape=[jax.ShapeDtypeStruct((b, t, D_RNN), F32), jax.ShapeDtypeStruct((b, 1, D_RNN), F32)],
        scratch_shapes=[pltpu.VMEM((8, ct), F32), pltpu.VMEM((1, ct), F32)],
        compiler_params=pltpu.CompilerParams(dimension_semantics=("arbitrary", "arbitrary", "arbitrary"),
                                             vmem_limit_bytes=VMEM_LIMIT),
        name="rglru",
    )(xr, xg, conv_state8, h0, conv_w, conv_b, wa_t, ba, wx_t, bx, lam)


def _rglru_short_kernel(t, xr_ref, xg_ref, cs_ref, h0_ref, cw_ref, cb_ref, wa_ref, ba_ref, wx_ref, bx_ref, lam_ref,
                        o_ref, h_ref):
    x = xr_ref[...]
    st = cs_ref[...]
    n = x.shape[0]
    step = lax.broadcasted_iota(jnp.int32, x.shape, 0) % t
    w = cw_ref[...]
    xc = cb_ref[...] + x * w[CONV_W - 1:CONV_W]
    for sh in range(1, CONV_W):
        prev = jnp.where(step >= sh, pltpu.roll(x, sh, axis=0), pltpu.roll(st, n - (t - sh), axis=0))
        xc = xc + prev * w[CONV_W - 1 - sh:CONV_W - sh]
    r = jax.nn.sigmoid(_mm(xc, wa_ref[0]) + ba_ref[...])
    gi = jax.nn.sigmoid(_mm(xc, wx_ref[0]) + bx_ref[...])
    log_a = -LRU_C * r * _softplus(-lam_ref[...])
    a = jnp.exp(log_a)
    u = jnp.sqrt(-jnp.tanh(log_a) * (a * a + 1.0)) * (gi * xc)
    s = 1
    while s < t:
        keep = step >= s
        a_sh = jnp.where(keep, pltpu.roll(a, s, axis=0), 1.0)
        u_sh = jnp.where(keep, pltpu.roll(u, s, axis=0), 0.0)
        u = a * u_sh + u
        a = a * a_sh
        s *= 2
    h = a * h0_ref[...] + u
    h_ref[...] = h
    o_ref[...] = h * jax.nn.gelu(xg_ref[...])


def _rglru_short(xr, xg, conv_rows, h0_rows, conv_w, conv_b, wa_t, ba, wx_t, bx, lam, t):
    n = xr.shape[0]
    ct = 256
    blk = pl.BlockSpec((n, ct), lambda c: (0, c))
    par = lambda c: (0, c)
    tile = pl.BlockSpec((1, ct, ct), lambda c: (c, 0, 0))
    return pl.pallas_call(
        functools.partial(_rglru_short_kernel, t),
        grid=(D_RNN // ct,),
        in_specs=[blk, blk, blk, blk, pl.BlockSpec((CONV_W, ct), par), pl.BlockSpec((1, ct), par),
                  tile, pl.BlockSpec((1, ct), par), tile, pl.BlockSpec((1, ct), par), pl.BlockSpec((1, ct), par)],
        out_specs=[blk, blk],
        out_shape=[jax.ShapeDtypeStruct((n, D_RNN), F32), jax.ShapeDtypeStruct((n, D_RNN), F32)],
        compiler_params=pltpu.CompilerParams(dimension_semantics=("arbitrary",), vmem_limit_bytes=VMEM_LIMIT),
        name="rglru_short",
    )(xr, xg, conv_rows, h0_rows, conv_w, conv_b, wa_t, ba, wx_t, bx, lam)


TQ = 128
W4 = GQA * TQ
WIN_KEYS = WINDOW + TQ


def _nsa_prompt_kernel(slopes_ref, q_ref, gt_ref, kc_ref, vc_ref, ks_ref, vs_ref, kw_ref, vw_ref, o_ref,
                       q_scr, kcb, vct, ksb, vst, kwb, vwt, bias0, caus, wlow,
                       impt_scr, selb_scr, m_scr, l_scr, acc_scr, out_scr):
    g = pl.program_id(1)
    i = pl.program_id(2)
    s_len = ks_ref.shape[1]
    n_cmp = kc_ref.shape[1]
    n_sel = n_cmp // 2
    lane = lax.broadcasted_iota(jnp.int32, (1, W4), 1)
    tl_row = (lane % TQ).astype(F32)
    slope_row = jnp.full((1, W4), slopes_ref[g * GQA + GQA - 1], F32)
    for r in reversed(range(GQA - 1)):
        slope_row = jnp.where(lane < (r + 1) * TQ, slopes_ref[g * GQA + r], slope_row)

    @pl.when(i == 0)
    def _prepare():
        kcb[...] = kc_ref[0].astype(BF16)
        vct[...] = vc_ref[0].T.astype(BF16)

        def cast(j, carry):
            off = pl.multiple_of(j * LANES, LANES)
            ksb[pl.ds(off, LANES), :] = ks_ref[0, pl.ds(off, LANES), :].astype(BF16)
            kwb[pl.ds(off, LANES), :] = kw_ref[0, pl.ds(off, LANES), :].astype(BF16)
            vst[:, pl.ds(off, LANES)] = vs_ref[0, pl.ds(off, LANES), :].T.astype(BF16)
            vwt[:, pl.ds(off, LANES)] = vw_ref[0, pl.ds(off, LANES), :].T.astype(BF16)
            return carry

        lax.fori_loop(0, s_len // LANES, cast, 0)
        rel = tl_row - lax.broadcasted_iota(jnp.int32, (WIN_KEYS, W4), 0).astype(F32)
        bias0[...] = slope_row * rel
        caus[...] = jnp.where(rel[0:LANES] >= 0, 0.0, NEG)
        wlow[...] = jnp.where(rel[0:LANES] <= 0, 0.0, NEG)

    start_f = (i * TQ).astype(F32)
    scale = HEAD_DIM ** -0.5
    for r in range(GQA):
        q_scr[r * TQ:(r + 1) * TQ, :] = (q_ref[0, :, r * HEAD_DIM:(r + 1) * HEAD_DIM] * scale).astype(BF16)
    gate_t = jax.nn.sigmoid(gt_ref[0]).T

    def gate_row(branch):
        return jnp.concatenate([gate_t[3 * r + branch:3 * r + branch + 1, :] for r in range(GQA)], axis=1)

    t_row = start_f + tl_row
    c_end = ((lax.broadcasted_iota(jnp.int32, (n_cmp, W4), 0) + 1) * CMP_BLOCK - 1).astype(F32)
    dist_c = t_row - c_end
    ok_c = dist_c >= 0
    x = jnp.where(ok_c, _dot_nt(kcb[...], q_scr[...]) - slope_row * dist_c, NEG)
    e = jnp.where(ok_c, jnp.exp(x - jnp.max(x, axis=0, keepdims=True)), 0.0)
    p = e * (1.0 / jnp.maximum(jnp.sum(e, axis=0, keepdims=True), 1e-30))
    out_scr[...] = gate_row(0) * _dot(vct[...], p.astype(BF16))

    p_heads = p[:, 0:TQ]
    for r in range(1, GQA):
        p_heads = p_heads + p[:, r * TQ:(r + 1) * TQ]
    impt_scr[...] = p_heads
    imp = impt_scr[pl.ds(0, n_sel, stride=2), :] + impt_scr[pl.ds(1, n_sel, stride=2), :]
    blk = lax.broadcasted_iota(jnp.int32, (n_sel, TQ), 0)
    cur = (i * TQ + lax.broadcasted_iota(jnp.int32, (n_sel, TQ), 1)) // SEL_BLOCK
    forced = (blk == 0) | (blk == cur) | (blk == cur - 1)
    imp = jnp.where(forced, FORCE_SCORE, imp)
    imp = jnp.where(blk > cur, NEG, imp)
    rank = jnp.zeros((n_sel, TQ), jnp.int32)
    for j in range(n_sel):
        row = imp[j:j + 1, :]
        tie = jnp.where(blk > j, jnp.where(row == imp, 1, 0), 0)
        rank = rank + jnp.where(row > imp, 1, tie)
    selb = jnp.where(rank < TOP_N, 0.0, NEG)
    for j in range(n_sel):
        selb_scr[j] = selb[j:j + 1, :]

    def scores(k_b, c, nk):
        off = pl.multiple_of(c * LANES, LANES)
        return _dot_nt(k_b[pl.ds(off, nk), :], q_scr[...]) - bias0[0:nk, :]

    def block_mask(c, nk):
        rows = [jnp.broadcast_to(selb_scr[2 * c + k], (SEL_BLOCK, TQ)) for k in range(nk // SEL_BLOCK)]
        return jnp.concatenate([jnp.concatenate(rows, axis=0)] * GQA, axis=1)

    def update(v_t, c, nk, x):
        off = pl.multiple_of(c * LANES, LANES)
        r_c = slope_row * ((c * LANES).astype(F32) - start_f)
        m_old = m_scr[...]
        m_new = jnp.maximum(m_old, jnp.max(x, axis=0, keepdims=True) + r_c)
        pr = jnp.exp(x + (r_c - m_new))
        alpha = jnp.exp(m_old - m_new)
        l_scr[...] = alpha * l_scr[...] + jnp.sum(pr, axis=0, keepdims=True)
        acc_scr[...] = alpha * acc_scr[...] + _dot(v_t[:, pl.ds(off, nk)], pr.astype(BF16))
        m_scr[...] = m_new

    def reset():
        m_scr[...] = jnp.full((1, W4), NEG, F32)
        l_scr[...] = jnp.zeros((1, W4), F32)
        acc_scr[...] = jnp.zeros((HEAD_DIM, W4), F32)

    def result():
        return acc_scr[...] * (1.0 / jnp.maximum(l_scr[...], 1e-30))

    reset()
    n_big = i // 4

    def big(c4, carry):
        update(vst, 4 * c4, 4 * LANES, scores(ksb, 4 * c4, 4 * LANES) + block_mask(4 * c4, 4 * LANES))
        return carry

    lax.fori_loop(0, n_big, big, 0)
    rem = i - 4 * n_big

    @pl.when(rem >= 2)
    def _():
        update(vst, 4 * n_big, 2 * LANES, scores(ksb, 4 * n_big, 2 * LANES) + block_mask(4 * n_big, 2 * LANES))

    @pl.when(rem % 2 == 1)
    def _():
        update(vst, i - 1, LANES, scores(ksb, i - 1, LANES) + block_mask(i - 1, LANES))

    update(vst, i, LANES, scores(ksb, i, LANES) + block_mask(i, LANES) + caus[...])
    out_scr[...] += gate_row(1) * result()

    n_back = WINDOW // LANES

    @pl.when(i >= n_back)
    def _():
        c = i - n_back
        x = scores(kwb, c, WIN_KEYS)
        x = jnp.concatenate([x[0:LANES] + wlow[...], x[LANES:WINDOW], x[WINDOW:WIN_KEYS] + caus[...]], axis=0)
        pr = jnp.exp(x - jnp.max(x, axis=0, keepdims=True))
        o_w = _dot(vwt[:, pl.ds(pl.multiple_of(c * LANES, LANES), WIN_KEYS)], pr.astype(BF16))
        out_scr[...] += gate_row(2) * (o_w * (1.0 / jnp.maximum(jnp.sum(pr, axis=0, keepdims=True), 1e-30)))

    @pl.when(i < n_back)
    def _():
        reset()

        def body(c, carry):
            update(vwt, c, LANES, scores(kwb, c, LANES))
            return carry

        lax.fori_loop(0, i, body, 0)
        update(vwt, i, LANES, scores(kwb, i, LANES) + caus[...])
        out_scr[...] += gate_row(2) * result()

    o_t = out_scr[...]
    for r in range(GQA):
        o_ref[0, :, r * HEAD_DIM:(r + 1) * HEAD_DIM] = o_t[:, r * TQ:(r + 1) * TQ].T


def _nsa_prompt(slopes, q, gt, cmp_kv, ks, kw):
    b, s, _ = q.shape
    n_cmp = cmp_kv.shape[1]
    k_of = lambda bi, g, i: (bi, 0, 2 * g)
    v_of = lambda bi, g, i: (bi, 0, 2 * g + 1)
    tile = pltpu.VMEM((LANES, W4), F32)
    return pl.pallas_call(
        _nsa_prompt_kernel,
        grid=(b, N_KV, s // TQ),
        in_specs=[pl.BlockSpec(memory_space=pltpu.SMEM),
                  pl.BlockSpec((1, TQ, GQA * HEAD_DIM), lambda bi, g, i: (bi, i, g)),
                  pl.BlockSpec((1, TQ, LANES), lambda bi, g, i: (bi, i, g)),
                  pl.BlockSpec((1, n_cmp, HEAD_DIM), k_of), pl.BlockSpec((1, n_cmp, HEAD_DIM), v_of),
                  pl.BlockSpec((1, s, HEAD_DIM), k_of), pl.BlockSpec((1, s, HEAD_DIM), v_of),
                  pl.BlockSpec((1, s, HEAD_DIM), k_of), pl.BlockSpec((1, s, HEAD_DIM), v_of)],
        out_specs=pl.BlockSpec((1, TQ, GQA * HEAD_DIM), lambda bi, g, i: (bi, i, g)),
        out_shape=jax.ShapeDtypeStruct((b, s, D_ATTN), F32),
        scratch_shapes=[pltpu.VMEM((W4, HEAD_DIM), BF16),
                        pltpu.VMEM((n_cmp, HEAD_DIM), BF16), pltpu.VMEM((HEAD_DIM, n_cmp), BF16),
                        pltpu.VMEM((s, HEAD_DIM), BF16), pltpu.VMEM((HEAD_DIM, s), BF16),
                        pltpu.VMEM((s, HEAD_DIM), BF16), pltpu.VMEM((HEAD_DIM, s), BF16),
                        pltpu.VMEM((WIN_KEYS, W4), F32), tile, tile,
                        pltpu.VMEM((n_cmp, TQ), F32),
                        pltpu.VMEM((n_cmp // 2, 1, TQ), F32),
                        pltpu.VMEM((1, W4), F32), pltpu.VMEM((1, W4), F32),
                        pltpu.VMEM((HEAD_DIM, W4), F32), pltpu.VMEM((HEAD_DIM, W4), F32)],
        compiler_params=pltpu.CompilerParams(dimension_semantics=("arbitrary", "arbitrary", "arbitrary"),
                                             vmem_limit_bytes=VMEM_LIMIT),
        name="nsa_prompt",
    )(slopes, q, gt, cmp_kv, cmp_kv, ks, ks, kw, kw)


KV_ROWS = 2 * N_KV


def _compress_paged_kernel(n_pages, pt_ref, *refs):
    del pt_ref
    pages, w_ref, o_ref = refs[:n_pages], refs[n_pages], refs[n_pages + 1]
    for gc in range(KV_ROWS):
        x = jnp.concatenate([p[pl.ds(gc, PAGE_SIZE, stride=KV_ROWS), :] for p in pages], axis=0)
        xb = x.reshape(n_pages * PAGE_SIZE // CMP_BLOCK, CMP_BLOCK, HEAD_DIM) * w_ref[gc % 2][None]
        o_ref[0, gc] = jnp.sum(xb, axis=1)


def _compress_paged(table, pool2d, w2, n_batch, pages_per_b, n_pages):
    steps = pages_per_b // n_pages
    out_rows = n_pages * PAGE_SIZE // CMP_BLOCK
    rows = PAGE_SIZE * KV_ROWS

    def page_spec(k):
        return pl.BlockSpec((rows, HEAD_DIM), lambda b, j, pt: (pt[b * pages_per_b + j * n_pages + k], 0))

    return pl.pallas_call(
        functools.partial(_compress_paged_kernel, n_pages),
        grid_spec=pltpu.PrefetchScalarGridSpec(
            num_scalar_prefetch=1,
            grid=(n_batch, steps),
            in_specs=[page_spec(k) for k in range(n_pages)]
            + [pl.BlockSpec((2, CMP_BLOCK, HEAD_DIM), lambda b, j, pt: (0, 0, 0))],
            out_specs=pl.BlockSpec((1, KV_ROWS, out_rows, HEAD_DIM), lambda b, j, pt: (b, 0, j, 0)),
        ),
        out_shape=jax.ShapeDtypeStruct((n_batch, KV_ROWS, steps * out_rows, HEAD_DIM), F32),
        compiler_params=pltpu.CompilerParams(dimension_semantics=("arbitrary", "arbitrary"), vmem_limit_bytes=VMEM_LIMIT),
        name="compress_paged",
    )(table, *([pool2d] * n_pages), w2)


def _nsa_sample_a_kernel(past, q_ref, gt_ref, cmp_ref, cw_ref, kwn_ref, idx_ref, o_ref):
    t_new = q_ref.shape[1]
    n_cmp = cmp_ref.shape[2]
    n_past_sel = n_cmp // 2
    n_win = cw_ref.shape[0] // KV_ROWS
    rows = GQA * t_new
    gates = jax.nn.sigmoid(gt_ref[0])
    t_row = past + lax.broadcasted_iota(jnp.int32, (rows, 1), 0) % t_new
    scale = HEAD_DIM ** -0.5
    pair = jnp.where(lax.broadcasted_iota(jnp.int32, (n_cmp, n_past_sel), 0) // 2
                     == lax.broadcasted_iota(jnp.int32, (n_cmp, n_past_sel), 1), 1.0, 0.0).astype(BF16)
    imps = []
    for g in range(N_KV):
        qg = jnp.concatenate([q_ref[0, :, (g * GQA + r) * HEAD_DIM:(g * GQA + r + 1) * HEAD_DIM]
                              for r in range(GQA)], axis=0)
        qg = qg * scale
        slope = jnp.concatenate([jnp.full((t_new, 1), 2.0 ** -(g * GQA + r + 1), F32) for r in range(GQA)], axis=0)
        kcol = g * 2 * HEAD_DIM
        kc = cmp_ref[0, 2 * g]
        vc = cmp_ref[0, 2 * g + 1]
        c_end = (lax.broadcasted_iota(jnp.int32, (1, n_cmp), 1) + 1) * CMP_BLOCK - 1
        dist_c = t_row - c_end
        mask_c = dist_c >= 0
        s = _dot_x3(qg, kc, _dot_nt) - slope * dist_c.astype(F32)
        s = jnp.where(mask_c, s, NEG)
        m = jnp.max(s, axis=-1, keepdims=True)
        p = jnp.where(mask_c, jnp.exp(s - m), 0.0)
        p = p / jnp.maximum(jnp.sum(p, axis=-1, keepdims=True), 1e-30)
        o_c = _dot_x3(p, vc)
        p_heads = p[0:t_new]
        for r in range(1, GQA):
            p_heads = p_heads + p[r * t_new:(r + 1) * t_new]
        imps.append(_dot_sel(p_heads, pair))
        n_pad = LANES - t_new
        kw = jnp.concatenate([cw_ref[pl.ds(2 * g, n_win, stride=KV_ROWS), :], kwn_ref[0, :, kcol:kcol + HEAD_DIM],
                              jnp.zeros((n_pad, HEAD_DIM), F32)], axis=0)
        vw = jnp.concatenate([cw_ref[pl.ds(2 * g + 1, n_win, stride=KV_ROWS), :],
                              kwn_ref[0, :, kcol + HEAD_DIM:kcol + 2 * HEAD_DIM],
                              jnp.zeros((n_pad, HEAD_DIM), F32)], axis=0)
        win_pos = past - n_win + lax.broadcasted_iota(jnp.int32, (1, n_win + LANES), 1)
        dist_w = t_row - win_pos
        mask_w = (dist_w >= 0) & (dist_w <= WINDOW)
        s = _dot_x3(qg, kw, _dot_nt) - slope * dist_w.astype(F32)
        s = jnp.where(mask_w, s, NEG)
        m = jnp.max(s, axis=-1, keepdims=True)
        p = jnp.where(mask_w, jnp.exp(s - m), 0.0)
        p = p / jnp.maximum(jnp.sum(p, axis=-1, keepdims=True), 1e-30)
        o_w = _dot_x3(p, vw)
        for r in range(GQA):
            h = g * GQA + r
            gl = g * LANES + 3 * r
            o_ref[0, :, h * HEAD_DIM:(h + 1) * HEAD_DIM] = (
                gates[:, gl:gl + 1] * o_c[r * t_new:(r + 1) * t_new]
                + gates[:, gl + 2:gl + 3] * o_w[r * t_new:(r + 1) * t_new])

    imp = jnp.concatenate(imps, axis=0)
    n_rows = N_KV * t_new
    lane = lax.broadcasted_iota(jnp.int32, (n_rows, n_past_sel), 1)
    lane_f = lane.astype(F32)
    cur = (past + lax.broadcasted_iota(jnp.int32, (n_rows, n_past_sel), 0) % t_new) // SEL_BLOCK
    forced = (lane == 0) | (lane == cur) | (lane == cur - 1)
    imp = jnp.where(forced, FORCE_SCORE, imp)
    imp = jnp.where(lane > cur, NEG, imp)
    out_lane = lax.broadcasted_iota(jnp.int32, (n_rows, LANES), 1)
    idx = jnp.zeros((n_rows, LANES), F32)
    for k in range(TOP_N - 1):
        m = jnp.max(imp, axis=-1, keepdims=True)
        j = jnp.min(jnp.where(imp == m, lane_f, float(n_past_sel)), axis=-1, keepdims=True)
        idx = jnp.where(out_lane == k, j, idx)
        imp = jnp.where(lane_f == j, -3e38, imp)
    idx_ref[0] = idx.astype(jnp.int32)


def _nsa_sample_a(past, q, gt, cmp_kv, cache_w2d, kw_new):
    b, t_new, _ = q.shape
    win_rows = cache_w2d.shape[0] // b
    per_b3 = lambda bi: (bi, 0, 0)
    return pl.pallas_call(
        functools.partial(_nsa_sample_a_kernel, past),
        grid=(b,),
        in_specs=[pl.BlockSpec((1, t_new, D_ATTN), per_b3), pl.BlockSpec((1, t_new, 2 * LANES), per_b3),
                  pl.BlockSpec((1,) + cmp_kv.shape[1:], lambda bi: (bi, 0, 0, 0)),
                  pl.BlockSpec((win_rows, HEAD_DIM), lambda bi: (bi, 0)),
                  pl.BlockSpec((1, t_new, KV_W), per_b3)],
        out_specs=[pl.BlockSpec((1, N_KV * t_new, LANES), per_b3), pl.BlockSpec((1, t_new, D_ATTN), per_b3)],
        out_shape=[jax.ShapeDtypeStruct((b, N_KV * t_new, LANES), jnp.int32),
                   jax.ShapeDtypeStruct((b, t_new, D_ATTN), F32)],
        compiler_params=pltpu.CompilerParams(dimension_semantics=("arbitrary",), vmem_limit_bytes=VMEM_LIMIT),
        name="nsa_sample_a",
    )(q, gt, cmp_kv, cache_w2d, kw_new)


N_GATHER = TOP_N - 1
BLOCK_ROWS = SEL_BLOCK * KV_ROWS
GATHER_KEYS = TOP_N * SEL_BLOCK


def _nsa_sample_b_kernel(past, t_new, idx_ref, pt_ref, q_ref, gs_ref, slope_ref, ocw_ref, new_ref, pool_ref,
                         o_ref, kv_buf, sems):
    n_steps = pl.num_programs(0) * t_new
    step = pl.program_id(0) * t_new + pl.program_id(1)
    slot = step % 2
    pages_per_b = past // PAGE_SIZE
    sel_per_page = PAGE_SIZE // SEL_BLOCK

    def block_index(st, g, k):
        return idx_ref[((st // t_new * N_KV + g) * t_new + st % t_new) * N_GATHER + k]

    def copies(st, sl):
        out = []
        for g in range(N_KV):
            for k in range(N_GATHER):
                lp = block_index(st, g, k)
                phys = pt_ref[st // t_new * pages_per_b + lp // sel_per_page] * sel_per_page + lp % sel_per_page
                out.append(pltpu.make_async_copy(
                    pool_ref.at[pl.ds(pl.multiple_of(phys * BLOCK_ROWS, BLOCK_ROWS), BLOCK_ROWS), :],
                    kv_buf.at[sl, g, pl.ds(k * BLOCK_ROWS, BLOCK_ROWS), :], sems.at[sl, g, k]))
            out.append(pltpu.make_async_copy(
                new_ref.at[st // t_new], kv_buf.at[sl, g, pl.ds(N_GATHER * BLOCK_ROWS, BLOCK_ROWS), :],
                sems.at[sl, g, N_GATHER]))
        return out

    @pl.when(step == 0)
    def _():
        for cp in copies(step, slot):
            cp.start()

    @pl.when(step + 1 < n_steps)
    def _():
        for cp in copies(step + 1, 1 - slot):
            cp.start()

    t_pos = past + pl.program_id(1)
    q8 = q_ref[0] * HEAD_DIM ** -0.5
    slope = slope_ref[:, 0:1]
    lane = lax.broadcasted_iota(jnp.int32, (1, GATHER_KEYS), 1)
    for cp in copies(step, slot):
        cp.wait()
    outs = []
    for g in range(N_KV):
        pos = past + (lane - N_GATHER * SEL_BLOCK)
        for k in range(N_GATHER):
            pos = jnp.where(lane // SEL_BLOCK == k, block_index(step, g, k) * SEL_BLOCK + lane % SEL_BLOCK, pos)
        kk = kv_buf[slot, g, pl.ds(2 * g, GATHER_KEYS, stride=KV_ROWS), :]
        vv = kv_buf[slot, g, pl.ds(2 * g + 1, GATHER_KEYS, stride=KV_ROWS), :]
        dist = t_pos - pos
        mask = dist >= 0
        s = _dot_x3(q8, kk, _dot_nt) - slope * dist.astype(F32)
        s = jnp.where(mask, s, NEG)
        m = jnp.max(s, axis=-1, keepdims=True)
        p = jnp.where(mask, jnp.exp(s - m), 0.0)
        p = p / jnp.maximum(jnp.sum(p, axis=-1, keepdims=True), 1e-30)
        outs.append(_dot_x3(p, vv))
    head = lax.broadcasted_iota(jnp.int32, (N_HEADS, HEAD_DIM), 0)
    o_s = jnp.where(head < GQA, outs[0], outs[1])
    o_ref[0] = ocw_ref[0] + jax.nn.sigmoid(gs_ref[0]) * o_s


def _nsa_sample_b(past, idx, page_table, q, gate_sel, slopes8, o_cw, new_blocks, pool2d):
    n_tok = q.shape[0]
    b = new_blocks.shape[0]
    t_new = n_tok // b
    tok = lambda bi, ti, *_: (bi * t_new + ti, 0, 0)
    return pl.pallas_call(
        functools.partial(_nsa_sample_b_kernel, past, t_new),
        grid_spec=pltpu.PrefetchScalarGridSpec(
            num_scalar_prefetch=2,
            grid=(b, t_new),
            in_specs=[pl.BlockSpec((1, N_HEADS, HEAD_DIM), tok), pl.BlockSpec((1, N_HEADS, HEAD_DIM), tok),
                      pl.BlockSpec((N_HEADS, LANES), lambda bi, ti, *_: (0, 0)),
                      pl.BlockSpec((1, N_HEADS, HEAD_DIM), tok),
                      pl.BlockSpec(memory_space=pl.ANY), pl.BlockSpec(memory_space=pl.ANY)],
            out_specs=pl.BlockSpec((1, N_HEADS, HEAD_DIM), tok),
            scratch_shapes=[pltpu.VMEM((2, N_KV, TOP_N * BLOCK_ROWS, HEAD_DIM), F32),
                            pltpu.SemaphoreType.DMA((2, N_KV, TOP_N))],
        ),
        out_shape=jax.ShapeDtypeStruct((n_tok, N_HEADS, HEAD_DIM), F32),
        compiler_params=pltpu.CompilerParams(dimension_semantics=("arbitrary", "arbitrary"),
                                             vmem_limit_bytes=VMEM_LIMIT),
        name="nsa_sample_b",
    )(idx, page_table, q, gate_sel, slopes8, o_cw, new_blocks, pool2d)


def _finish_kernel(x_ref, a_ref, r_ref, wo_ref, gn_ref, wr_ref, br_ref, h_ref, xn_ref, comb_ref):
    h = x_ref[...] + _mm(a_ref[...], wo_ref[0:D_ATTN, :]) + _mm(r_ref[...], wo_ref[D_ATTN:D_MODEL, :])
    h_ref[...] = h
    xn = _rms(h, gn_ref[...])
    xn_ref[...] = xn.astype(BF16)
    logit = _dot_f32(xn, wr_ref[...]) + br_ref[...]
    lane = lax.broadcasted_iota(jnp.int32, logit.shape, 1)
    lane_f = lane.astype(F32)
    is_g = (lane >= N_EXPERTS) & (lane < N_EXPERTS + N_GROUPS)
    gl = jnp.where(is_g, logit, NEG)
    g_max = jnp.max(gl, axis=-1, keepdims=True)
    g_star = jnp.min(jnp.where(gl == g_max, lane_f, 1e9), axis=-1, keepdims=True) - N_EXPERTS
    g_prob = 1.0 / jnp.sum(jnp.where(is_g, jnp.exp(gl - g_max), 0.0), axis=-1, keepdims=True)
    in_grp = (lane < N_EXPERTS) & ((lane // EXP_PER_GROUP).astype(F32) == g_star)
    el = jnp.where(in_grp, logit, NEG)
    e_max = jnp.max(el, axis=-1, keepdims=True)
    ee = jnp.where(in_grp, jnp.exp(el - e_max), 0.0)
    ep = jnp.where(in_grp, ee / jnp.sum(ee, axis=-1, keepdims=True), -1.0)
    p1 = jnp.max(ep, axis=-1, keepdims=True)
    i1 = jnp.min(jnp.where(ep == p1, lane_f, 1e9), axis=-1, keepdims=True)
    ep2 = jnp.where(lane_f == i1, -1.0, ep)
    p2 = jnp.max(ep2, axis=-1, keepdims=True)
    i2 = jnp.min(jnp.where(ep2 == p2, lane_f, 1e9), axis=-1, keepdims=True)
    tot = p1 + p2
    comb_ref[...] = (jnp.where(lane_f == i1, p1 / tot * g_prob, 0.0)
                     + jnp.where(lane_f == i2, p2 / tot * g_prob, 0.0)
                     + jnp.where(lane == GROUP_LANE, g_star, 0.0))


def _finish(x, attn_o, rnn_o, w_out, g_ffn, w_router, b_router, tm):
    n = x.shape[0]
    row = lambda i: (i, 0)
    fixed = lambda i: (0, 0)
    return pl.pallas_call(
        _finish_kernel,
        grid=(n // tm,),
        in_specs=[pl.BlockSpec((tm, D_MODEL), row), pl.BlockSpec((tm, D_ATTN), row), pl.BlockSpec((tm, D_RNN), row),
                  pl.BlockSpec((D_MODEL, D_MODEL), fixed, pipeline_mode=pl.Buffered(1)),
                  pl.BlockSpec((1, D_MODEL), fixed),
                  pl.BlockSpec((D_MODEL, LANES), fixed), pl.BlockSpec((1, LANES), fixed)],
        out_specs=[pl.BlockSpec((tm, D_MODEL), row), pl.BlockSpec((tm, D_MODEL), row), pl.BlockSpec((tm, LANES), row)],
        out_shape=[jax.ShapeDtypeStruct((n, D_MODEL), F32), jax.ShapeDtypeStruct((n, D_MODEL), BF16),
                   jax.ShapeDtypeStruct((n, LANES), F32)],
        compiler_params=pltpu.CompilerParams(dimension_semantics=("arbitrary",), vmem_limit_bytes=VMEM_LIMIT),
        name="finish",
    )(x, attn_o, rnn_o, w_out, g_ffn, w_router, b_router)


MOE_SUB = 128


MOE_SORT = 512


def _moe_kernel(sd, xn_ref, comb_ref, h_ref, wg_ref, wu_ref, wd_ref, gf_ref, y_ref, xs_scr, cs_scr, pt_scr, ends_smem):
    e = pl.program_id(1)
    tm = xn_ref.shape[0]
    domains = [slice(d * sd, (d + 1) * sd) for d in range(tm // sd)]

    @pl.when(e == 0)
    def _sort_rows():
        for d, dom in enumerate(domains):
            comb = comb_ref[dom, :]
            lane = lax.broadcasted_iota(jnp.int32, comb.shape, 1)
            grp = jnp.sum(jnp.where(lane == GROUP_LANE, comb, 0.0), axis=-1, keepdims=True)
            onehot = jnp.where(lane.astype(F32) == grp, 1.0, 0.0)
            earlier = jnp.where(lax.broadcasted_iota(jnp.int32, (sd, sd), 1)
                                < lax.broadcasted_iota(jnp.int32, (sd, sd), 0), 1.0, 0.0).astype(BF16)
            before = _dot(earlier, onehot.astype(BF16))
            rank = jnp.sum(onehot * before, axis=-1, keepdims=True)
            count = jnp.sum(onehot, axis=0, keepdims=True)
            end = jnp.int32(0)
            ends_smem[d, 0] = end
            for g in range(N_GROUPS):
                end = end + jnp.sum(jnp.where(lane[0:1] == g, count, 0.0)).astype(jnp.int32)
                ends_smem[d, g + 1] = end
            first = jnp.sum(jnp.where(lane.astype(F32) < grp, count, 0.0), axis=-1, keepdims=True)
            pos = first + rank
            to_sorted_t = jnp.where(lax.broadcasted_iota(jnp.int32, (sd, sd), 1).astype(F32) == pos, 1.0, 0.0)
            pt_scr[d] = to_sorted_t.astype(BF16)
            to_sorted = to_sorted_t.T.astype(BF16)
            xs_scr[dom, :] = _dot(to_sorted, xn_ref[dom, :]).astype(BF16)
            c1, c2, c3 = _split3(comb)
            cs_scr[dom, :] = (_dot(to_sorted, c3) + _dot(to_sorted, c2)) + _dot(to_sorted, c1)
        y_ref[...] = jnp.zeros(y_ref.shape, F32)

    grp_e = e // EXP_PER_GROUP
    for d in range(tm // sd):
        lo = ends_smem[d, grp_e]
        hi = ends_smem[d, grp_e + 1]
        start = lo // MOE_SUB * MOE_SUB
        n_blk = (hi - start + MOE_SUB - 1) // MOE_SUB
        for k in range(1, sd // MOE_SUB + 1):
            @pl.when(jnp.logical_and(hi > lo, n_blk == k))
            def _():
                rows = pl.ds(pl.multiple_of(d * sd + start, MOE_SUB), k * MOE_SUB)
                lane = lax.broadcasted_iota(jnp.int32, (k * MOE_SUB, LANES), 1)
                cw = jnp.sum(jnp.where(lane == e, cs_scr[rows, :], 0.0), axis=-1, keepdims=True)
                x = xs_scr[rows, :]
                hh = jax.nn.silu(_dot(x, wg_ref[0])) * _dot(x, wu_ref[0]) * cw
                y_ref[rows, :] += _dot(hh.astype(BF16), wd_ref[0])

    @pl.when(e == pl.num_programs(1) - 1)
    def _():
        for d, dom in enumerate(domains):
            a1, a2, _ = _split3(y_ref[dom, :])
            ffn = _dot(pt_scr[d], a2) + _dot(pt_scr[d], a1)
            y_ref[dom, :] = _rms(h_ref[dom, :] + ffn, gf_ref[...])


def _moe(xn, comb, h, w_gate, w_up, w_down, g_final, tm):
    n = xn.shape[0]
    sd = min(tm, MOE_SORT)
    row = lambda i, e: (i, 0)
    return pl.pallas_call(
        functools.partial(_moe_kernel, sd),
        grid=(n // tm, N_EXPERTS),
        in_specs=[pl.BlockSpec((tm, D_MODEL), row), pl.BlockSpec((tm, LANES), row),
                  pl.BlockSpec((tm, D_MODEL), row, pipeline_mode=pl.Buffered(1)),
                  pl.BlockSpec((1, D_MODEL, D_EXPERT), lambda i, e: (e, 0, 0)),
                  pl.BlockSpec((1, D_MODEL, D_EXPERT), lambda i, e: (e, 0, 0)),
                  pl.BlockSpec((1, D_EXPERT, D_MODEL), lambda i, e: (e, 0, 0)),
                  pl.BlockSpec((1, D_MODEL), lambda i, e: (0, 0))],
        out_specs=pl.BlockSpec((tm, D_MODEL), row),
        out_shape=jax.ShapeDtypeStruct((n, D_MODEL), F32),
        scratch_shapes=[pltpu.VMEM((tm, D_MODEL), BF16), pltpu.VMEM((tm, LANES), F32),
                        pltpu.VMEM((tm // sd, sd, sd), BF16), pltpu.SMEM((tm // sd, N_GROUPS + 1), jnp.int32)],
        compiler_params=pltpu.CompilerParams(dimension_semantics=("arbitrary", "arbitrary"),
                                             vmem_limit_bytes=VMEM_LIMIT),
        name="moe",
    )(xn, comb, h, w_gate, w_up, w_down, g_final)


def _block_diag_tiles(w):
    per = 256 // RNN_BLOCK_DIM
    w4 = w.reshape(RNN_BLOCKS // per, per, RNN_BLOCK_DIM, RNN_BLOCK_DIM)
    eye = jnp.eye(per, dtype=w.dtype)
    tiles = jnp.einsum('tpde,pq->tpdqe', w4, eye)
    return tiles.reshape(RNN_BLOCKS // per, 256, 256)


def _layer(l, xp, xs, cache_cmp_kv, cache_sel_kv, cache_win_kv, state_conv, state_h, page_table,
           norm_mix, w_in, cmp_pool_w, conv_w, conv_b, lru_wa, lru_ba, lru_wx, lru_bx, lru_lambda, w_out,
           norm_ffn, w_router_group, b_router_group, w_router_expert, b_router_expert,
           w_exp_gate, w_exp_up, w_exp_down, final_gain):
    bp, sp, _ = xp.shape
    bs, ts, _ = xs.shape
    n_pages = page_table.shape[1]
    past = n_pages * PAGE_SIZE

    wi = w_in[l]
    gt_cols = wi[:, 2560:2584].reshape(D_MODEL, N_KV, GQA * 3)
    gt_cols = jnp.pad(gt_cols, ((0, 0), (0, 0), (0, LANES - GQA * 3))).reshape(D_MODEL, N_KV * LANES)
    w_proj_f32 = jnp.concatenate([wi[:, :2560], wi[:, 2584:], gt_cols], axis=1)
    w_proj = w_proj_f32.astype(BF16)
    g_mix = norm_mix[l].reshape(1, D_MODEL)
    wfull = jnp.tile(jnp.repeat(cmp_pool_w[l], HEAD_DIM, axis=1), (1, N_KV))
    w2 = jnp.broadcast_to(cmp_pool_w[l].T[:, :, None], (2, CMP_BLOCK, HEAD_DIM))
    wa_f32 = _block_diag_tiles(lru_wa[l])
    wx_f32 = _block_diag_tiles(lru_wx[l])
    row = lambda v: v.reshape(1, -1)
    slopes = jnp.exp2(-8.0 * jnp.arange(1, N_HEADS + 1, dtype=F32) / N_HEADS)
    w_o = w_out[l].astype(BF16)
    w_router = jnp.pad(jnp.concatenate([w_router_expert[l], w_router_group[l]], axis=1),
                       ((0, 0), (0, LANES - N_EXPERTS - N_GROUPS)))
    b_router = jnp.pad(jnp.concatenate([b_router_expert[l], b_router_group[l]]),
                       (0, LANES - N_EXPERTS - N_GROUPS)).reshape(1, LANES)
    wg, wu, wd = w_exp_gate[l].astype(BF16), w_exp_up[l].astype(BF16), w_exp_down[l].astype(BF16)
    lru = lambda wa, wx: (conv_w[l], row(conv_b[l]), wa, row(lru_ba[l]), wx, row(lru_bx[l]), row(lru_lambda[l]))

    def tail(x, attn_o, rnn_o, w_out_l, tm_f, tm_m):
        h, xn, comb = _finish(x, attn_o, rnn_o, w_out_l, row(norm_ffn[l]), w_router, b_router, tm_f)
        return _moe(xn, comb, h, wg, wu, wd, final_gain, tm_m)

    np_ = bp * sp
    q, kc, ks, kw, xr, xg, gt = _project(xp.reshape(np_, D_MODEL), g_mix, w_proj, 256)
    shp = lambda a: a.reshape(bp, sp, a.shape[-1])
    ident = jnp.arange(np_ // 1024, dtype=jnp.int32)
    cmp_p = _compress(ident, kc.reshape(np_ // 1024, 1024, KV_W), wfull, bp, sp // 1024, 1, 1024)
    attn_p = _nsa_prompt(slopes, shp(q), shp(gt), cmp_p, shp(ks), shp(kw))
    rnn_p, h_p = _rglru(shp(xr), shp(xg), jnp.zeros((bp, 8, D_RNN), F32), jnp.zeros((bp, 1, D_RNN), F32),
                       *lru(wa_f32.astype(BF16), wx_f32.astype(BF16)), tc=512)
    y_p = tail(xp.reshape(np_, D_MODEL), attn_p.reshape(np_, D_ATTN), rnn_p.reshape(np_, D_RNN), w_o, 256, 1024)
    kv6 = lambda a, b_, t_: a.reshape(b_, t_, N_KV, 2, HEAD_DIM)
    outs_p = (y_p.reshape(bp, sp, D_MODEL), kv6(kc, bp, sp), kv6(ks, bp, sp),
              kv6(kw, bp, sp)[:, -min(WINDOW, sp):], shp(xr)[:, sp - (CONV_W - 1):], h_p.reshape(bp, D_RNN))

    ns_ = bs * ts
    q, kc, ks, kw, xr, xg, gt = _project_precise(xs.reshape(ns_, D_MODEL), g_mix, w_proj_f32)
    shs = lambda a: a.reshape(bs, ts, a.shape[-1])
    cmp_s = _compress_paged(page_table.reshape(-1), cache_cmp_kv[l].reshape(-1, HEAD_DIM), w2, bs, n_pages, 16)
    idx, o_cw = _nsa_sample_a(past, shs(q), shs(gt), cmp_s, cache_win_kv[l].reshape(-1, HEAD_DIM), shs(kw))
    idx = idx.reshape(bs, N_KV, ts, LANES)[..., :N_GATHER].reshape(-1)
    gate_sel = gt.reshape(ns_, N_KV, LANES)[:, :, :GQA * 3].reshape(ns_, N_HEADS, 3)[:, :, 1:2]
    gate_sel = jnp.broadcast_to(gate_sel, (ns_, N_HEADS, HEAD_DIM))
    slopes8 = jnp.broadcast_to(slopes.reshape(N_HEADS, 1), (N_HEADS, LANES))
    new_blocks = jnp.pad(ks.reshape(bs, ts * KV_ROWS, HEAD_DIM), ((0, 0), (0, BLOCK_ROWS - ts * KV_ROWS), (0, 0)))
    attn_s = _nsa_sample_b(past, idx, page_table.reshape(-1), q.reshape(ns_, N_HEADS, HEAD_DIM), gate_sel, slopes8,
                           o_cw.reshape(ns_, N_HEADS, HEAD_DIM), new_blocks, cache_sel_kv[l].reshape(-1, HEAD_DIM))
    conv_rows = jnp.pad(state_conv[l], ((0, 0), (ts - (CONV_W - 1), 0), (0, 0))).reshape(ns_, D_RNN)
    rnn_s, h_all = _rglru_short(xr, xg, conv_rows, jnp.repeat(state_h[l], ts, axis=0), *lru(wa_f32, wx_f32), t=ts)
    h_s = h_all.reshape(bs, ts, D_RNN)[:, -1]
    y_s = tail(xs.reshape(ns_, D_MODEL), attn_s.reshape(ns_, D_ATTN), rnn_s, w_out[l], ns_, ns_)
    win_s = jnp.concatenate([cache_win_kv[l], kv6(kw, bs, ts)], axis=1)[:, ts:]
    conv_s = jnp.concatenate([state_conv[l], shs(xr)], axis=1)[:, ts:]
    outs_s = (y_s.reshape(bs, ts, D_MODEL), kv6(kc, bs, ts), kv6(ks, bs, ts),
              win_s, conv_s, h_s.reshape(bs, D_RNN))
    return outs_p, outs_s


def kernel(x_prompt, x_sample, cache_cmp_kv, cache_sel_kv, cache_win_kv, state_conv, state_h, page_table, norm_mix, w_in, cmp_pool_w, conv_w, conv_b, lru_wa, lru_ba, lru_wx, lru_bx, lru_lambda, w_out, norm_ffn, w_router_group, b_router_group, w_router_expert, b_router_expert, w_exp_gate, w_exp_up, w_exp_down, norm_final):
    depth = w_in.shape[0]
    assert depth == 1, "the final norm is fused into the single layer's expert kernel"
    p, s = _layer(0, x_prompt, x_sample, cache_cmp_kv, cache_sel_kv, cache_win_kv, state_conv, state_h, page_table,
                  norm_mix, w_in, cmp_pool_w, conv_w, conv_b, lru_wa, lru_ba, lru_wx, lru_bx, lru_lambda, w_out,
                  norm_ffn, w_router_group, b_router_group, w_router_expert, b_router_expert,
                  w_exp_gate, w_exp_up, w_exp_down, norm_final.reshape(1, D_MODEL))
    st = lambda a: a[None]
    return (p[0], s[0], st(p[1]), st(s[1]), st(p[2]), st(s[2]), st(p[3]), st(s[3]),
            st(p[4]), st(s[4]), st(p[5]), st(s[5]))
```

```python
import functools

import jax
import jax.numpy as jnp
from jax import lax
from jax.experimental import pallas as pl
from jax.experimental.pallas import tpu as pltpu

F32 = jnp.float32
BF16 = jnp.bfloat16

D_MODEL = 2048
D_ATTN = 1024
D_RNN = 1024
N_HEADS = 8
HEAD_DIM = 128
N_KV = 2
GQA = 4
KV_W = 512
CMP_BLOCK = 32
SEL_BLOCK = 64
TOP_N = 16
WINDOW = 512
FORCE_SCORE = 1e4
RNN_BLOCKS = 16
RNN_BLOCK_DIM = 64
CONV_W = 4
LRU_C = 8.0
N_GROUPS = 4
EXP_PER_GROUP = 4
N_EXPERTS = 16
D_EXPERT = 512
RMS_EPS = 1e-6
PAGE_SIZE = 128

GROUP_LANE = N_EXPERTS
LANES = 128
NEG = -1e30
VMEM_LIMIT = 56 * 1024 * 1024

_SEG_Q = (0, 1024)
_SEG_KC = (1024, 1536)
_SEG_KS = (1536, 2048)
_SEG_KW = (2048, 2560)
_SEG_XR = (2560, 3584)
_SEG_XG = (3584, 4608)
_SEG_GT = (4608, 4864)
_PROJ_W = 4864


def _dot(a, b):
    return jnp.dot(a, b, preferred_element_type=F32)


def _dot_nt(a, b):
    return lax.dot_general(a, b, (((1,), (1,)), ((), ())), preferred_element_type=F32)


def _rms(x, g):
    return x * lax.rsqrt(jnp.mean(x * x, axis=-1, keepdims=True) + RMS_EPS) * g


def _split3(x):
    h1 = x.astype(BF16)
    r1 = x - h1.astype(F32)
    h2 = r1.astype(BF16)
    h3 = (r1 - h2.astype(F32)).astype(BF16)
    return h1, h2, h3


def _dot_f32(a, b, dot=_dot):
    a1, a2, a3 = _split3(a)
    b1, b2, b3 = _split3(b)
    return (dot(a3, b1) + dot(a2, b2) + dot(a1, b3)) + (dot(a2, b1) + dot(a1, b2)) + dot(a1, b1)


def _dot_x3(a, b, dot=_dot):
    a1, a2, _ = _split3(a)
    b1, b2, _ = _split3(b)
    return (dot(a2, b1) + dot(a1, b2)) + dot(a1, b1)


def _mm(a, w):
    if w.dtype == BF16:
        return _dot(a.astype(BF16), w)
    return _dot_x3(a, w)


def _dot_sel(a, m01):
    a1, a2, a3 = _split3(a)
    return (_dot(a3, m01) + _dot(a2, m01)) + _dot(a1, m01)


def _proj_kernel(x_ref, g_ref, w_ref, q_ref, kc_ref, ks_ref, kw_ref, xr_ref, xg_ref, gt_ref,
                 kc4_ref, ks4_ref, kw4_ref):
    xn = _rms(x_ref[...], g_ref[...]).astype(BF16)
    tm = x_ref.shape[0]
    outs = ((q_ref, _SEG_Q, None), (kc_ref, _SEG_KC, kc4_ref), (ks_ref, _SEG_KS, ks4_ref), (kw_ref, _SEG_KW, kw4_ref),
            (xr_ref, _SEG_XR, None), (xg_ref, _SEG_XG, None), (gt_ref, _SEG_GT, None))
    for ref, (lo, hi), cache_ref in outs:
        z = _dot(xn, w_ref[:, lo:hi])
        ref[...] = z
        if cache_ref is not None:
            for gc in range(KV_ROWS):
                cache_ref[pl.ds(gc, tm, stride=KV_ROWS), :] = z[:, gc * HEAD_DIM:(gc + 1) * HEAD_DIM]


def _project(x, g, w, tm):
    n = x.shape[0]
    widths = [hi - lo for lo, hi in (_SEG_Q, _SEG_KC, _SEG_KS, _SEG_KW, _SEG_XR, _SEG_XG, _SEG_GT)]
    return pl.pallas_call(
        _proj_kernel,
        grid=(n // tm,),
        in_specs=[pl.BlockSpec((tm, D_MODEL), lambda i: (i, 0)),
                  pl.BlockSpec((1, D_MODEL), lambda i: (0, 0)),
                  pl.BlockSpec((D_MODEL, _PROJ_W), lambda i: (0, 0), pipeline_mode=pl.Buffered(1))],
        out_specs=[pl.BlockSpec((tm, w_), lambda i: (i, 0)) for w_ in widths]
        + [pl.BlockSpec((tm * KV_ROWS, HEAD_DIM), lambda i: (i, 0))] * 3,
        out_shape=[jax.ShapeDtypeStruct((n, w_), F32) for w_ in widths]
        + [jax.ShapeDtypeStruct((n * KV_ROWS, HEAD_DIM), F32)] * 3,
        compiler_params=pltpu.CompilerParams(dimension_semantics=("arbitrary",), vmem_limit_bytes=VMEM_LIMIT),
        name="proj",
    )(x, g, w)


def _proj_precise_kernel(x_ref, g_ref, w_ref, z_ref, xn_scr):
    @pl.when(pl.program_id(0) == 0)
    def _():
        xn_scr[...] = _rms(x_ref[...], g_ref[...])

    z_ref[...] = _dot_x3(xn_scr[...], w_ref[...])


def _project_precise(x, g, w_f32):
    n = x.shape[0]
    tn = 2 * LANES
    z = pl.pallas_call(
        _proj_precise_kernel,
        grid=(_PROJ_W // tn,),
        in_specs=[pl.BlockSpec((n, D_MODEL), lambda j: (0, 0)),
                  pl.BlockSpec((1, D_MODEL), lambda j: (0, 0)),
                  pl.BlockSpec((D_MODEL, tn), lambda j: (0, j))],
        out_specs=pl.BlockSpec((n, tn), lambda j: (0, j)),
        out_shape=jax.ShapeDtypeStruct((n, _PROJ_W), F32),
        scratch_shapes=[pltpu.VMEM((n, D_MODEL), F32)],
        compiler_params=pltpu.CompilerParams(dimension_semantics=("arbitrary",), vmem_limit_bytes=VMEM_LIMIT),
        name="proj_precise",
    )(x, g, w_f32)
    return [z[:, lo:hi] for lo, hi in (_SEG_Q, _SEG_KC, _SEG_KS, _SEG_KW, _SEG_XR, _SEG_XG, _SEG_GT)]


def _compress_kernel(n_pages, pt_ref, *refs):
    del pt_ref
    pages, w_ref, o_ref = refs[:n_pages], refs[n_pages], refs[n_pages + 1]
    if n_pages == 1:
        x = pages[0][0]
    else:
        x = jnp.concatenate([p[0] for p in pages], axis=0)
    rows = x.shape[0]
    xb = x.reshape(rows // CMP_BLOCK, CMP_BLOCK, KV_W) * w_ref[...][None]
    o_ref[0] = jnp.sum(xb, axis=1)


def _compress(table, src, wfull, n_batch, steps, n_pages, page_rows):
    out_rows = n_pages * page_rows // CMP_BLOCK
    per_b = steps * n_pages

    def page_spec(k):
        return pl.BlockSpec((1, page_rows, KV_W), lambda b, j, pt: (pt[b * per_b + j * n_pages + k], 0, 0))

    return pl.pallas_call(
        functools.partial(_compress_kernel, n_pages),
        grid_spec=pltpu.PrefetchScalarGridSpec(
            num_scalar_prefetch=1,
            grid=(n_batch, steps),
            in_specs=[page_spec(k) for k in range(n_pages)] + [pl.BlockSpec((CMP_BLOCK, KV_W), lambda b, j, pt: (0, 0))],
            out_specs=pl.BlockSpec((1, out_rows, KV_W), lambda b, j, pt: (b, j, 0)),
        ),
        out_shape=jax.ShapeDtypeStruct((n_batch, steps * out_rows, KV_W), F32),
        compiler_params=pltpu.CompilerParams(dimension_semantics=("arbitrary", "arbitrary"), vmem_limit_bytes=VMEM_LIMIT),
        name="compress",
    )(table, *([src] * n_pages), wfull)


def _softplus(x):
    return jnp.maximum(x, 0.0) + jnp.log1p(jnp.exp(-jnp.abs(x)))


def _rglru_kernel(tc, xr_ref, xg_ref, cs_ref, h0_ref, cw_ref, cb_ref, wa_ref, ba_ref, wx_ref, bx_ref, lam_ref,
                  o_ref, hl_ref, tail_scr, h_scr):
    @pl.when(pl.program_id(2) == 0)
    def _():
        tail_scr[...] = cs_ref[0]
        h_scr[...] = h0_ref[0]

    x = xr_ref[0]
    xp = jnp.concatenate([tail_scr[...], x], axis=0)
    w = cw_ref[...]
    xc = cb_ref[...] + pltpu.roll(xp, 3, axis=0)[8:] * w[0:1]
    xc = xc + pltpu.roll(xp, 2, axis=0)[8:] * w[1:2]
    xc = xc + pltpu.roll(xp, 1, axis=0)[8:] * w[2:3]
    xc = xc + x * w[3:4]
    tail_scr[...] = xp[tc:]

    r = jax.nn.sigmoid(_mm(xc, wa_ref[0]) + ba_ref[...])
    gi = jax.nn.sigmoid(_mm(xc, wx_ref[0]) + bx_ref[...])
    log_a = -LRU_C * r * _softplus(-lam_ref[...])
    a = jnp.exp(log_a)
    u = jnp.sqrt(-jnp.tanh(log_a) * (a * a + 1.0)) * (gi * xc)

    row = lax.broadcasted_iota(jnp.int32, a.shape, 0)
    s = 1
    while s < tc:
        keep = row >= s
        a_sh = jnp.where(keep, pltpu.roll(a, s, axis=0), 1.0)
        u_sh = jnp.where(keep, pltpu.roll(u, s, axis=0), 0.0)
        u = a * u_sh + u
        a = a * a_sh
        s *= 2
    h = a * h_scr[...] + u
    h_last = h[tc - 1:tc]
    h_scr[...] = h_last
    hl_ref[0] = h_last
    o_ref[0] = h * jax.nn.gelu(xg_ref[0])


def _rglru(xr, xg, conv_state8, h0, conv_w, conv_b, wa_t, ba, wx_t, bx, lam, tc):
    b, t, _ = xr.shape
    ct = 256
    n_ct = D_RNN // ct
    seq = lambda bi, c, j: (bi, j, c)
    per_b = lambda bi, c, j: (bi, 0, c)
    par = lambda bi, c, j: (0, c)
    return pl.pallas_call(
        functools.partial(_rglru_kernel, tc),
        grid=(b, n_ct, t // tc),
        in_specs=[pl.BlockSpec((1, tc, ct), seq), pl.BlockSpec((1, tc, ct), seq),
                  pl.BlockSpec((1, 8, ct), per_b), pl.BlockSpec((1, 1, ct), per_b),
                  pl.BlockSpec((CONV_W, ct), par), pl.BlockSpec((1, ct), par),
                  pl.BlockSpec((1, ct, ct), lambda bi, c, j: (c, 0, 0)), pl.BlockSpec((1, ct), par),
                  pl.BlockSpec((1, ct, ct), lambda bi, c, j: (c, 0, 0)), pl.BlockSpec((1, ct), par),
                  pl.BlockSpec((1, ct), par)],
        out_specs=[pl.BlockSpec((1, tc, ct), seq), pl.BlockSpec((1, 1, ct), per_b)],
        out_shape=[jax.ShapeDtypeStruct((b, t, D_RNN), F32), jax.ShapeDtypeStruct((b, 1, D_RNN), F32)],
        scratch_shapes=[pltpu.VMEM((8, ct), F32), pltpu.VMEM((1, ct), F32)],
        compiler_params=pltpu.CompilerParams(dimension_semantics=("arbitrary", "arbitrary", "arbitrary"),
                                             vmem_limit_bytes=VMEM_LIMIT),
        name="rglru",
    )(xr, xg, conv_state8, h0, conv_w, conv_b, wa_t, ba, wx_t, bx, lam)


def _rglru_short_kernel(t, xr_ref, xg_ref, cs_ref, h0_ref, cw_ref, cb_ref, wa_ref, ba_ref, wx_ref, bx_ref, lam_ref,
                        o_ref, h_ref):
    x = xr_ref[...]
    st = cs_ref[...]
    n = x.shape[0]
    step = lax.broadcasted_iota(jnp.int32, x.shape, 0) % t
    w = cw_ref[...]
    xc = cb_ref[...] + x * w[CONV_W - 1:CONV_W]
    for sh in range(1, CONV_W):
        prev = jnp.where(step >= sh, pltpu.roll(x, sh, axis=0), pltpu.roll(st, n - (t - sh), axis=0))
        xc = xc + prev * w[CONV_W - 1 - sh:CONV_W - sh]
    r = jax.nn.sigmoid(_mm(xc, wa_ref[0]) + ba_ref[...])
    gi = jax.nn.sigmoid(_mm(xc, wx_ref[0]) + bx_ref[...])
    log_a = -LRU_C * r * _softplus(-lam_ref[...])
    a = jnp.exp(log_a)
    u = jnp.sqrt(-jnp.tanh(log_a) * (a * a + 1.0)) * (gi * xc)
    s = 1
    while s < t:
        keep = step >= s
        a_sh = jnp.where(keep, pltpu.roll(a, s, axis=0), 1.0)
        u_sh = jnp.where(keep, pltpu.roll(u, s, axis=0), 0.0)
        u = a * u_sh + u
        a = a * a_sh
        s *= 2
    h = a * h0_ref[...] + u
    h_ref[...] = h
    o_ref[...] = h * jax.nn.gelu(xg_ref[...])


def _rglru_short(xr, xg, conv_rows, h0_rows, conv_w, conv_b, wa_t, ba, wx_t, bx, lam, t):
    n = xr.shape[0]
    ct = 256
    blk = pl.BlockSpec((n, ct), lambda c: (0, c))
    par = lambda c: (0, c)
    tile = pl.BlockSpec((1, ct, ct), lambda c: (c, 0, 0))
    return pl.pallas_call(
        functools.partial(_rglru_short_kernel, t),
        grid=(D_RNN // ct,),
        in_specs=[blk, blk, blk, blk, pl.BlockSpec((CONV_W, ct), par), pl.BlockSpec((1, ct), par),
                  tile, pl.BlockSpec((1, ct), par), tile, pl.BlockSpec((1, ct), par), pl.BlockSpec((1, ct), par)],
        out_specs=[blk, blk],
        out_shape=[jax.ShapeDtypeStruct((n, D_RNN), F32), jax.ShapeDtypeStruct((n, D_RNN), F32)],
        compiler_params=pltpu.CompilerParams(dimension_semantics=("arbitrary",), vmem_limit_bytes=VMEM_LIMIT),
        name="rglru_short",
    )(xr, xg, conv_rows, h0_rows, conv_w, conv_b, wa_t, ba, wx_t, bx, lam)


TQ = 128
W4 = GQA * TQ
WIN_KEYS = WINDOW + TQ


def _nsa_prompt_kernel(slopes_ref, q_ref, gt_ref, kc_ref, vc_ref, ks_ref, vs_ref, kw_ref, vw_ref, o_ref,
                       q_scr, kcb, vct, ksb, vst, kwb, vwt, bias0, caus, wlow,
                       impt_scr, selb_scr, m_scr, l_scr, acc_scr, out_scr):
    g = pl.program_id(1)
    i = pl.program_id(2)
    s_len = ks_ref.shape[1]
    n_cmp = kc_ref.shape[1]
    n_sel = n_cmp // 2
    lane = lax.broadcasted_iota(jnp.int32, (1, W4), 1)
    tl_row = (lane % TQ).astype(F32)
    slope_row = jnp.full((1, W4), slopes_ref[g * GQA + GQA - 1], F32)
    for r in reversed(range(GQA - 1)):
        slope_row = jnp.where(lane < (r + 1) * TQ, slopes_ref[g * GQA + r], slope_row)

    @pl.when(i == 0)
    def _prepare():
        kcb[...] = kc_ref[0].astype(BF16)
        vct[...] = vc_ref[0].T.astype(BF16)

        def cast(j, carry):
            off = pl.multiple_of(j * LANES, LANES)
            ksb[pl.ds(off, LANES), :] = ks_ref[0, pl.ds(off, LANES), :].astype(BF16)
            kwb[pl.ds(off, LANES), :] = kw_ref[0, pl.ds(off, LANES), :].astype(BF16)
            vst[:, pl.ds(off, LANES)] = vs_ref[0, pl.ds(off, LANES), :].T.astype(BF16)
            vwt[:, pl.ds(off, LANES)] = vw_ref[0, pl.ds(off, LANES), :].T.astype(BF16)
            return carry

        lax.fori_loop(0, s_len // LANES, cast, 0)
        rel = tl_row - lax.broadcasted_iota(jnp.int32, (WIN_KEYS, W4), 0).astype(F32)
        bias0[...] = slope_row * rel
        caus[...] = jnp.where(rel[0:LANES] >= 0, 0.0, NEG)
        wlow[...] = jnp.where(rel[0:LANES] <= 0, 0.0, NEG)

    start_f = (i * TQ).astype(F32)
    scale = HEAD_DIM ** -0.5
    for r in range(GQA):
        q_scr[r * TQ:(r + 1) * TQ, :] = (q_ref[0, :, r * HEAD_DIM:(r + 1) * HEAD_DIM] * scale).astype(BF16)
    gate_t = jax.nn.sigmoid(gt_ref[0]).T

    def gate_row(branch):
        return jnp.concatenate([gate_t[3 * r + branch:3 * r + branch + 1, :] for r in range(GQA)], axis=1)

    t_row = start_f + tl_row
    c_end = ((lax.broadcasted_iota(jnp.int32, (n_cmp, W4), 0) + 1) * CMP_BLOCK - 1).astype(F32)
    dist_c = t_row - c_end
    ok_c = dist_c >= 0
    x = jnp.where(ok_c, _dot_nt(kcb[...], q_scr[...]) - slope_row * dist_c, NEG)
    e = jnp.where(ok_c, jnp.exp(x - jnp.max(x, axis=0, keepdims=True)), 0.0)
    p = e * (1.0 / jnp.maximum(jnp.sum(e, axis=0, keepdims=True), 1e-30))
    out_scr[...] = gate_row(0) * _dot(vct[...], p.astype(BF16))

    p_heads = p[:, 0:TQ]
    for r in range(1, GQA):
        p_heads = p_heads + p[:, r * TQ:(r + 1) * TQ]
    impt_scr[...] = p_heads
    imp = impt_scr[pl.ds(0, n_sel, stride=2), :] + impt_scr[pl.ds(1, n_sel, stride=2), :]
    blk = lax.broadcasted_iota(jnp.int32, (n_sel, TQ), 0)
    cur = (i * TQ + lax.broadcasted_iota(jnp.int32, (n_sel, TQ), 1)) // SEL_BLOCK
    forced = (blk == 0) | (blk == cur) | (blk == cur - 1)
    imp = jnp.where(forced, FORCE_SCORE, imp)
    imp = jnp.where(blk > cur, NEG, imp)
    rank = jnp.zeros((n_sel, TQ), jnp.int32)
    for j in range(n_sel):
        row = imp[j:j + 1, :]
        tie = jnp.where(blk > j, jnp.where(row == imp, 1, 0), 0)
        rank = rank + jnp.where(row > imp, 1, tie)
    selb = jnp.where(rank < TOP_N, 0.0, NEG)
    for j in range(n_sel):
        selb_scr[j] = selb[j:j + 1, :]

    def scores(k_b, c, nk):
        off = pl.multiple_of(c * LANES, LANES)
        return _dot_nt(k_b[pl.ds(off, nk), :], q_scr[...]) - bias0[0:nk, :]

    def block_mask(c, nk):
        rows = [jnp.broadcast_to(selb_scr[2 * c + k], (SEL_BLOCK, TQ)) for k in range(nk // SEL_BLOCK)]
        return jnp.concatenate([jnp.concatenate(rows, axis=0)] * GQA, axis=1)

    def update(v_t, c, nk, x):
        off = pl.multiple_of(c * LANES, LANES)
        r_c = slope_row * ((c * LANES).astype(F32) - start_f)
        m_old = m_scr[...]
        m_new = jnp.maximum(m_old, jnp.max(x, axis=0, keepdims=True) + r_c)
        pr = jnp.exp(x + (r_c - m_new))
        alpha = jnp.exp(m_old - m_new)
        l_scr[...] = alpha * l_scr[...] + jnp.sum(pr, axis=0, keepdims=True)
        acc_scr[...] = alpha * acc_scr[...] + _dot(v_t[:, pl.ds(off, nk)], pr.astype(BF16))
        m_scr[...] = m_new

    def reset():
        m_scr[...] = jnp.full((1, W4), NEG, F32)
        l_scr[...] = jnp.zeros((1, W4), F32)
        acc_scr[...] = jnp.zeros((HEAD_DIM, W4), F32)

    def result():
        return acc_scr[...] * (1.0 / jnp.maximum(l_scr[...], 1e-30))

    reset()
    n_big = i // 4

    def big(c4, carry):
        update(vst, 4 * c4, 4 * LANES, scores(ksb, 4 * c4, 4 * LANES) + block_mask(4 * c4, 4 * LANES))
        return carry

    lax.fori_loop(0, n_big, big, 0)
    rem = i - 4 * n_big

    @pl.when(rem >= 2)
    def _():
        update(vst, 4 * n_big, 2 * LANES, scores(ksb, 4 * n_big, 2 * LANES) + block_mask(4 * n_big, 2 * LANES))

    @pl.when(rem % 2 == 1)
    def _():
        update(vst, i - 1, LANES, scores(ksb, i - 1, LANES) + block_mask(i - 1, LANES))

    update(vst, i, LANES, scores(ksb, i, LANES) + block_mask(i, LANES) + caus[...])
    out_scr[...] += gate_row(1) * result()

    n_back = WINDOW // LANES

    @pl.when(i >= n_back)
    def _():
        c = i - n_back
        x = scores(kwb, c, WIN_KEYS)
        x = jnp.concatenate([x[0:LANES] + wlow[...], x[LANES:WINDOW], x[WINDOW:WIN_KEYS] + caus[...]], axis=0)
        pr = jnp.exp(x - jnp.max(x, axis=0, keepdims=True))
        o_w = _dot(vwt[:, pl.ds(pl.multiple_of(c * LANES, LANES), WIN_KEYS)], pr.astype(BF16))
        out_scr[...] += gate_row(2) * (o_w * (1.0 / jnp.maximum(jnp.sum(pr, axis=0, keepdims=True), 1e-30)))

    @pl.when(i < n_back)
    def _():
        reset()

        def body(c, carry):
            update(vwt, c, LANES, scores(kwb, c, LANES))
            return carry

        lax.fori_loop(0, i, body, 0)
        update(vwt, i, LANES, scores(kwb, i, LANES) + caus[...])
        out_scr[...] += gate_row(2) * result()

    o_t = out_scr[...]
    for r in range(GQA):
        o_ref[0, :, r * HEAD_DIM:(r + 1) * HEAD_DIM] = o_t[:, r * TQ:(r + 1) * TQ].T


def _nsa_prompt(slopes, q, gt, cmp_kv, ks, kw):
    b, s, _ = q.shape
    n_cmp = cmp_kv.shape[1]
    k_of = lambda bi, g, i: (bi, 0, 2 * g)
    v_of = lambda bi, g, i: (bi, 0, 2 * g + 1)
    tile = pltpu.VMEM((LANES, W4), F32)
    return pl.pallas_call(
        _nsa_prompt_kernel,
        grid=(b, N_KV, s // TQ),
        in_specs=[pl.BlockSpec(memory_space=pltpu.SMEM),
                  pl.BlockSpec((1, TQ, GQA * HEAD_DIM), lambda bi, g, i: (bi, i, g)),
                  pl.BlockSpec((1, TQ, LANES), lambda bi, g, i: (bi, i, g)),
                  pl.BlockSpec((1, n_cmp, HEAD_DIM), k_of), pl.BlockSpec((1, n_cmp, HEAD_DIM), v_of),
                  pl.BlockSpec((1, s, HEAD_DIM), k_of), pl.BlockSpec((1, s, HEAD_DIM), v_of),
                  pl.BlockSpec((1, s, HEAD_DIM), k_of), pl.BlockSpec((1, s, HEAD_DIM), v_of)],
        out_specs=pl.BlockSpec((1, TQ, GQA * HEAD_DIM), lambda bi, g, i: (bi, i, g)),
        out_shape=jax.ShapeDtypeStruct((b, s, D_ATTN), F32),
        scratch_shapes=[pltpu.VMEM((W4, HEAD_DIM), BF16),
                        pltpu.VMEM((n_cmp, HEAD_DIM), BF16), pltpu.VMEM((HEAD_DIM, n_cmp), BF16),
                        pltpu.VMEM((s, HEAD_DIM), BF16), pltpu.VMEM((HEAD_DIM, s), BF16),
                        pltpu.VMEM((s, HEAD_DIM), BF16), pltpu.VMEM((HEAD_DIM, s), BF16),
                        pltpu.VMEM((WIN_KEYS, W4), F32), tile, tile,
                        pltpu.VMEM((n_cmp, TQ), F32),
                        pltpu.VMEM((n_cmp // 2, 1, TQ), F32),
                        pltpu.VMEM((1, W4), F32), pltpu.VMEM((1, W4), F32),
                        pltpu.VMEM((HEAD_DIM, W4), F32), pltpu.VMEM((HEAD_DIM, W4), F32)],
        compiler_params=pltpu.CompilerParams(dimension_semantics=("arbitrary", "arbitrary", "arbitrary"),
                                             vmem_limit_bytes=VMEM_LIMIT),
        name="nsa_prompt",
    )(slopes, q, gt, cmp_kv, cmp_kv, ks, ks, kw, kw)


KV_ROWS = 2 * N_KV


def _compress_paged_kernel(n_pages, pt_ref, *refs):
    del pt_ref
    pages, w_ref, o_ref = refs[:n_pages], refs[n_pages], refs[n_pages + 1]
    for gc in range(KV_ROWS):
        x = jnp.concatenate([p[pl.ds(gc, PAGE_SIZE, stride=KV_ROWS), :] for p in pages], axis=0)
        xb = x.reshape(n_pages * PAGE_SIZE // CMP_BLOCK, CMP_BLOCK, HEAD_DIM) * w_ref[gc % 2][None]
        o_ref[0, gc] = jnp.sum(xb, axis=1)


def _compress_paged(table, pool2d, w2, n_batch, pages_per_b, n_pages):
    steps = pages_per_b // n_pages
    out_rows = n_pages * PAGE_SIZE // CMP_BLOCK
    rows = PAGE_SIZE * KV_ROWS

    def page_spec(k):
        return pl.BlockSpec((rows, HEAD_DIM), lambda b, j, pt: (pt[b * pages_per_b + j * n_pages + k], 0))

    return pl.pallas_call(
        functools.partial(_compress_paged_kernel, n_pages),
        grid_spec=pltpu.PrefetchScalarGridSpec(
            num_scalar_prefetch=1,
            grid=(n_batch, steps),
            in_specs=[page_spec(k) for k in range(n_pages)]
            + [pl.BlockSpec((2, CMP_BLOCK, HEAD_DIM), lambda b, j, pt: (0, 0, 0))],
            out_specs=pl.BlockSpec((1, KV_ROWS, out_rows, HEAD_DIM), lambda b, j, pt: (b, 0, j, 0)),
        ),
        out_shape=jax.ShapeDtypeStruct((n_batch, KV_ROWS, steps * out_rows, HEAD_DIM), F32),
        compiler_params=pltpu.CompilerParams(dimension_semantics=("arbitrary", "arbitrary"), vmem_limit_bytes=VMEM_LIMIT),
        name="compress_paged",
    )(table, *([pool2d] * n_pages), w2)


def _nsa_sample_a_kernel(past, q_ref, gt_ref, cmp_ref, cw_ref, kwn_ref, idx_ref, o_ref):
    t_new = q_ref.shape[1]
    n_cmp = cmp_ref.shape[2]
    n_past_sel = n_cmp // 2
    n_win = cw_ref.shape[0] // KV_ROWS
    rows = GQA * t_new
    gates = jax.nn.sigmoid(gt_ref[0])
    t_row = past + lax.broadcasted_iota(jnp.int32, (rows, 1), 0) % t_new
    scale = HEAD_DIM ** -0.5
    pair = jnp.where(lax.broadcasted_iota(jnp.int32, (n_cmp, n_past_sel), 0) // 2
                     == lax.broadcasted_iota(jnp.int32, (n_cmp, n_past_sel), 1), 1.0, 0.0).astype(BF16)
    imps = []
    for g in range(N_KV):
        qg = jnp.concatenate([q_ref[0, :, (g * GQA + r) * HEAD_DIM:(g * GQA + r + 1) * HEAD_DIM]
                              for r in range(GQA)], axis=0)
        qg = qg * scale
        slope = jnp.concatenate([jnp.full((t_new, 1), 2.0 ** -(g * GQA + r + 1), F32) for r in range(GQA)], axis=0)
        kcol = g * 2 * HEAD_DIM
        kc = cmp_ref[0, 2 * g]
        vc = cmp_ref[0, 2 * g + 1]
        c_end = (lax.broadcasted_iota(jnp.int32, (1, n_cmp), 1) + 1) * CMP_BLOCK - 1
        dist_c = t_row - c_end
        mask_c = dist_c >= 0
        s = _dot_x3(qg, kc, _dot_nt) - slope * dist_c.astype(F32)
        s = jnp.where(mask_c, s, NEG)
        m = jnp.max(s, axis=-1, keepdims=True)
        p = jnp.where(mask_c, jnp.exp(s - m), 0.0)
        p = p / jnp.maximum(jnp.sum(p, axis=-1, keepdims=True), 1e-30)
        o_c = _dot_x3(p, vc)
        p_heads = p[0:t_new]
        for r in range(1, GQA):
            p_heads = p_heads + p[r * t_new:(r + 1) * t_new]
        imps.append(_dot_sel(p_heads, pair))
        n_pad = LANES - t_new
        kw = jnp.concatenate([cw_ref[pl.ds(2 * g, n_win, stride=KV_ROWS), :], kwn_ref[0, :, kcol:kcol + HEAD_DIM],
                              jnp.zeros((n_pad, HEAD_DIM), F32)], axis=0)
        vw = jnp.concatenate([cw_ref[pl.ds(2 * g + 1, n_win, stride=KV_ROWS), :],
                              kwn_ref[0, :, kcol + HEAD_DIM:kcol + 2 * HEAD_DIM],
                              jnp.zeros((n_pad, HEAD_DIM), F32)], axis=0)
        win_pos = past - n_win + lax.broadcasted_iota(jnp.int32, (1, n_win + LANES), 1)
        dist_w = t_row - win_pos
        mask_w = (dist_w >= 0) & (dist_w <= WINDOW)
        s = _dot_x3(qg, kw, _dot_nt) - slope * dist_w.astype(F32)
        s = jnp.where(mask_w, s, NEG)
        m = jnp.max(s, axis=-1, keepdims=True)
        p = jnp.where(mask_w, jnp.exp(s - m), 0.0)
        p = p / jnp.maximum(jnp.sum(p, axis=-1, keepdims=True), 1e-30)
        o_w = _dot_x3(p, vw)
        for r in range(GQA):
            h = g * GQA + r
            gl = g * LANES + 3 * r
            o_ref[0, :, h * HEAD_DIM:(h + 1) * HEAD_DIM] = (
                gates[:, gl:gl + 1] * o_c[r * t_new:(r + 1) * t_new]
                + gates[:, gl + 2:gl + 3] * o_w[r * t_new:(r + 1) * t_new])

    imp = jnp.concatenate(imps, axis=0)
    n_rows = N_KV * t_new
    lane = lax.broadcasted_iota(jnp.int32, (n_rows, n_past_sel), 1)
    lane_f = lane.astype(F32)
    cur = (past + lax.broadcasted_iota(jnp.int32, (n_rows, n_past_sel), 0) % t_new) // SEL_BLOCK
    forced = (lane == 0) | (lane == cur) | (lane == cur - 1)
    imp = jnp.where(forced, FORCE_SCORE, imp)
    imp = jnp.where(lane > cur, NEG, imp)
    out_lane = lax.broadcasted_iota(jnp.int32, (n_rows, LANES), 1)
    idx = jnp.zeros((n_rows, LANES), F32)
    for k in range(TOP_N - 1):
        m = jnp.max(imp, axis=-1, keepdims=True)
        j = jnp.min(jnp.where(imp == m, lane_f, float(n_past_sel)), axis=-1, keepdims=True)
        idx = jnp.where(out_lane == k, j, idx)
        imp = jnp.where(lane_f == j, -3e38, imp)
    idx_ref[0] = idx.astype(jnp.int32)


def _nsa_sample_a(past, q, gt, cmp_kv, cache_w2d, kw_new):
    b, t_new, _ = q.shape
    win_rows = cache_w2d.shape[0] // b
    per_b3 = lambda bi: (bi, 0, 0)
    return pl.pallas_call(
        functools.partial(_nsa_sample_a_kernel, past),
        grid=(b,),
        in_specs=[pl.BlockSpec((1, t_new, D_ATTN), per_b3), pl.BlockSpec((1, t_new, 2 * LANES), per_b3),
                  pl.BlockSpec((1,) + cmp_kv.shape[1:], lambda bi: (bi, 0, 0, 0)),
                  pl.BlockSpec((win_rows, HEAD_DIM), lambda bi: (bi, 0)),
                  pl.BlockSpec((1, t_new, KV_W), per_b3)],
        out_specs=[pl.BlockSpec((1, N_KV * t_new, LANES), per_b3), pl.BlockSpec((1, t_new, D_ATTN), per_b3)],
        out_shape=[jax.ShapeDtypeStruct((b, N_KV * t_new, LANES), jnp.int32),
                   jax.ShapeDtypeStruct((b, t_new, D_ATTN), F32)],
        compiler_params=pltpu.CompilerParams(dimension_semantics=("arbitrary",), vmem_limit_bytes=VMEM_LIMIT),
        name="nsa_sample_a",
    )(q, gt, cmp_kv, cache_w2d, kw_new)


N_GATHER = TOP_N - 1
BLOCK_ROWS = SEL_BLOCK * KV_ROWS
GATHER_KEYS = TOP_N * SEL_BLOCK


def _nsa_sample_b_kernel(past, t_new, idx_ref, pt_ref, q_ref, gs_ref, slope_ref, ocw_ref, new_ref, pool_ref,
                         o_ref, kv_buf, sems):
    n_steps = pl.num_programs(0) * t_new
    step = pl.program_id(0) * t_new + pl.program_id(1)
    slot = step % 2
    pages_per_b = past // PAGE_SIZE
    sel_per_page = PAGE_SIZE // SEL_BLOCK

    def block_index(st, g, k):
        return idx_ref[((st // t_new * N_KV + g) * t_new + st % t_new) * N_GATHER + k]

    def copies(st, sl):
        out = []
        for g in range(N_KV):
            for k in range(N_GATHER):
                lp = block_index(st, g, k)
                phys = pt_ref[st // t_new * pages_per_b + lp // sel_per_page] * sel_per_page + lp % sel_per_page
                out.append(pltpu.make_async_copy(
                    pool_ref.at[pl.ds(pl.multiple_of(phys * BLOCK_ROWS, BLOCK_ROWS), BLOCK_ROWS), :],
                    kv_buf.at[sl, g, pl.ds(k * BLOCK_ROWS, BLOCK_ROWS), :], sems.at[sl, g, k]))
            out.append(pltpu.make_async_copy(
                new_ref.at[st // t_new], kv_buf.at[sl, g, pl.ds(N_GATHER * BLOCK_ROWS, BLOCK_ROWS), :],
                sems.at[sl, g, N_GATHER]))
        return out

    @pl.when(step == 0)
    def _():
        for cp in copies(step, slot):
            cp.start()

    @pl.when(step + 1 < n_steps)
    def _():
        for cp in copies(step + 1, 1 - slot):
            cp.start()

    t_pos = past + pl.program_id(1)
    q8 = q_ref[0] * HEAD_DIM ** -0.5
    slope = slope_ref[:, 0:1]
    lane = lax.broadcasted_iota(jnp.int32, (1, GATHER_KEYS), 1)
    for cp in copies(step, slot):
        cp.wait()
    outs = []
    for g in range(N_KV):
        pos = past + (lane - N_GATHER * SEL_BLOCK)
        for k in range(N_GATHER):
            pos = jnp.where(lane // SEL_BLOCK == k, block_index(step, g, k) * SEL_BLOCK + lane % SEL_BLOCK, pos)
        kk = kv_buf[slot, g, pl.ds(2 * g, GATHER_KEYS, stride=KV_ROWS), :]
        vv = kv_buf[slot, g, pl.ds(2 * g + 1, GATHER_KEYS, stride=KV_ROWS), :]
        dist = t_pos - pos
        mask = dist >= 0
        s = _dot_x3(q8, kk, _dot_nt) - slope * dist.astype(F32)
        s = jnp.where(mask, s, NEG)
        m = jnp.max(s, axis=-1, keepdims=True)
        p = jnp.where(mask, jnp.exp(s - m), 0.0)
        p = p / jnp.maximum(jnp.sum(p, axis=-1, keepdims=True), 1e-30)
        outs.append(_dot_x3(p, vv))
    head = lax.broadcasted_iota(jnp.int32, (N_HEADS, HEAD_DIM), 0)
    o_s = jnp.where(head < GQA, outs[0], outs[1])
    o_ref[0] = ocw_ref[0] + jax.nn.sigmoid(gs_ref[0]) * o_s


def _nsa_sample_b(past, idx, page_table, q, gate_sel, slopes8, o_cw, new_blocks, pool2d):
    n_tok = q.shape[0]
    b = new_blocks.shape[0]
    t_new = n_tok // b
    tok = lambda bi, ti, *_: (bi * t_new + ti, 0, 0)
    return pl.pallas_call(
        functools.partial(_nsa_sample_b_kernel, past, t_new),
        grid_spec=pltpu.PrefetchScalarGridSpec(
            num_scalar_prefetch=2,
            grid=(b, t_new),
            in_specs=[pl.BlockSpec((1, N_HEADS, HEAD_DIM), tok), pl.BlockSpec((1, N_HEADS, HEAD_DIM), tok),
                      pl.BlockSpec((N_HEADS, LANES), lambda bi, ti, *_: (0, 0)),
                      pl.BlockSpec((1, N_HEADS, HEAD_DIM), tok),
                      pl.BlockSpec(memory_space=pl.ANY), pl.BlockSpec(memory_space=pl.ANY)],
            out_specs=pl.BlockSpec((1, N_HEADS, HEAD_DIM), tok),
            scratch_shapes=[pltpu.VMEM((2, N_KV, TOP_N * BLOCK_ROWS, HEAD_DIM), F32),
                            pltpu.SemaphoreType.DMA((2, N_KV, TOP_N))],
        ),
        out_shape=jax.ShapeDtypeStruct((n_tok, N_HEADS, HEAD_DIM), F32),
        compiler_params=pltpu.CompilerParams(dimension_semantics=("arbitrary", "arbitrary"),
                                             vmem_limit_bytes=VMEM_LIMIT),
        name="nsa_sample_b",
    )(idx, page_table, q, gate_sel, slopes8, o_cw, new_blocks, pool2d)


def _finish_kernel(x_ref, a_ref, r_ref, wo_ref, gn_ref, wr_ref, br_ref, h_ref, xn_ref, comb_ref):
    h = x_ref[...] + _mm(a_ref[...], wo_ref[0:D_ATTN, :]) + _mm(r_ref[...], wo_ref[D_ATTN:D_MODEL, :])
    h_ref[...] = h
    xn = _rms(h, gn_ref[...])
    xn_ref[...] = xn.astype(BF16)
    logit = _dot_f32(xn, wr_ref[...]) + br_ref[...]
    lane = lax.broadcasted_iota(jnp.int32, logit.shape, 1)
    lane_f = lane.astype(F32)
    is_g = (lane >= N_EXPERTS) & (lane < N_EXPERTS + N_GROUPS)
    gl = jnp.where(is_g, logit, NEG)
    g_max = jnp.max(gl, axis=-1, keepdims=True)
    g_star = jnp.min(jnp.where(gl == g_max, lane_f, 1e9), axis=-1, keepdims=True) - N_EXPERTS
    g_prob = 1.0 / jnp.sum(jnp.where(is_g, jnp.exp(gl - g_max), 0.0), axis=-1, keepdims=True)
    in_grp = (lane < N_EXPERTS) & ((lane // EXP_PER_GROUP).astype(F32) == g_star)
    el = jnp.where(in_grp, logit, NEG)
    e_max = jnp.max(el, axis=-1, keepdims=True)
    ee = jnp.where(in_grp, jnp.exp(el - e_max), 0.0)
    ep = jnp.where(in_grp, ee / jnp.sum(ee, axis=-1, keepdims=True), -1.0)
    p1 = jnp.max(ep, axis=-1, keepdims=True)
    i1 = jnp.min(jnp.where(ep == p1, lane_f, 1e9), axis=-1, keepdims=True)
    ep2 = jnp.where(lane_f == i1, -1.0, ep)
    p2 = jnp.max(ep2, axis=-1, keepdims=True)
    i2 = jnp.min(jnp.where(ep2 == p2, lane_f, 1e9), axis=-1, keepdims=True)
    tot = p1 + p2
    comb_ref[...] = (jnp.where(lane_f == i1, p1 / tot * g_prob, 0.0)
                     + jnp.where(lane_f == i2, p2 / tot * g_prob, 0.0)
                     + jnp.where(lane == GROUP_LANE, g_star, 0.0))


def _finish(x, attn_o, rnn_o, w_out, g_ffn, w_router, b_router, tm):
    n = x.shape[0]
    row = lambda i: (i, 0)
    fixed = lambda i: (0, 0)
    return pl.pallas_call(
        _finish_kernel,
        grid=(n // tm,),
        in_specs=[pl.BlockSpec((tm, D_MODEL), row), pl.BlockSpec((tm, D_ATTN), row), pl.BlockSpec((tm, D_RNN), row),
                  pl.BlockSpec((D_MODEL, D_MODEL), fixed, pipeline_mode=pl.Buffered(1)),
                  pl.BlockSpec((1, D_MODEL), fixed),
                  pl.BlockSpec((D_MODEL, LANES), fixed), pl.BlockSpec((1, LANES), fixed)],
        out_specs=[pl.BlockSpec((tm, D_MODEL), row), pl.BlockSpec((tm, D_MODEL), row), pl.BlockSpec((tm, LANES), row)],
        out_shape=[jax.ShapeDtypeStruct((n, D_MODEL), F32), jax.ShapeDtypeStruct((n, D_MODEL), BF16),
                   jax.ShapeDtypeStruct((n, LANES), F32)],
        compiler_params=pltpu.CompilerParams(dimension_semantics=("arbitrary",), vmem_limit_bytes=VMEM_LIMIT),
        name="finish",
    )(x, attn_o, rnn_o, w_out, g_ffn, w_router, b_router)


MOE_SUB = 128


MOE_SORT = 512


def _moe_kernel(sd, xn_ref, comb_ref, h_ref, wg_ref, wu_ref, wd_ref, gf_ref, y_ref, xs_scr, cs_scr, pt_scr, ends_smem):
    e = pl.program_id(1)
    tm = xn_ref.shape[0]
    domains = [slice(d * sd, (d + 1) * sd) for d in range(tm // sd)]

    @pl.when(e == 0)
    def _sort_rows():
        for d, dom in enumerate(domains):
            comb = comb_ref[dom, :]
            lane = lax.broadcasted_iota(jnp.int32, comb.shape, 1)
            grp = jnp.sum(jnp.where(lane == GROUP_LANE, comb, 0.0), axis=-1, keepdims=True)
            onehot = jnp.where(lane.astype(F32) == grp, 1.0, 0.0)
            earlier = jnp.where(lax.broadcasted_iota(jnp.int32, (sd, sd), 1)
                                < lax.broadcasted_iota(jnp.int32, (sd, sd), 0), 1.0, 0.0).astype(BF16)
            before = _dot(earlier, onehot.astype(BF16))
            rank = jnp.sum(onehot * before, axis=-1, keepdims=True)
            count = jnp.sum(onehot, axis=0, keepdims=True)
            end = jnp.int32(0)
            ends_smem[d, 0] = end
            for g in range(N_GROUPS):
                end = end + jnp.sum(jnp.where(lane[0:1] == g, count, 0.0)).astype(jnp.int32)
                ends_smem[d, g + 1] = end
            first = jnp.sum(jnp.where(lane.astype(F32) < grp, count, 0.0), axis=-1, keepdims=True)
            pos = first + rank
            to_sorted_t = jnp.where(lax.broadcasted_iota(jnp.int32, (sd, sd), 1).astype(F32) == pos, 1.0, 0.0)
            pt_scr[d] = to_sorted_t.astype(BF16)
            to_sorted = to_sorted_t.T.astype(BF16)
            xs_scr[dom, :] = _dot(to_sorted, xn_ref[dom, :]).astype(BF16)
            c1, c2, c3 = _split3(comb)
            cs_scr[dom, :] = (_dot(to_sorted, c3) + _dot(to_sorted, c2)) + _dot(to_sorted, c1)
        y_ref[...] = jnp.zeros(y_ref.shape, F32)

    grp_e = e // EXP_PER_GROUP
    for d in range(tm // sd):
        lo = ends_smem[d, grp_e]
        hi = ends_smem[d, grp_e + 1]
        start = lo // MOE_SUB * MOE_SUB
        n_blk = (hi - start + MOE_SUB - 1) // MOE_SUB
        for k in range(1, sd // MOE_SUB + 1):
            @pl.when(jnp.logical_and(hi > lo, n_blk == k))
            def _():
                rows = pl.ds(pl.multiple_of(d * sd + start, MOE_SUB), k * MOE_SUB)
                lane = lax.broadcasted_iota(jnp.int32, (k * MOE_SUB, LANES), 1)
                cw = jnp.sum(jnp.where(lane == e, cs_scr[rows, :], 0.0), axis=-1, keepdims=True)
                x = xs_scr[rows, :]
                hh = jax.nn.silu(_dot(x, wg_ref[0])) * _dot(x, wu_ref[0]) * cw
                y_ref[rows, :] += _dot(hh.astype(BF16), wd_ref[0])

    @pl.when(e == pl.num_programs(1) - 1)
    def _():
        for d, dom in enumerate(domains):
            a1, a2, _ = _split3(y_ref[dom, :])
            ffn = _dot(pt_scr[d], a2) + _dot(pt_scr[d], a1)
            y_ref[dom, :] = _rms(h_ref[dom, :] + ffn, gf_ref[...])


def _moe(xn, comb, h, w_gate, w_up, w_down, g_final, tm):
    n = xn.shape[0]
    sd = min(tm, MOE_SORT)
    row = lambda i, e: (i, 0)
    return pl.pallas_call(
        functools.partial(_moe_kernel, sd),
        grid=(n // tm, N_EXPERTS),
        in_specs=[pl.BlockSpec((tm, D_MODEL), row), pl.BlockSpec((tm, LANES), row),
                  pl.BlockSpec((tm, D_MODEL), row, pipeline_mode=pl.Buffered(1)),
                  pl.BlockSpec((1, D_MODEL, D_EXPERT), lambda i, e: (e, 0, 0)),
                  pl.BlockSpec((1, D_MODEL, D_EXPERT), lambda i, e: (e, 0, 0)),
                  pl.BlockSpec((1, D_EXPERT, D_MODEL), lambda i, e: (e, 0, 0)),
                  pl.BlockSpec((1, D_MODEL), lambda i, e: (0, 0))],
        out_specs=pl.BlockSpec((tm, D_MODEL), row),
        out_shape=jax.ShapeDtypeStruct((n, D_MODEL), F32),
        scratch_shapes=[pltpu.VMEM((tm, D_MODEL), BF16), pltpu.VMEM((tm, LANES), F32),
                        pltpu.VMEM((tm // sd, sd, sd), BF16), pltpu.SMEM((tm // sd, N_GROUPS + 1), jnp.int32)],
        compiler_params=pltpu.CompilerParams(dimension_semantics=("arbitrary", "arbitrary"),
                                             vmem_limit_bytes=VMEM_LIMIT),
        name="moe",
    )(xn, comb, h, w_gate, w_up, w_down, g_final)


def _block_diag_tiles(w):
    per = 256 // RNN_BLOCK_DIM
    w4 = w.reshape(RNN_BLOCKS // per, per, RNN_BLOCK_DIM, RNN_BLOCK_DIM)
    eye = jnp.eye(per, dtype=w.dtype)
    tiles = jnp.einsum('tpde,pq->tpdqe', w4, eye)
    return tiles.reshape(RNN_BLOCKS // per, 256, 256)


def _layer(l, xp, xs, cache_cmp_kv, cache_sel_kv, cache_win_kv, state_conv, state_h, page_table,
           norm_mix, w_in, cmp_pool_w, conv_w, conv_b, lru_wa, lru_ba, lru_wx, lru_bx, lru_lambda, w_out,
           norm_ffn, w_router_group, b_router_group, w_router_expert, b_router_expert,
           w_exp_gate, w_exp_up, w_exp_down, final_gain):
    bp, sp, _ = xp.shape
    bs, ts, _ = xs.shape
    n_pages = page_table.shape[1]
    past = n_pages * PAGE_SIZE

    wi = w_in[l]
    gt_cols = wi[:, 2560:2584].reshape(D_MODEL, N_KV, GQA * 3)
    gt_cols = jnp.pad(gt_cols, ((0, 0), (0, 0), (0, LANES - GQA * 3))).reshape(D_MODEL, N_KV * LANES)
    w_proj_f32 = jnp.concatenate([wi[:, :2560], wi[:, 2584:], gt_cols], axis=1)
    w_proj = w_proj_f32.astype(BF16)
    g_mix = norm_mix[l].reshape(1, D_MODEL)
    wfull = jnp.tile(jnp.repeat(cmp_pool_w[l], HEAD_DIM, axis=1), (1, N_KV))
    w2 = jnp.broadcast_to(cmp_pool_w[l].T[:, :, None], (2, CMP_BLOCK, HEAD_DIM))
    wa_f32 = _block_diag_tiles(lru_wa[l])
    wx_f32 = _block_diag_tiles(lru_wx[l])
    row = lambda v: v.reshape(1, -1)
    slopes = jnp.exp2(-8.0 * jnp.arange(1, N_HEADS + 1, dtype=F32) / N_HEADS)
    w_o = w_out[l].astype(BF16)
    w_router = jnp.pad(jnp.concatenate([w_router_expert[l], w_router_group[l]], axis=1),
                       ((0, 0), (0, LANES - N_EXPERTS - N_GROUPS)))
    b_router = jnp.pad(jnp.concatenate([b_router_expert[l], b_router_group[l]]),
                       (0, LANES - N_EXPERTS - N_GROUPS)).reshape(1, LANES)
    wg, wu, wd = w_exp_gate[l].astype(BF16), w_exp_up[l].astype(BF16), w_exp_down[l].astype(BF16)
    lru = lambda wa, wx: (conv_w[l], row(conv_b[l]), wa, row(lru_ba[l]), wx, row(lru_bx[l]), row(lru_lambda[l]))

    def tail(x, attn_o, rnn_o, w_out_l, tm_f, tm_m):
        h, xn, comb = _finish(x, attn_o, rnn_o, w_out_l, row(norm_ffn[l]), w_router, b_router, tm_f)
        return _moe(xn, comb, h, wg, wu, wd, final_gain, tm_m)

    np_ = bp * sp
    q, kc, ks, kw, xr, xg, gt, kc4, ks4, kw4 = _project(xp.reshape(np_, D_MODEL), g_mix, w_proj, 256)
    shp = lambda a: a.reshape(bp, sp, a.shape[-1])
    ident = jnp.arange(np_ // 1024, dtype=jnp.int32)
    cmp_p = _compress(ident, kc.reshape(np_ // 1024, 1024, KV_W), wfull, bp, sp // 1024, 1, 1024)
    attn_p = _nsa_prompt(slopes, shp(q), shp(gt), cmp_p, shp(ks), shp(kw))
    rnn_p, h_p = _rglru(shp(xr), shp(xg), jnp.zeros((bp, 8, D_RNN), F32), jnp.zeros((bp, 1, D_RNN), F32),
                       *lru(wa_f32.astype(BF16), wx_f32.astype(BF16)), tc=512)
    y_p = tail(xp.reshape(np_, D_MODEL), attn_p.reshape(np_, D_ATTN), rnn_p.reshape(np_, D_RNN), w_o, 512, 1024)
    kv6 = lambda a, b_, t_: a.reshape(b_, t_, N_KV, 2, HEAD_DIM)
    outs_p = (y_p.reshape(bp, sp, D_MODEL), kv6(kc4, bp, sp), kv6(ks4, bp, sp),
              kv6(kw4, bp, sp)[:, -min(WINDOW, sp):], shp(xr)[:, sp - (CONV_W - 1):], h_p.reshape(bp, D_RNN))

    ns_ = bs * ts
    q, kc, ks, kw, xr, xg, gt = _project_precise(xs.reshape(ns_, D_MODEL), g_mix, w_proj_f32)
    shs = lambda a: a.reshape(bs, ts, a.shape[-1])
    cmp_s = _compress_paged(page_table.reshape(-1), cache_cmp_kv[l].reshape(-1, HEAD_DIM), w2, bs, n_pages, 16)
    idx, o_cw = _nsa_sample_a(past, shs(q), shs(gt), cmp_s, cache_win_kv[l].reshape(-1, HEAD_DIM), shs(kw))
    idx = idx.reshape(bs, N_KV, ts, LANES)[..., :N_GATHER].reshape(-1)
    gate_sel = gt.reshape(ns_, N_KV, LANES)[:, :, :GQA * 3].reshape(ns_, N_HEADS, 3)[:, :, 1:2]
    gate_sel = jnp.broadcast_to(gate_sel, (ns_, N_HEADS, HEAD_DIM))
    slopes8 = jnp.broadcast_to(slopes.reshape(N_HEADS, 1), (N_HEADS, LANES))
    new_blocks = jnp.pad(ks.reshape(bs, ts * KV_ROWS, HEAD_DIM), ((0, 0), (0, BLOCK_ROWS - ts * KV_ROWS), (0, 0)))
    attn_s = _nsa_sample_b(past, idx, page_table.reshape(-1), q.reshape(ns_, N_HEADS, HEAD_DIM), gate_sel, slopes8,
                           o_cw.reshape(ns_, N_HEADS, HEAD_DIM), new_blocks, cache_sel_kv[l].reshape(-1, HEAD_DIM))
    conv_rows = jnp.pad(state_conv[l], ((0, 0), (ts - (CONV_W - 1), 0), (0, 0))).reshape(ns_, D_RNN)
    rnn_s, h_all = _rglru_short(xr, xg, conv_rows, jnp.repeat(state_h[l], ts, axis=0), *lru(wa_f32, wx_f32), t=ts)
    h_s = h_all.reshape(bs, ts, D_RNN)[:, -1]
    y_s = tail(xs.reshape(ns_, D_MODEL), attn_s.reshape(ns_, D_ATTN), rnn_s, w_out[l], ns_, ns_)
    win_s = jnp.concatenate([cache_win_kv[l], kv6(kw, bs, ts)], axis=1)[:, ts:]
    conv_s = jnp.concatenate([state_conv[l], shs(xr)], axis=1)[:, ts:]
    outs_s = (y_s.reshape(bs, ts, D_MODEL), kv6(kc, bs, ts), kv6(ks, bs, ts),
              win_s, conv_s, h_s.reshape(bs, D_RNN))
    return outs_p, outs_s


def kernel(x_prompt, x_sample, cache_cmp_kv, cache_sel_kv, cache_win_kv, state_conv, state_h, page_table, norm_mix, w_in, cmp_pool_w, conv_w, conv_b, lru_wa, lru_ba, lru_wx, lru_bx, lru_lambda, w_out, norm_ffn, w_router_group, b_router_group, w_router_expert, b_router_expert, w_exp_gate, w_exp_up, w_exp_down, norm_final):
    depth = w_in.shape[0]
    assert depth == 1, "the final norm is fused into the single layer's expert kernel"
    p, s = _layer(0, x_prompt, x_sample, cache_cmp_kv, cache_sel_kv, cache_win_kv, state_conv, state_h, page_table,
                  norm_mix, w_in, cmp_pool_w, conv_w, conv_b, lru_wa, lru_ba, lru_wx, lru_bx, lru_lambda, w_out,
                  norm_ffn, w_router_group, b_router_group, w_router_expert, b_router_expert,
                  w_exp_gate, w_exp_up, w_exp_down, norm_final.reshape(1, D_MODEL))
    st = lambda a: a[None]
    return (p[0], s[0], st(p[1]), st(s[1]), st(p[2]), st(s[2]), st(p[3]), st(s[3]),
            st(p[4]), st(s[4]), st(p[5]), st(s[5]))
```

```python
import functools

import jax
import jax.numpy as jnp
from jax import lax
from jax.experimental import pallas as pl
from jax.experimental.pallas import tpu as pltpu

F32 = jnp.float32
BF16 = jnp.bfloat16

D_MODEL = 2048
D_ATTN = 1024
D_RNN = 1024
N_HEADS = 8
HEAD_DIM = 128
N_KV = 2
GQA = 4
KV_W = 512
CMP_BLOCK = 32
SEL_BLOCK = 64
TOP_N = 16
WINDOW = 512
FORCE_SCORE = 1e4
RNN_BLOCKS = 16
RNN_BLOCK_DIM = 64
CONV_W = 4
LRU_C = 8.0
N_GROUPS = 4
EXP_PER_GROUP = 4
N_EXPERTS = 16
D_EXPERT = 512
RMS_EPS = 1e-6
PAGE_SIZE = 128

GROUP_LANE = N_EXPERTS
LANES = 128
NEG = -1e30
VMEM_LIMIT = 56 * 1024 * 1024

_SEG_Q = (0, 1024)
_SEG_KC = (1024, 1536)
_SEG_KS = (1536, 2048)
_SEG_KW = (2048, 2560)
_SEG_XR = (2560, 3584)
_SEG_XG = (3584, 4608)
_SEG_GT = (4608, 4864)
_PROJ_W = 4864


def _dot(a, b):
    return jnp.dot(a, b, preferred_element_type=F32)


def _dot_nt(a, b):
    return lax.dot_general(a, b, (((1,), (1,)), ((), ())), preferred_element_type=F32)


def _rms(x, g):
    return x * lax.rsqrt(jnp.mean(x * x, axis=-1, keepdims=True) + RMS_EPS) * g


def _split3(x):
    h1 = x.astype(BF16)
    r1 = x - h1.astype(F32)
    h2 = r1.astype(BF16)
    h3 = (r1 - h2.astype(F32)).astype(BF16)
    return h1, h2, h3


def _dot_f32(a, b, dot=_dot):
    a1, a2, a3 = _split3(a)
    b1, b2, b3 = _split3(b)
    return (dot(a3, b1) + dot(a2, b2) + dot(a1, b3)) + (dot(a2, b1) + dot(a1, b2)) + dot(a1, b1)


def _dot_x3(a, b, dot=_dot):
    a1, a2, _ = _split3(a)
    b1, b2, _ = _split3(b)
    return (dot(a2, b1) + dot(a1, b2)) + dot(a1, b1)


def _mm(a, w):
    if w.dtype == BF16:
        return _dot(a.astype(BF16), w)
    return _dot_x3(a, w)


def _dot_sel(a, m01):
    a1, a2, a3 = _split3(a)
    return (_dot(a3, m01) + _dot(a2, m01)) + _dot(a1, m01)


def _proj_kernel(x_ref, g_ref, w_ref, q_ref, kc_ref, ks_ref, kw_ref, xr_ref, xg_ref, gt_ref,
                 kc4_ref, ks4_ref, kw4_ref):
    xn = _rms(x_ref[...], g_ref[...]).astype(BF16)
    tm = x_ref.shape[0]
    outs = ((q_ref, _SEG_Q, None), (kc_ref, _SEG_KC, kc4_ref), (ks_ref, _SEG_KS, ks4_ref), (kw_ref, _SEG_KW, kw4_ref),
            (xr_ref, _SEG_XR, None), (xg_ref, _SEG_XG, None), (gt_ref, _SEG_GT, None))
    for ref, (lo, hi), cache_ref in outs:
        z = _dot(xn, w_ref[:, lo:hi])
        ref[...] = z
        if cache_ref is not None:
            for gc in range(KV_ROWS):
                cache_ref[pl.ds(gc, tm, stride=KV_ROWS), :] = z[:, gc * HEAD_DIM:(gc + 1) * HEAD_DIM]


def _project(x, g, w, tm):
    n = x.shape[0]
    widths = [hi - lo for lo, hi in (_SEG_Q, _SEG_KC, _SEG_KS, _SEG_KW, _SEG_XR, _SEG_XG, _SEG_GT)]
    return pl.pallas_call(
        _proj_kernel,
        grid=(n // tm,),
        in_specs=[pl.BlockSpec((tm, D_MODEL), lambda i: (i, 0)),
                  pl.BlockSpec((1, D_MODEL), lambda i: (0, 0)),
                  pl.BlockSpec((D_MODEL, _PROJ_W), lambda i: (0, 0), pipeline_mode=pl.Buffered(1))],
        out_specs=[pl.BlockSpec((tm, w_), lambda i: (i, 0)) for w_ in widths]
        + [pl.BlockSpec((tm * KV_ROWS, HEAD_DIM), lambda i: (i, 0))] * 3,
        out_shape=[jax.ShapeDtypeStruct((n, w_), F32) for w_ in widths]
        + [jax.ShapeDtypeStruct((n * KV_ROWS, HEAD_DIM), F32)] * 3,
        compiler_params=pltpu.CompilerParams(dimension_semantics=("arbitrary",), vmem_limit_bytes=VMEM_LIMIT),
        name="proj",
    )(x, g, w)


def _proj_precise_kernel(x_ref, g_ref, w_ref, z_ref, xn_scr):
    @pl.when(pl.program_id(0) == 0)
    def _():
        xn_scr[...] = _rms(x_ref[...], g_ref[...])

    z_ref[...] = _dot_x3(xn_scr[...], w_ref[...])


def _project_precise(x, g, w_f32):
    n = x.shape[0]
    tn = 2 * LANES
    z = pl.pallas_call(
        _proj_precise_kernel,
        grid=(_PROJ_W // tn,),
        in_specs=[pl.BlockSpec((n, D_MODEL), lambda j: (0, 0)),
                  pl.BlockSpec((1, D_MODEL), lambda j: (0, 0)),
                  pl.BlockSpec((D_MODEL, tn), lambda j: (0, j))],
        out_specs=pl.BlockSpec((n, tn), lambda j: (0, j)),
        out_shape=jax.ShapeDtypeStruct((n, _PROJ_W), F32),
        scratch_shapes=[pltpu.VMEM((n, D_MODEL), F32)],
        compiler_params=pltpu.CompilerParams(dimension_semantics=("arbitrary",), vmem_limit_bytes=VMEM_LIMIT),
        name="proj_precise",
    )(x, g, w_f32)
    return [z[:, lo:hi] for lo, hi in (_SEG_Q, _SEG_KC, _SEG_KS, _SEG_KW, _SEG_XR, _SEG_XG, _SEG_GT)]


def _compress_kernel(n_pages, pt_ref, *refs):
    del pt_ref
    pages, w_ref, o_ref = refs[:n_pages], refs[n_pages], refs[n_pages + 1]
    if n_pages == 1:
        x = pages[0][0]
    else:
        x = jnp.concatenate([p[0] for p in pages], axis=0)
    rows = x.shape[0]
    xb = x.reshape(rows // CMP_BLOCK, CMP_BLOCK, KV_W) * w_ref[...][None]
    o_ref[0] = jnp.sum(xb, axis=1)


def _compress(table, src, wfull, n_batch, steps, n_pages, page_rows):
    out_rows = n_pages * page_rows // CMP_BLOCK
    per_b = steps * n_pages

    def page_spec(k):
        return pl.BlockSpec((1, page_rows, KV_W), lambda b, j, pt: (pt[b * per_b + j * n_pages + k], 0, 0))

    return pl.pallas_call(
        functools.partial(_compress_kernel, n_pages),
        grid_spec=pltpu.PrefetchScalarGridSpec(
            num_scalar_prefetch=1,
            grid=(n_batch, steps),
            in_specs=[page_spec(k) for k in range(n_pages)] + [pl.BlockSpec((CMP_BLOCK, KV_W), lambda b, j, pt: (0, 0))],
            out_specs=pl.BlockSpec((1, out_rows, KV_W), lambda b, j, pt: (b, j, 0)),
        ),
        out_shape=jax.ShapeDtypeStruct((n_batch, steps * out_rows, KV_W), F32),
        compiler_params=pltpu.CompilerParams(dimension_semantics=("arbitrary", "arbitrary"), vmem_limit_bytes=VMEM_LIMIT),
        name="compress",
    )(table, *([src] * n_pages), wfull)


def _softplus(x):
    return jnp.maximum(x, 0.0) + jnp.log1p(jnp.exp(-jnp.abs(x)))


def _rglru_kernel(tc, xr_ref, xg_ref, cs_ref, h0_ref, cw_ref, cb_ref, wa_ref, ba_ref, wx_ref, bx_ref, lam_ref,
                  o_ref, hl_ref, tail_scr, h_scr):
    @pl.when(pl.program_id(2) == 0)
    def _():
        tail_scr[...] = cs_ref[0]
        h_scr[...] = h0_ref[0]

    x = xr_ref[0]
    xp = jnp.concatenate([tail_scr[...], x], axis=0)
    w = cw_ref[...]
    xc = cb_ref[...] + pltpu.roll(xp, 3, axis=0)[8:] * w[0:1]
    xc = xc + pltpu.roll(xp, 2, axis=0)[8:] * w[1:2]
    xc = xc + pltpu.roll(xp, 1, axis=0)[8:] * w[2:3]
    xc = xc + x * w[3:4]
    tail_scr[...] = xp[tc:]

    r = jax.nn.sigmoid(_mm(xc, wa_ref[0]) + ba_ref[...])
    gi = jax.nn.sigmoid(_mm(xc, wx_ref[0]) + bx_ref[...])
    log_a = -LRU_C * r * _softplus(-lam_ref[...])
    a = jnp.exp(log_a)
    u = jnp.sqrt(-jnp.tanh(log_a) * (a * a + 1.0)) * (gi * xc)

    row = lax.broadcasted_iota(jnp.int32, a.shape, 0)
    s = 1
    while s < tc:
        keep = row >= s
        a_sh = jnp.where(keep, pltpu.roll(a, s, axis=0), 1.0)
        u_sh = jnp.where(keep, pltpu.roll(u, s, axis=0), 0.0)
        u = a * u_sh + u
        a = a * a_sh
        s *= 2
    h = a * h_scr[...] + u
    h_last = h[tc - 1:tc]
    h_scr[...] = h_last
    hl_ref[0] = h_last
    o_ref[0] = h * jax.nn.gelu(xg_ref[0])


def _rglru(xr, xg, conv_state8, h0, conv_w, conv_b, wa_t, ba, wx_t, bx, lam, tc):
    b, t, _ = xr.shape
    ct = 256
    n_ct = D_RNN // ct
    seq = lambda bi, c, j: (bi, j, c)
    per_b = lambda bi, c, j: (bi, 0, c)
    par = lambda bi, c, j: (0, c)
    return pl.pallas_call(
        functools.partial(_rglru_kernel, tc),
        grid=(b, n_ct, t // tc),
        in_specs=[pl.BlockSpec((1, tc, ct), seq), pl.BlockSpec((1, tc, ct), seq),
                  pl.BlockSpec((1, 8, ct), per_b), pl.BlockSpec((1, 1, ct), per_b),
                  pl.BlockSpec((CONV_W, ct), par), pl.BlockSpec((1, ct), par),
                  pl.BlockSpec((1, ct, ct), lambda bi, c, j: (c, 0, 0)), pl.BlockSpec((1, ct), par),
                  pl.BlockSpec((1, ct, ct), lambda bi, c, j: (c, 0, 0)), pl.BlockSpec((1, ct), par),
                  pl.BlockSpec((1, ct), par)],
        out_specs=[pl.BlockSpec((1, tc, ct), seq), pl.BlockSpec((1, 1, ct), per_b)],
        out_shape=[jax.ShapeDtypeStruct((b, t, D_RNN), F32), jax.ShapeDtypeStruct((b, 1, D_RNN), F32)],
        scratch_shapes=[pltpu.VMEM((8, ct), F32), pltpu.VMEM((1, ct), F32)],
        compiler_params=pltpu.CompilerParams(dimension_semantics=("arbitrary", "arbitrary", "arbitrary"),
                                             vmem_limit_bytes=VMEM_LIMIT),
        name="rglru",
    )(xr, xg, conv_state8, h0, conv_w, conv_b, wa_t, ba, wx_t, bx, lam)


def _rglru_short_kernel(t, xr_ref, xg_ref, cs_ref, h0_ref, cw_ref, cb_ref, wa_ref, ba_ref, wx_ref, bx_ref, lam_ref,
                        o_ref, h_ref):
    x = xr_ref[...]
    st = cs_ref[...]
    n = x.shape[0]
    step = lax.broadcasted_iota(jnp.int32, x.shape, 0) % t
    w = cw_ref[...]
    xc = cb_ref[...] + x * w[CONV_W - 1:CONV_W]
    for sh in range(1, CONV_W):
        prev = jnp.where(step >= sh, pltpu.roll(x, sh, axis=0), pltpu.roll(st, n - (t - sh), axis=0))
        xc = xc + prev * w[CONV_W - 1 - sh:CONV_W - sh]
    r = jax.nn.sigmoid(_mm(xc, wa_ref[0]) + ba_ref[...])
    gi = jax.nn.sigmoid(_mm(xc, wx_ref[0]) + bx_ref[...])
    log_a = -LRU_C * r * _softplus(-lam_ref[...])
    a = jnp.exp(log_a)
    u = jnp.sqrt(-jnp.tanh(log_a) * (a * a + 1.0)) * (gi * xc)
    s = 1
    while s < t:
        keep = step >= s
        a_sh = jnp.where(keep, pltpu.roll(a, s, axis=0), 1.0)
        u_sh = jnp.where(keep, pltpu.roll(u, s, axis=0), 0.0)
        u = a * u_sh + u
        a = a * a_sh
        s *= 2
    h = a * h0_ref[...] + u
    h_ref[...] = h
    o_ref[...] = h * jax.nn.gelu(xg_ref[...])


def _rglru_short(xr, xg, conv_rows, h0_rows, conv_w, conv_b, wa_t, ba, wx_t, bx, lam, t):
    n = xr.shape[0]
    ct = 256
    blk = pl.BlockSpec((n, ct), lambda c: (0, c))
    par = lambda c: (0, c)
    tile = pl.BlockSpec((1, ct, ct), lambda c: (c, 0, 0))
    return pl.pallas_call(
        functools.partial(_rglru_short_kernel, t),
        grid=(D_RNN // ct,),
        in_specs=[blk, blk, blk, blk, pl.BlockSpec((CONV_W, ct), par), pl.BlockSpec((1, ct), par),
                  tile, pl.BlockSpec((1, ct), par), tile, pl.BlockSpec((1, ct), par), pl.BlockSpec((1, ct), par)],
        out_specs=[blk, blk],
        out_shape=[jax.ShapeDtypeStruct((n, D_RNN), F32), jax.ShapeDtypeStruct((n, D_RNN), F32)],
        compiler_params=pltpu.CompilerParams(dimension_semantics=("arbitrary",), vmem_limit_bytes=VMEM_LIMIT),
        name="rglru_short",
    )(xr, xg, conv_rows, h0_rows, conv_w, conv_b, wa_t, ba, wx_t, bx, lam)


TQ = 128
W4 = GQA * TQ
WIN_KEYS = WINDOW + TQ
BIG_KEYS = 8 * LANES


def _nsa_prompt_kernel(slopes_ref, q_ref, gt_ref, kc_ref, vc_ref, ks_ref, vs_ref, kw_ref, vw_ref, o_ref,
                       q_scr, kcb, vct, ksb, vst, kwb, vwt, bias0, caus, wlow,
                       impt_scr, selb_scr, m_scr, l_scr, acc_scr, out_scr):
    g = pl.program_id(1)
    i = pl.program_id(2)
    s_len = ks_ref.shape[1]
    n_cmp = kc_ref.shape[1]
    n_sel = n_cmp // 2
    lane = lax.broadcasted_iota(jnp.int32, (1, W4), 1)
    tl_row = (lane % TQ).astype(F32)
    slope_row = jnp.full((1, W4), slopes_ref[g * GQA + GQA - 1], F32)
    for r in reversed(range(GQA - 1)):
        slope_row = jnp.where(lane < (r + 1) * TQ, slopes_ref[g * GQA + r], slope_row)

    @pl.when(i == 0)
    def _prepare():
        kcb[...] = kc_ref[0].astype(BF16)
        vct[...] = vc_ref[0].T.astype(BF16)

        def cast(j, carry):
            off = pl.multiple_of(j * LANES, LANES)
            ksb[pl.ds(off, LANES), :] = ks_ref[0, pl.ds(off, LANES), :].astype(BF16)
            kwb[pl.ds(off, LANES), :] = kw_ref[0, pl.ds(off, LANES), :].astype(BF16)
            vst[:, pl.ds(off, LANES)] = vs_ref[0, pl.ds(off, LANES), :].T.astype(BF16)
            vwt[:, pl.ds(off, LANES)] = vw_ref[0, pl.ds(off, LANES), :].T.astype(BF16)
            return carry

        lax.fori_loop(0, s_len // LANES, cast, 0)
        rel = tl_row - lax.broadcasted_iota(jnp.int32, (BIG_KEYS, W4), 0).astype(F32)
        bias0[...] = slope_row * rel
        caus[...] = jnp.where(rel[0:LANES] >= 0, 0.0, NEG)
        wlow[...] = jnp.where(rel[0:LANES] <= 0, 0.0, NEG)

    start_f = (i * TQ).astype(F32)
    scale = HEAD_DIM ** -0.5
    for r in range(GQA):
        q_scr[r * TQ:(r + 1) * TQ, :] = (q_ref[0, :, r * HEAD_DIM:(r + 1) * HEAD_DIM] * scale).astype(BF16)
    gate_t = jax.nn.sigmoid(gt_ref[0]).T

    def gate_row(branch):
        return jnp.concatenate([gate_t[3 * r + branch:3 * r + branch + 1, :] for r in range(GQA)], axis=1)

    t_row = start_f + tl_row
    c_end = ((lax.broadcasted_iota(jnp.int32, (n_cmp, W4), 0) + 1) * CMP_BLOCK - 1).astype(F32)
    dist_c = t_row - c_end
    ok_c = dist_c >= 0
    x = jnp.where(ok_c, _dot_nt(kcb[...], q_scr[...]) - slope_row * dist_c, NEG)
    e = jnp.where(ok_c, jnp.exp(x - jnp.max(x, axis=0, keepdims=True)), 0.0)
    p = e * (1.0 / jnp.maximum(jnp.sum(e, axis=0, keepdims=True), 1e-30))
    out_scr[...] = gate_row(0) * _dot(vct[...], p.astype(BF16))

    p_heads = p[:, 0:TQ]
    for r in range(1, GQA):
        p_heads = p_heads + p[:, r * TQ:(r + 1) * TQ]
    impt_scr[...] = p_heads
    imp = impt_scr[pl.ds(0, n_sel, stride=2), :] + impt_scr[pl.ds(1, n_sel, stride=2), :]
    blk = lax.broadcasted_iota(jnp.int32, (n_sel, TQ), 0)
    cur = (i * TQ + lax.broadcasted_iota(jnp.int32, (n_sel, TQ), 1)) // SEL_BLOCK
    forced = (blk == 0) | (blk == cur) | (blk == cur - 1)
    imp = jnp.where(forced, FORCE_SCORE, imp)
    imp = jnp.where(blk > cur, NEG, imp)
    rank = jnp.zeros((n_sel, TQ), jnp.int32)
    for j in range(n_sel):
        row = imp[j:j + 1, :]
        tie = jnp.where(blk > j, jnp.where(row == imp, 1, 0), 0)
        rank = rank + jnp.where(row > imp, 1, tie)
    selb = jnp.where(rank < TOP_N, 0.0, NEG)
    for j in range(n_sel):
        selb_scr[j] = selb[j:j + 1, :]

    def scores(k_b, c, nk):
        off = pl.multiple_of(c * LANES, LANES)
        return _dot_nt(k_b[pl.ds(off, nk), :], q_scr[...]) - bias0[0:nk, :]

    def block_mask(c, nk):
        rows = [jnp.broadcast_to(selb_scr[2 * c + k], (SEL_BLOCK, TQ)) for k in range(nk // SEL_BLOCK)]
        return jnp.concatenate([jnp.concatenate(rows, axis=0)] * GQA, axis=1)

    def update(v_t, c, nk, x):
        off = pl.multiple_of(c * LANES, LANES)
        r_c = slope_row * ((c * LANES).astype(F32) - start_f)
        m_old = m_scr[...]
        m_new = jnp.maximum(m_old, jnp.max(x, axis=0, keepdims=True) + r_c)
        pr = jnp.exp(x + (r_c - m_new))
        alpha = jnp.exp(m_old - m_new)
        l_scr[...] = alpha * l_scr[...] + jnp.sum(pr, axis=0, keepdims=True)
        acc_scr[...] = alpha * acc_scr[...] + _dot(v_t[:, pl.ds(off, nk)], pr.astype(BF16))
        m_scr[...] = m_new

    def reset():
        m_scr[...] = jnp.full((1, W4), NEG, F32)
        l_scr[...] = jnp.zeros((1, W4), F32)
        acc_scr[...] = jnp.zeros((HEAD_DIM, W4), F32)

    def result():
        return acc_scr[...] * (1.0 / jnp.maximum(l_scr[...], 1e-30))

    reset()
    per_big = BIG_KEYS // LANES
    n_big = i // per_big

    def sel_update(c, nk):
        update(vst, c, nk, scores(ksb, c, nk) + block_mask(c, nk))

    def big(cb, carry):
        sel_update(per_big * cb, BIG_KEYS)
        return carry

    lax.fori_loop(0, n_big, big, 0)
    done = per_big * n_big
    for part in (4, 2, 1):
        take = (i - done) // part % 2 == 1
        start = done + (i - done) // (2 * part) * (2 * part)

        @pl.when(take)
        def _():
            sel_update(start, part * LANES)

    update(vst, i, LANES, scores(ksb, i, LANES) + block_mask(i, LANES) + caus[...])
    out_scr[...] += gate_row(1) * result()

    n_back = WINDOW // LANES

    @pl.when(i >= n_back)
    def _():
        c = i - n_back
        x = scores(kwb, c, WIN_KEYS)
        x = jnp.concatenate([x[0:LANES] + wlow[...], x[LANES:WINDOW], x[WINDOW:WIN_KEYS] + caus[...]], axis=0)
        pr = jnp.exp(x - jnp.max(x, axis=0, keepdims=True))
        o_w = _dot(vwt[:, pl.ds(pl.multiple_of(c * LANES, LANES), WIN_KEYS)], pr.astype(BF16))
        out_scr[...] += gate_row(2) * (o_w * (1.0 / jnp.maximum(jnp.sum(pr, axis=0, keepdims=True), 1e-30)))

    @pl.when(i < n_back)
    def _():
        reset()

        def body(c, carry):
            update(vwt, c, LANES, scores(kwb, c, LANES))
            return carry

        lax.fori_loop(0, i, body, 0)
        update(vwt, i, LANES, scores(kwb, i, LANES) + caus[...])
        out_scr[...] += gate_row(2) * result()

    o_t = out_scr[...]
    for r in range(GQA):
        o_ref[0, :, r * HEAD_DIM:(r + 1) * HEAD_DIM] = o_t[:, r * TQ:(r + 1) * TQ].T


def _nsa_prompt(slopes, q, gt, cmp_kv, ks, kw):
    b, s, _ = q.shape
    n_cmp = cmp_kv.shape[1]
    k_of = lambda bi, g, i: (bi, 0, 2 * g)
    v_of = lambda bi, g, i: (bi, 0, 2 * g + 1)
    tile = pltpu.VMEM((LANES, W4), F32)
    return pl.pallas_call(
        _nsa_prompt_kernel,
        grid=(b, N_KV, s // TQ),
        in_specs=[pl.BlockSpec(memory_space=pltpu.SMEM),
                  pl.BlockSpec((1, TQ, GQA * HEAD_DIM), lambda bi, g, i: (bi, i, g)),
                  pl.BlockSpec((1, TQ, LANES), lambda bi, g, i: (bi, i, g)),
                  pl.BlockSpec((1, n_cmp, HEAD_DIM), k_of), pl.BlockSpec((1, n_cmp, HEAD_DIM), v_of),
                  pl.BlockSpec((1, s, HEAD_DIM), k_of), pl.BlockSpec((1, s, HEAD_DIM), v_of),
                  pl.BlockSpec((1, s, HEAD_DIM), k_of), pl.BlockSpec((1, s, HEAD_DIM), v_of)],
        out_specs=pl.BlockSpec((1, TQ, GQA * HEAD_DIM), lambda bi, g, i: (bi, i, g)),
        out_shape=jax.ShapeDtypeStruct((b, s, D_ATTN), F32),
        scratch_shapes=[pltpu.VMEM((W4, HEAD_DIM), BF16),
                        pltpu.VMEM((n_cmp, HEAD_DIM), BF16), pltpu.VMEM((HEAD_DIM, n_cmp), BF16),
                        pltpu.VMEM((s, HEAD_DIM), BF16), pltpu.VMEM((HEAD_DIM, s), BF16),
                        pltpu.VMEM((s, HEAD_DIM), BF16), pltpu.VMEM((HEAD_DIM, s), BF16),
                        pltpu.VMEM((BIG_KEYS, W4), F32), tile, tile,
                        pltpu.VMEM((n_cmp, TQ), F32),
                        pltpu.VMEM((n_cmp // 2, 1, TQ), F32),
                        pltpu.VMEM((1, W4), F32), pltpu.VMEM((1, W4), F32),
                        pltpu.VMEM((HEAD_DIM, W4), F32), pltpu.VMEM((HEAD_DIM, W4), F32)],
        compiler_params=pltpu.CompilerParams(dimension_semantics=("arbitrary", "arbitrary", "arbitrary"),
                                             vmem_limit_bytes=VMEM_LIMIT),
        name="nsa_prompt",
    )(slopes, q, gt, cmp_kv, cmp_kv, ks, ks, kw, kw)


KV_ROWS = 2 * N_KV


def _compress_paged_kernel(n_pages, pt_ref, *refs):
    del pt_ref
    pages, w_ref, o_ref = refs[:n_pages], refs[n_pages], refs[n_pages + 1]
    for gc in range(KV_ROWS):
        x = jnp.concatenate([p[pl.ds(gc, PAGE_SIZE, stride=KV_ROWS), :] for p in pages], axis=0)
        xb = x.reshape(n_pages * PAGE_SIZE // CMP_BLOCK, CMP_BLOCK, HEAD_DIM) * w_ref[gc % 2][None]
        o_ref[0, gc] = jnp.sum(xb, axis=1)


def _compress_paged(table, pool2d, w2, n_batch, pages_per_b, n_pages):
    steps = pages_per_b // n_pages
    out_rows = n_pages * PAGE_SIZE // CMP_BLOCK
    rows = PAGE_SIZE * KV_ROWS

    def page_spec(k):
        return pl.BlockSpec((rows, HEAD_DIM), lambda b, j, pt: (pt[b * pages_per_b + j * n_pages + k], 0))

    return pl.pallas_call(
        functools.partial(_compress_paged_kernel, n_pages),
        grid_spec=pltpu.PrefetchScalarGridSpec(
            num_scalar_prefetch=1,
            grid=(n_batch, steps),
            in_specs=[page_spec(k) for k in range(n_pages)]
            + [pl.BlockSpec((2, CMP_BLOCK, HEAD_DIM), lambda b, j, pt: (0, 0, 0))],
            out_specs=pl.BlockSpec((1, KV_ROWS, out_rows, HEAD_DIM), lambda b, j, pt: (b, 0, j, 0)),
        ),
        out_shape=jax.ShapeDtypeStruct((n_batch, KV_ROWS, steps * out_rows, HEAD_DIM), F32),
        compiler_params=pltpu.CompilerParams(dimension_semantics=("arbitrary", "arbitrary"), vmem_limit_bytes=VMEM_LIMIT),
        name="compress_paged",
    )(table, *([pool2d] * n_pages), w2)


def _nsa_sample_a_kernel(past, q_ref, gt_ref, cmp_ref, cw_ref, kwn_ref, idx_ref, o_ref):
    t_new = q_ref.shape[1]
    n_cmp = cmp_ref.shape[2]
    n_past_sel = n_cmp // 2
    n_win = cw_ref.shape[0] // KV_ROWS
    rows = GQA * t_new
    gates = jax.nn.sigmoid(gt_ref[0])
    t_row = past + lax.broadcasted_iota(jnp.int32, (rows, 1), 0) % t_new
    scale = HEAD_DIM ** -0.5
    pair = jnp.where(lax.broadcasted_iota(jnp.int32, (n_cmp, n_past_sel), 0) // 2
                     == lax.broadcasted_iota(jnp.int32, (n_cmp, n_past_sel), 1), 1.0, 0.0).astype(BF16)
    imps = []
    for g in range(N_KV):
        qg = jnp.concatenate([q_ref[0, :, (g * GQA + r) * HEAD_DIM:(g * GQA + r + 1) * HEAD_DIM]
                              for r in range(GQA)], axis=0)
        qg = qg * scale
        slope = jnp.concatenate([jnp.full((t_new, 1), 2.0 ** -(g * GQA + r + 1), F32) for r in range(GQA)], axis=0)
        kcol = g * 2 * HEAD_DIM
        kc = cmp_ref[0, 2 * g]
        vc = cmp_ref[0, 2 * g + 1]
        c_end = (lax.broadcasted_iota(jnp.int32, (1, n_cmp), 1) + 1) * CMP_BLOCK - 1
        dist_c = t_row - c_end
        mask_c = dist_c >= 0
        s = _dot_x3(qg, kc, _dot_nt) - slope * dist_c.astype(F32)
        s = jnp.where(mask_c, s, NEG)
        m = jnp.max(s, axis=-1, keepdims=True)
        p = jnp.where(mask_c, jnp.exp(s - m), 0.0)
        p = p / jnp.maximum(jnp.sum(p, axis=-1, keepdims=True), 1e-30)
        o_c = _dot_x3(p, vc)
        p_heads = p[0:t_new]
        for r in range(1, GQA):
            p_heads = p_heads + p[r * t_new:(r + 1) * t_new]
        imps.append(_dot_sel(p_heads, pair))
        n_pad = LANES - t_new
        kw = jnp.concatenate([cw_ref[pl.ds(2 * g, n_win, stride=KV_ROWS), :], kwn_ref[0, :, kcol:kcol + HEAD_DIM],
                              jnp.zeros((n_pad, HEAD_DIM), F32)], axis=0)
        vw = jnp.concatenate([cw_ref[pl.ds(2 * g + 1, n_win, stride=KV_ROWS), :],
                              kwn_ref[0, :, kcol + HEAD_DIM:kcol + 2 * HEAD_DIM],
                              jnp.zeros((n_pad, HEAD_DIM), F32)], axis=0)
        win_pos = past - n_win + lax.broadcasted_iota(jnp.int32, (1, n_win + LANES), 1)
        dist_w = t_row - win_pos
        mask_w = (dist_w >= 0) & (dist_w <= WINDOW)
        s = _dot_x3(qg, kw, _dot_nt) - slope * dist_w.astype(F32)
        s = jnp.where(mask_w, s, NEG)
        m = jnp.max(s, axis=-1, keepdims=True)
        p = jnp.where(mask_w, jnp.exp(s - m), 0.0)
        p = p / jnp.maximum(jnp.sum(p, axis=-1, keepdims=True), 1e-30)
        o_w = _dot_x3(p, vw)
        for r in range(GQA):
            h = g * GQA + r
            gl = g * LANES + 3 * r
            o_ref[0, :, h * HEAD_DIM:(h + 1) * HEAD_DIM] = (
                gates[:, gl:gl + 1] * o_c[r * t_new:(r + 1) * t_new]
                + gates[:, gl + 2:gl + 3] * o_w[r * t_new:(r + 1) * t_new])

    imp = jnp.concatenate(imps, axis=0)
    n_rows = N_KV * t_new
    lane = lax.broadcasted_iota(jnp.int32, (n_rows, n_past_sel), 1)
    lane_f = lane.astype(F32)
    cur = (past + lax.broadcasted_iota(jnp.int32, (n_rows, n_past_sel), 0) % t_new) // SEL_BLOCK
    forced = (lane == 0) | (lane == cur) | (lane == cur - 1)
    imp = jnp.where(forced, FORCE_SCORE, imp)
    imp = jnp.where(lane > cur, NEG, imp)
    out_lane = lax.broadcasted_iota(jnp.int32, (n_rows, LANES), 1)
    idx = jnp.zeros((n_rows, LANES), F32)
    for k in range(TOP_N - 1):
        m = jnp.max(imp, axis=-1, keepdims=True)
        j = jnp.min(jnp.where(imp == m, lane_f, float(n_past_sel)), axis=-1, keepdims=True)
        idx = jnp.where(out_lane == k, j, idx)
        imp = jnp.where(lane_f == j, -3e38, imp)
    idx_ref[0] = idx.astype(jnp.int32)


def _nsa_sample_a(past, q, gt, cmp_kv, cache_w2d, kw_new):
    b, t_new, _ = q.shape
    win_rows = cache_w2d.shape[0] // b
    per_b3 = lambda bi: (bi, 0, 0)
    return pl.pallas_call(
        functools.partial(_nsa_sample_a_kernel, past),
        grid=(b,),
        in_specs=[pl.BlockSpec((1, t_new, D_ATTN), per_b3), pl.BlockSpec((1, t_new, 2 * LANES), per_b3),
                  pl.BlockSpec((1,) + cmp_kv.shape[1:], lambda bi: (bi, 0, 0, 0)),
                  pl.BlockSpec((win_rows, HEAD_DIM), lambda bi: (bi, 0)),
                  pl.BlockSpec((1, t_new, KV_W), per_b3)],
        out_specs=[pl.BlockSpec((1, N_KV * t_new, LANES), per_b3), pl.BlockSpec((1, t_new, D_ATTN), per_b3)],
        out_shape=[jax.ShapeDtypeStruct((b, N_KV * t_new, LANES), jnp.int32),
                   jax.ShapeDtypeStruct((b, t_new, D_ATTN), F32)],
        compiler_params=pltpu.CompilerParams(dimension_semantics=("arbitrary",), vmem_limit_bytes=VMEM_LIMIT),
        name="nsa_sample_a",
    )(q, gt, cmp_kv, cache_w2d, kw_new)


N_GATHER = TOP_N - 1
BLOCK_ROWS = SEL_BLOCK * KV_ROWS
GATHER_KEYS = TOP_N * SEL_BLOCK


def _nsa_sample_b_kernel(past, t_new, idx_ref, pt_ref, q_ref, gs_ref, slope_ref, ocw_ref, new_ref, pool_ref,
                         o_ref, kv_buf, sems):
    n_steps = pl.num_programs(0) * t_new
    step = pl.program_id(0) * t_new + pl.program_id(1)
    slot = step % 2
    pages_per_b = past // PAGE_SIZE
    sel_per_page = PAGE_SIZE // SEL_BLOCK

    def block_index(st, g, k):
        return idx_ref[((st // t_new * N_KV + g) * t_new + st % t_new) * N_GATHER + k]

    def copies(st, sl):
        out = []
        for g in range(N_KV):
            for k in range(N_GATHER):
                lp = block_index(st, g, k)
                phys = pt_ref[st // t_new * pages_per_b + lp // sel_per_page] * sel_per_page + lp % sel_per_page
                out.append(pltpu.make_async_copy(
                    pool_ref.at[pl.ds(pl.multiple_of(phys * BLOCK_ROWS, BLOCK_ROWS), BLOCK_ROWS), :],
                    kv_buf.at[sl, g, pl.ds(k * BLOCK_ROWS, BLOCK_ROWS), :], sems.at[sl, g, k]))
            out.append(pltpu.make_async_copy(
                new_ref.at[st // t_new], kv_buf.at[sl, g, pl.ds(N_GATHER * BLOCK_ROWS, BLOCK_ROWS), :],
                sems.at[sl, g, N_GATHER]))
        return out

    @pl.when(step == 0)
    def _():
        for cp in copies(step, slot):
            cp.start()

    @pl.when(step + 1 < n_steps)
    def _():
        for cp in copies(step + 1, 1 - slot):
            cp.start()

    t_pos = past + pl.program_id(1)
    q8 = q_ref[0] * HEAD_DIM ** -0.5
    slope = slope_ref[:, 0:1]
    lane = lax.broadcasted_iota(jnp.int32, (1, GATHER_KEYS), 1)
    for cp in copies(step, slot):
        cp.wait()
    outs = []
    for g in range(N_KV):
        pos = past + (lane - N_GATHER * SEL_BLOCK)
        for k in range(N_GATHER):
            pos = jnp.where(lane // SEL_BLOCK == k, block_index(step, g, k) * SEL_BLOCK + lane % SEL_BLOCK, pos)
        kk = kv_buf[slot, g, pl.ds(2 * g, GATHER_KEYS, stride=KV_ROWS), :]
        vv = kv_buf[slot, g, pl.ds(2 * g + 1, GATHER_KEYS, stride=KV_ROWS), :]
        dist = t_pos - pos
        mask = dist >= 0
        s = _dot_x3(q8, kk, _dot_nt) - slope * dist.astype(F32)
        s = jnp.where(mask, s, NEG)
        m = jnp.max(s, axis=-1, keepdims=True)
        p = jnp.where(mask, jnp.exp(s - m), 0.0)
        p = p / jnp.maximum(jnp.sum(p, axis=-1, keepdims=True), 1e-30)
        outs.append(_dot_x3(p, vv))
    head = lax.broadcasted_iota(jnp.int32, (N_HEADS, HEAD_DIM), 0)
    o_s = jnp.where(head < GQA, outs[0], outs[1])
    o_ref[0] = ocw_ref[0] + jax.nn.sigmoid(gs_ref[0]) * o_s


def _nsa_sample_b(past, idx, page_table, q, gate_sel, slopes8, o_cw, new_blocks, pool2d):
    n_tok = q.shape[0]
    b = new_blocks.shape[0]
    t_new = n_tok // b
    tok = lambda bi, ti, *_: (bi * t_new + ti, 0, 0)
    return pl.pallas_call(
        functools.partial(_nsa_sample_b_kernel, past, t_new),
        grid_spec=pltpu.PrefetchScalarGridSpec(
            num_scalar_prefetch=2,
            grid=(b, t_new),
            in_specs=[pl.BlockSpec((1, N_HEADS, HEAD_DIM), tok), pl.BlockSpec((1, N_HEADS, HEAD_DIM), tok),
                      pl.BlockSpec((N_HEADS, LANES), lambda bi, ti, *_: (0, 0)),
                      pl.BlockSpec((1, N_HEADS, HEAD_DIM), tok),
                      pl.BlockSpec(memory_space=pl.ANY), pl.BlockSpec(memory_space=pl.ANY)],
            out_specs=pl.BlockSpec((1, N_HEADS, HEAD_DIM), tok),
            scratch_shapes=[pltpu.VMEM((2, N_KV, TOP_N * BLOCK_ROWS, HEAD_DIM), F32),
                            pltpu.SemaphoreType.DMA((2, N_KV, TOP_N))],
        ),
        out_shape=jax.ShapeDtypeStruct((n_tok, N_HEADS, HEAD_DIM), F32),
        compiler_params=pltpu.CompilerParams(dimension_semantics=("arbitrary", "arbitrary"),
                                             vmem_limit_bytes=VMEM_LIMIT),
        name="nsa_sample_b",
    )(idx, page_table, q, gate_sel, slopes8, o_cw, new_blocks, pool2d)


def _finish_kernel(x_ref, a_ref, r_ref, wo_ref, gn_ref, wr_ref, br_ref, h_ref, xn_ref, comb_ref):
    h = x_ref[...] + _mm(a_ref[...], wo_ref[0:D_ATTN, :]) + _mm(r_ref[...], wo_ref[D_ATTN:D_MODEL, :])
    h_ref[...] = h
    xn = _rms(h, gn_ref[...])
    xn_ref[...] = xn.astype(BF16)
    logit = _dot_f32(xn, wr_ref[...]) + br_ref[...]
    lane = lax.broadcasted_iota(jnp.int32, logit.shape, 1)
    lane_f = lane.astype(F32)
    is_g = (lane >= N_EXPERTS) & (lane < N_EXPERTS + N_GROUPS)
    gl = jnp.where(is_g, logit, NEG)
    g_max = jnp.max(gl, axis=-1, keepdims=True)
    g_star = jnp.min(jnp.where(gl == g_max, lane_f, 1e9), axis=-1, keepdims=True) - N_EXPERTS
    g_prob = 1.0 / jnp.sum(jnp.where(is_g, jnp.exp(gl - g_max), 0.0), axis=-1, keepdims=True)
    in_grp = (lane < N_EXPERTS) & ((lane // EXP_PER_GROUP).astype(F32) == g_star)
    el = jnp.where(in_grp, logit, NEG)
    e_max = jnp.max(el, axis=-1, keepdims=True)
    ee = jnp.where(in_grp, jnp.exp(el - e_max), 0.0)
    ep = jnp.where(in_grp, ee / jnp.sum(ee, axis=-1, keepdims=True), -1.0)
    p1 = jnp.max(ep, axis=-1, keepdims=True)
    i1 = jnp.min(jnp.where(ep == p1, lane_f, 1e9), axis=-1, keepdims=True)
    ep2 = jnp.where(lane_f == i1, -1.0, ep)
    p2 = jnp.max(ep2, axis=-1, keepdims=True)
    i2 = jnp.min(jnp.where(ep2 == p2, lane_f, 1e9), axis=-1, keepdims=True)
    tot = p1 + p2
    comb_ref[...] = (jnp.where(lane_f == i1, p1 / tot * g_prob, 0.0)
                     + jnp.where(lane_f == i2, p2 / tot * g_prob, 0.0)
                     + jnp.where(lane == GROUP_LANE, g_star, 0.0))


def _finish(x, attn_o, rnn_o, w_out, g_ffn, w_router, b_router, tm):
    n = x.shape[0]
    row = lambda i: (i, 0)
    fixed = lambda i: (0, 0)
    return pl.pallas_call(
        _finish_kernel,
        grid=(n // tm,),
        in_specs=[pl.BlockSpec((tm, D_MODEL), row), pl.BlockSpec((tm, D_ATTN), row), pl.BlockSpec((tm, D_RNN), row),
                  pl.BlockSpec((D_MODEL, D_MODEL), fixed, pipeline_mode=pl.Buffered(1)),
                  pl.BlockSpec((1, D_MODEL), fixed),
                  pl.BlockSpec((D_MODEL, LANES), fixed), pl.BlockSpec((1, LANES), fixed)],
        out_specs=[pl.BlockSpec((tm, D_MODEL), row), pl.BlockSpec((tm, D_MODEL), row), pl.BlockSpec((tm, LANES), row)],
        out_shape=[jax.ShapeDtypeStruct((n, D_MODEL), F32), jax.ShapeDtypeStruct((n, D_MODEL), BF16),
                   jax.ShapeDtypeStruct((n, LANES), F32)],
        compiler_params=pltpu.CompilerParams(dimension_semantics=("arbitrary",), vmem_limit_bytes=VMEM_LIMIT),
        name="finish",
    )(x, attn_o, rnn_o, w_out, g_ffn, w_router, b_router)


MOE_SUB = 128


MOE_SORT = 512


def _moe_kernel(sd, xn_ref, comb_ref, h_ref, wg_ref, wu_ref, wd_ref, gf_ref, y_ref, xs_scr, cs_scr, pt_scr, ends_smem):
    e = pl.program_id(1)
    tm = xn_ref.shape[0]
    domains = [slice(d * sd, (d + 1) * sd) for d in range(tm // sd)]

    @pl.when(e == 0)
    def _sort_rows():
        for d, dom in enumerate(domains):
            comb = comb_ref[dom, :]
            lane = lax.broadcasted_iota(jnp.int32, comb.shape, 1)
            grp = jnp.sum(jnp.where(lane == GROUP_LANE, comb, 0.0), axis=-1, keepdims=True)
            onehot = jnp.where(lane.astype(F32) == grp, 1.0, 0.0)
            earlier = jnp.where(lax.broadcasted_iota(jnp.int32, (sd, sd), 1)
                                < lax.broadcasted_iota(jnp.int32, (sd, sd), 0), 1.0, 0.0).astype(BF16)
            before = _dot(earlier, onehot.astype(BF16))
            rank = jnp.sum(onehot * before, axis=-1, keepdims=True)
            count = jnp.sum(onehot, axis=0, keepdims=True)
            end = jnp.int32(0)
            ends_smem[d, 0] = end
            for g in range(N_GROUPS):
                end = end + jnp.sum(jnp.where(lane[0:1] == g, count, 0.0)).astype(jnp.int32)
                ends_smem[d, g + 1] = end
            first = jnp.sum(jnp.where(lane.astype(F32) < grp, count, 0.0), axis=-1, keepdims=True)
            pos = first + rank
            to_sorted_t = jnp.where(lax.broadcasted_iota(jnp.int32, (sd, sd), 1).astype(F32) == pos, 1.0, 0.0)
            pt_scr[d] = to_sorted_t.astype(BF16)
            to_sorted = to_sorted_t.T.astype(BF16)
            xs_scr[dom, :] = _dot(to_sorted, xn_ref[dom, :]).astype(BF16)
            c1, c2, c3 = _split3(comb)
            cs_scr[dom, :] = (_dot(to_sorted, c3) + _dot(to_sorted, c2)) + _dot(to_sorted, c1)
        y_ref[...] = jnp.zeros(y_ref.shape, F32)

    grp_e = e // EXP_PER_GROUP
    for d in range(tm // sd):
        lo = ends_smem[d, grp_e]
        hi = ends_smem[d, grp_e + 1]
        start = lo // MOE_SUB * MOE_SUB
        n_blk = (hi - start + MOE_SUB - 1) // MOE_SUB
        for k in range(1, sd // MOE_SUB + 1):
            @pl.when(jnp.logical_and(hi > lo, n_blk == k))
            def _():
                rows = pl.ds(pl.multiple_of(d * sd + start, MOE_SUB), k * MOE_SUB)
                lane = lax.broadcasted_iota(jnp.int32, (k * MOE_SUB, LANES), 1)
                cw = jnp.sum(jnp.where(lane == e, cs_scr[rows, :], 0.0), axis=-1, keepdims=True)
                x = xs_scr[rows, :]
                hh = jax.nn.silu(_dot(x, wg_ref[0])) * _dot(x, wu_ref[0]) * cw
                y_ref[rows, :] += _dot(hh.astype(BF16), wd_ref[0])

    @pl.when(e == pl.num_programs(1) - 1)
    def _():
        for d, dom in enumerate(domains):
            a1, a2, _ = _split3(y_ref[dom, :])
            ffn = _dot(pt_scr[d], a2) + _dot(pt_scr[d], a1)
            y_ref[dom, :] = _rms(h_ref[dom, :] + ffn, gf_ref[...])


def _moe(xn, comb, h, w_gate, w_up, w_down, g_final, tm):
    n = xn.shape[0]
    sd = min(tm, MOE_SORT)
    row = lambda i, e: (i, 0)
    return pl.pallas_call(
        functools.partial(_moe_kernel, sd),
        grid=(n // tm, N_EXPERTS),
        in_specs=[pl.BlockSpec((tm, D_MODEL), row), pl.BlockSpec((tm, LANES), row),
                  pl.BlockSpec((tm, D_MODEL), row, pipeline_mode=pl.Buffered(1)),
                  pl.BlockSpec((1, D_MODEL, D_EXPERT), lambda i, e: (e, 0, 0)),
                  pl.BlockSpec((1, D_MODEL, D_EXPERT), lambda i, e: (e, 0, 0)),
                  pl.BlockSpec((1, D_EXPERT, D_MODEL), lambda i, e: (e, 0, 0)),
                  pl.BlockSpec((1, D_MODEL), lambda i, e: (0, 0))],
        out_specs=pl.BlockSpec((tm, D_MODEL), row),
        out_shape=jax.ShapeDtypeStruct((n, D_MODEL), F32),
        scratch_shapes=[pltpu.VMEM((tm, D_MODEL), BF16), pltpu.VMEM((tm, LANES), F32),
                        pltpu.VMEM((tm // sd, sd, sd), BF16), pltpu.SMEM((tm // sd, N_GROUPS + 1), jnp.int32)],
        compiler_params=pltpu.CompilerParams(dimension_semantics=("arbitrary", "arbitrary"),
                                             vmem_limit_bytes=VMEM_LIMIT),
        name="moe",
    )(xn, comb, h, w_gate, w_up, w_down, g_final)


def _block_diag_tiles(w):
    per = 256 // RNN_BLOCK_DIM
    w4 = w.reshape(RNN_BLOCKS // per, per, RNN_BLOCK_DIM, RNN_BLOCK_DIM)
    eye = jnp.eye(per, dtype=w.dtype)
    tiles = jnp.einsum('tpde,pq->tpdqe', w4, eye)
    return tiles.reshape(RNN_BLOCKS // per, 256, 256)


def _layer(l, xp, xs, cache_cmp_kv, cache_sel_kv, cache_win_kv, state_conv, state_h, page_table,
           norm_mix, w_in, cmp_pool_w, conv_w, conv_b, lru_wa, lru_ba, lru_wx, lru_bx, lru_lambda, w_out,
           norm_ffn, w_router_group, b_router_group, w_router_expert, b_router_expert,
           w_exp_gate, w_exp_up, w_exp_down, final_gain):
    bp, sp, _ = xp.shape
    bs, ts, _ = xs.shape
    n_pages = page_table.shape[1]
    past = n_pages * PAGE_SIZE

    wi = w_in[l]
    gt_cols = wi[:, 2560:2584].reshape(D_MODEL, N_KV, GQA * 3)
    gt_cols = jnp.pad(gt_cols, ((0, 0), (0, 0), (0, LANES - GQA * 3))).reshape(D_MODEL, N_KV * LANES)
    w_proj_f32 = jnp.concatenate([wi[:, :2560], wi[:, 2584:], gt_cols], axis=1)
    w_proj = w_proj_f32.astype(BF16)
    g_mix = norm_mix[l].reshape(1, D_MODEL)
    wfull = jnp.tile(jnp.repeat(cmp_pool_w[l], HEAD_DIM, axis=1), (1, N_KV))
    w2 = jnp.broadcast_to(cmp_pool_w[l].T[:, :, None], (2, CMP_BLOCK, HEAD_DIM))
    wa_f32 = _block_diag_tiles(lru_wa[l])
    wx_f32 = _block_diag_tiles(lru_wx[l])
    row = lambda v: v.reshape(1, -1)
    slopes = jnp.exp2(-8.0 * jnp.arange(1, N_HEADS + 1, dtype=F32) / N_HEADS)
    w_o = w_out[l].astype(BF16)
    w_router = jnp.pad(jnp.concatenate([w_router_expert[l], w_router_group[l]], axis=1),
                       ((0, 0), (0, LANES - N_EXPERTS - N_GROUPS)))
    b_router = jnp.pad(jnp.concatenate([b_router_expert[l], b_router_group[l]]),
                       (0, LANES - N_EXPERTS - N_GROUPS)).reshape(1, LANES)
    wg, wu, wd = w_exp_gate[l].astype(BF16), w_exp_up[l].astype(BF16), w_exp_down[l].astype(BF16)
    lru = lambda wa, wx: (conv_w[l], row(conv_b[l]), wa, row(lru_ba[l]), wx, row(lru_bx[l]), row(lru_lambda[l]))

    def tail(x, attn_o, rnn_o, w_out_l, tm_f, tm_m):
        h, xn, comb = _finish(x, attn_o, rnn_o, w_out_l, row(norm_ffn[l]), w_router, b_router, tm_f)
        return _moe(xn, comb, h, wg, wu, wd, final_gain, tm_m)

    np_ = bp * sp
    q, kc, ks, kw, xr, xg, gt, kc4, ks4, kw4 = _project(xp.reshape(np_, D_MODEL), g_mix, w_proj, 256)
    shp = lambda a: a.reshape(bp, sp, a.shape[-1])
    ident = jnp.arange(np_ // 1024, dtype=jnp.int32)
    cmp_p = _compress(ident, kc.reshape(np_ // 1024, 1024, KV_W), wfull, bp, sp // 1024, 1, 1024)
    attn_p = _nsa_prompt(slopes, shp(q), shp(gt), cmp_p, shp(ks), shp(kw))
    rnn_p, h_p = _rglru(shp(xr), shp(xg), jnp.zeros((bp, 8, D_RNN), F32), jnp.zeros((bp, 1, D_RNN), F32),
                       *lru(wa_f32.astype(BF16), wx_f32.astype(BF16)), tc=512)
    y_p = tail(xp.reshape(np_, D_MODEL), attn_p.reshape(np_, D_ATTN), rnn_p.reshape(np_, D_RNN), w_o, 512, 1024)
    kv6 = lambda a, b_, t_: a.reshape(b_, t_, N_KV, 2, HEAD_DIM)
    outs_p = (y_p.reshape(bp, sp, D_MODEL), kv6(kc4, bp, sp), kv6(ks4, bp, sp),
              kv6(kw4, bp, sp)[:, -min(WINDOW, sp):], shp(xr)[:, sp - (CONV_W - 1):], h_p.reshape(bp, D_RNN))

    ns_ = bs * ts
    q, kc, ks, kw, xr, xg, gt = _project_precise(xs.reshape(ns_, D_MODEL), g_mix, w_proj_f32)
    shs = lambda a: a.reshape(bs, ts, a.shape[-1])
    cmp_s = _compress_paged(page_table.reshape(-1), cache_cmp_kv[l].reshape(-1, HEAD_DIM), w2, bs, n_pages, 16)
    idx, o_cw = _nsa_sample_a(past, shs(q), shs(gt), cmp_s, cache_win_kv[l].reshape(-1, HEAD_DIM), shs(kw))
    idx = idx.reshape(bs, N_KV, ts, LANES)[..., :N_GATHER].reshape(-1)
    gate_sel = gt.reshape(ns_, N_KV, LANES)[:, :, :GQA * 3].reshape(ns_, N_HEADS, 3)[:, :, 1:2]
    gate_sel = jnp.broadcast_to(gate_sel, (ns_, N_HEADS, HEAD_DIM))
    slopes8 = jnp.broadcast_to(slopes.reshape(N_HEADS, 1), (N_HEADS, LANES))
    new_blocks = jnp.pad(ks.reshape(bs, ts * KV_ROWS, HEAD_DIM), ((0, 0), (0, BLOCK_ROWS - ts * KV_ROWS), (0, 0)))
    attn_s = _nsa_sample_b(past, idx, page_table.reshape(-1), q.reshape(ns_, N_HEADS, HEAD_DIM), gate_sel, slopes8,
                           o_cw.reshape(ns_, N_HEADS, HEAD_DIM), new_blocks, cache_sel_kv[l].reshape(-1, HEAD_DIM))
    conv_rows = jnp.pad(state_conv[l], ((0, 0), (ts - (CONV_W - 1), 0), (0, 0))).reshape(ns_, D_RNN)
    rnn_s, h_all = _rglru_short(xr, xg, conv_rows, jnp.repeat(state_h[l], ts, axis=0), *lru(wa_f32, wx_f32), t=ts)
    h_s = h_all.reshape(bs, ts, D_RNN)[:, -1]
    y_s = tail(xs.reshape(ns_, D_MODEL), attn_s.reshape(ns_, D_ATTN), rnn_s, w_out[l], ns_, ns_)
    win_s = jnp.concatenate([cache_win_kv[l], kv6(kw, bs, ts)], axis=1)[:, ts:]
    conv_s = jnp.concatenate([state_conv[l], shs(xr)], axis=1)[:, ts:]
    outs_s = (y_s.reshape(bs, ts, D_MODEL), kv6(kc, bs, ts), kv6(ks, bs, ts),
              win_s, conv_s, h_s.reshape(bs, D_RNN))
    return outs_p, outs_s


def kernel(x_prompt, x_sample, cache_cmp_kv, cache_sel_kv, cache_win_kv, state_conv, state_h, page_table, norm_mix, w_in, cmp_pool_w, conv_w, conv_b, lru_wa, lru_ba, lru_wx, lru_bx, lru_lambda, w_out, norm_ffn, w_router_group, b_router_group, w_router_expert, b_router_expert, w_exp_gate, w_exp_up, w_exp_down, norm_final):
    depth = w_in.shape[0]
    assert depth == 1, "the final norm is fused into the single layer's expert kernel"
    p, s = _layer(0, x_prompt, x_sample, cache_cmp_kv, cache_sel_kv, cache_win_kv, state_conv, state_h, page_table,
                  norm_mix, w_in, cmp_pool_w, conv_w, conv_b, lru_wa, lru_ba, lru_wx, lru_bx, lru_lambda, w_out,
                  norm_ffn, w_router_group, b_router_group, w_router_expert, b_router_expert,
                  w_exp_gate, w_exp_up, w_exp_down, norm_final.reshape(1, D_MODEL))
    st = lambda a: a[None]
    return (p[0], s[0], st(p[1]), st(s[1]), st(p[2]), st(s[2]), st(p[3]), st(s[3]),
            st(p[4]), st(s[4]), st(p[5]), st(s[5]))
```
